```python
import jax, jax.numpy as jnp
from jax import lax
import numpy as np

D_MODEL = 1024
BATCH = 2
SEQ = 8192
DEPTH = 2
DEC_BATCH = 32
DEC_SEQ = 64
PAST_LEN = 4096

CHUNK = 64
PLE_DIM = 256
D_FF = 4 * D_MODEL
RET_HEADS = 4
RET_DK = 128
RET_DV = 128
DN_HEADS = 4
DN_DK = 128
DN_DV = 128
CONV_W = 4
ROPE_BASE = 10000.0
EPS = 1e-6
RET_W = RET_HEADS * RET_DV
DN_W = DN_HEADS * DN_DV
MIX_W = RET_W + DN_W
DN_CONV_CH = DN_HEADS * (2 * DN_DK + DN_DV)
IN_COLS = 2 * RET_HEADS * RET_DK + RET_W + RET_W + DN_CONV_CH + DN_W + 2 * DN_HEADS

kernel_name = 'hybrid_retention_gated_deltanet_stream_step'


def _rmsnorm(x, g):
    xf = x.astype(jnp.float32)
    y = xf * lax.rsqrt(jnp.mean(xf * xf, axis=-1, keepdims=True) + EPS)
    return (y * g.astype(jnp.float32)).astype(x.dtype)


def _head_layernorm(o, g):
    mu = jnp.mean(o, axis=-1, keepdims=True)
    oc = o - mu
    var = jnp.mean(oc * oc, axis=-1, keepdims=True)
    return oc * lax.rsqrt(var + EPS) * g.astype(jnp.float32).reshape(o.shape[-2:])


def _l2norm(x):
    return x * lax.rsqrt(jnp.sum(x * x, axis=-1, keepdims=True) + EPS)


def _rope(x, pos):
    half = x.shape[-1] // 2
    inv = ROPE_BASE ** (-jnp.arange(half, dtype=jnp.float32) / half)
    ang = pos.astype(jnp.float32)[:, None] * inv[None, :]
    cos = jnp.cos(ang)[None, :, None, :]
    sin = jnp.sin(ang)[None, :, None, :]
    x1, x2 = x[..., :half], x[..., half:]
    return jnp.concatenate([x1 * cos - x2 * sin, x1 * sin + x2 * cos], axis=-1)


def _causal_conv(x, buf, w):
    T = x.shape[1]
    xp = jnp.concatenate([buf.astype(x.dtype), x], axis=1)
    y = xp[:, 0:T] * w[0]
    for i in range(1, CONV_W):
        y = y + xp[:, i:i + T] * w[i]
    return y, xp[:, -(CONV_W - 1):]


def _retention(q, k, v, s0):
    B, T, H, DK = q.shape
    DV = v.shape[-1]
    C = min(CHUNK, T)
    N = T // C
    log_gamma = jnp.log(1.0 - 2.0 ** (-5.0 - jnp.arange(H, dtype=jnp.float32)))
    idx = jnp.arange(C, dtype=jnp.float32)
    intra = jnp.exp(jnp.abs(idx[:, None] - idx[None, :])[None] * log_gamma[:, None, None])
    cross = jnp.exp((idx + 1.0)[None, :] * log_gamma[:, None])
    inject = jnp.exp((C - 1.0 - idx)[None, :] * log_gamma[:, None])
    carry = jnp.exp(C * log_gamma)
    qc = q.reshape(B, N, C, H, DK)
    kc = k.reshape(B, N, C, H, DK)
    vc = v.reshape(B, N, C, H, DV)
    scores = jnp.einsum('bnihd,bnjhd->bnhij', qc, kc) * intra
    o = jnp.einsum('bnhij,bnjhv->bnihv', scores, vc)
    u = jnp.einsum('bnjhd,bnjhv,hj->nbhdv', kc, vc, inject)

    def step(s, u_n):
        return s * carry[:, None, None] + u_n, s

    s_final, s_prev = lax.scan(step, s0, u)
    o = o + jnp.einsum('bnihd,nbhdv,hi->bnihv', qc, s_prev, cross)
    return o.reshape(B, T, H, DV), s_final


def _gated_delta(q, k, v, g, beta, s0):
    B, T, H, DK = q.shape
    DV = v.shape[-1]
    C = min(CHUNK, T)
    N = T // C

    def chunks(x):
        x = x.reshape((B, N, C, H) + x.shape[3:])
        return jnp.moveaxis(x, 3, 2)

    qc, kc, vc = chunks(q), chunks(k), chunks(v)
    gc = jnp.cumsum(chunks(g), axis=-1)
    bc = chunks(beta)
    idx = jnp.arange(C)
    causal = idx[:, None] >= idx[None, :]
    strict = idx[:, None] > idx[None, :]
    decay = jnp.exp(jnp.where(causal, gc[..., :, None] - gc[..., None, :], -jnp.inf))
    kb = kc * bc[..., None]
    a = jnp.where(strict, jnp.einsum('bnhid,bnhjd->bnhij', kb, kc) * decay, 0.0)
    lhs = a + jnp.eye(C, dtype=a.dtype)
    rhs = jnp.concatenate([vc * bc[..., None], kb * jnp.exp(gc)[..., None]], axis=-1)
    sol = lax.linalg.triangular_solve(lhs, rhs, left_side=True, lower=True, unit_diagonal=True)
    u, w = sol[..., :DV], sol[..., DV:]
    attn = jnp.einsum('bnhid,bnhjd->bnhij', qc, kc) * decay
    qg = qc * jnp.exp(gc)[..., None]
    kg = kc * jnp.exp(gc[..., -1:] - gc)[..., None]
    g_end = jnp.exp(gc[..., -1])
    xs = tuple(jnp.moveaxis(t, 1, 0) for t in (u, w, attn, qg, kg, g_end))

    def step(s, inp):
        u_n, w_n, attn_n, qg_n, kg_n, ge_n = inp
        v_new = u_n - jnp.einsum('bhcd,bhdv->bhcv', w_n, s)
        o_n = jnp.einsum('bhcd,bhdv->bhcv', qg_n, s) + jnp.einsum('bhij,bhjv->bhiv', attn_n, v_new)
        s = s * ge_n[..., None, None] + jnp.einsum('bhcd,bhcv->bhdv', kg_n, v_new)
        return s, o_n

    s_final, o = lax.scan(step, s0, xs)
    o = jnp.transpose(o, (1, 0, 3, 2, 4)).reshape(B, T, H, DV)
    return o, s_final


def _layer(h, p, conv_buf, s_ret, s_dn, pos0, norm1_g, w_in, conv_w, dt_bias, a_log, ret_norm_g,
           dn_norm_g, w_out, norm2_g, w_up, w_down, ple_proj, ple_norm_g, ple_gate_w, ple_gate_b):
    B, T, _ = h.shape
    f32 = jnp.float32
    xn = _rmsnorm(h, norm1_g)
    z = xn @ w_in
    sizes = [RET_HEADS * RET_DK, RET_HEADS * RET_DK, RET_W, RET_W, DN_CONV_CH, DN_W, DN_HEADS, DN_HEADS]
    offs = [int(o) for o in np.cumsum(sizes)[:-1]]
    q_a, k_a, v_a, g_a, qkv_b, g_b, b_b, a_b = jnp.split(z, offs, axis=-1)

    pos = pos0 + jnp.arange(T)
    q_a = _rope(q_a.reshape(B, T, RET_HEADS, RET_DK).astype(f32), pos)
    k_a = _rope(k_a.reshape(B, T, RET_HEADS, RET_DK).astype(f32), pos) * (RET_DK ** -0.5)
    v_a = v_a.reshape(B, T, RET_HEADS, RET_DV).astype(f32)
    o_a, s_ret_new = _retention(q_a, k_a, v_a, s_ret.astype(f32))
    o_a = _head_layernorm(o_a, ret_norm_g) * jax.nn.silu(g_a.astype(f32)).reshape(B, T, RET_HEADS, RET_DV)

    c, conv_new = _causal_conv(qkv_b, conv_buf, conv_w)
    c = jax.nn.silu(c.astype(f32))
    q_b, k_b, v_b = jnp.split(c, [DN_HEADS * DN_DK, 2 * DN_HEADS * DN_DK], axis=-1)
    q_b = _l2norm(q_b.reshape(B, T, DN_HEADS, DN_DK)) * (DN_DK ** -0.5)
    k_b = _l2norm(k_b.reshape(B, T, DN_HEADS, DN_DK))
    v_b = v_b.reshape(B, T, DN_HEADS, DN_DV)
    g = -jnp.exp(a_log.astype(f32)) * jax.nn.softplus(a_b.astype(f32) + dt_bias.astype(f32))
    beta = jax.nn.sigmoid(b_b.astype(f32))
    o_b, s_dn_new = _gated_delta(q_b, k_b, v_b, g, beta, s_dn.astype(f32))
    o_b = _rmsnorm(o_b, dn_norm_g) * jax.nn.silu(g_b.astype(f32)).reshape(B, T, DN_HEADS, DN_DV)

    mix = jnp.concatenate([o_a.reshape(B, T, RET_W), o_b.reshape(B, T, DN_W)], axis=-1).astype(h.dtype)
    h = h + mix @ w_out
    up = jax.nn.relu(_rmsnorm(h, norm2_g) @ w_up)
    h = h + (up * up) @ w_down
    gate = jax.nn.sigmoid(_rmsnorm(h, ple_norm_g) @ ple_gate_w + ple_gate_b)
    h = h + (p @ ple_proj) * gate
    return h, s_ret_new.astype(s_ret.dtype), s_dn_new.astype(s_dn.dtype), conv_new


def _trunk(x, p, conv_bufs, s_rets, s_dns, pos0, layer_weights, final_norm_g):
    h = x
    new_ret, new_dn, new_conv = [], [], []
    for l in range(DEPTH):
        lw = [wt[l] for wt in layer_weights]
        h, sr, sd, cb = _layer(h, p[l], conv_bufs[l], s_rets[l], s_dns[l], pos0, *lw)
        new_ret.append(sr)
        new_dn.append(sd)
        new_conv.append(cb)
    y = _rmsnorm(h, final_norm_g)
    return y, jnp.stack(new_ret), jnp.stack(new_dn), jnp.stack(new_conv)


def setup_inputs(seed: int = 0) -> dict:
    key = jax.random.key(seed)
    ks = jax.random.split(key, 24)
    f32 = jnp.float32

    def nrm(k, shape, scale):
        return jax.random.normal(k, shape, f32) * scale

    x_prompt = nrm(ks[0], (BATCH, SEQ, D_MODEL), 1.0)
    x_sample = nrm(ks[1], (DEC_BATCH, DEC_SEQ, D_MODEL), 1.0)
    p_prompt = nrm(ks[2], (DEPTH, BATCH, SEQ, PLE_DIM), 1.0)
    p_sample = nrm(ks[3], (DEPTH, DEC_BATCH, DEC_SEQ, PLE_DIM), 1.0)
    state_ret = nrm(ks[4], (DEPTH, DEC_BATCH, RET_HEADS, RET_DK, RET_DV), 0.1)
    state_delta = nrm(ks[5], (DEPTH, DEC_BATCH, DN_HEADS, DN_DK, DN_DV), 0.1)
    state_conv = nrm(ks[6], (DEPTH, DEC_BATCH, CONV_W - 1, DN_CONV_CH), 1.0)
    norm1_g = 1.0 + nrm(ks[7], (DEPTH, D_MODEL), 0.02)
    w_in = nrm(ks[8], (DEPTH, D_MODEL, IN_COLS), D_MODEL ** -0.5)
    conv_w = nrm(ks[9], (DEPTH, CONV_W, DN_CONV_CH), CONV_W ** -0.5)
    dt = jnp.exp(jax.random.uniform(ks[10], (DEPTH, DN_HEADS), f32, np.log(1e-3), np.log(1e-1)))
    dt_bias = dt + jnp.log(-jnp.expm1(-dt))
    a_log = jnp.log(jax.random.uniform(ks[11], (DEPTH, DN_HEADS), f32, 1.0, 16.0))
    ret_norm_g = 1.0 + nrm(ks[12], (DEPTH, RET_W), 0.02)
    dn_norm_g = 1.0 + nrm(ks[13], (DEPTH, DN_DV), 0.02)
    w_out = nrm(ks[14], (DEPTH, MIX_W, D_MODEL), MIX_W ** -0.5)
    norm2_g = 1.0 + nrm(ks[15], (DEPTH, D_MODEL), 0.02)
    w_up = nrm(ks[16], (DEPTH, D_MODEL, D_FF), D_MODEL ** -0.5)
    w_down = nrm(ks[17], (DEPTH, D_FF, D_MODEL), D_FF ** -0.5)
    ple_proj = nrm(ks[18], (DEPTH, PLE_DIM, D_MODEL), PLE_DIM ** -0.5)
    ple_norm_g = 1.0 + nrm(ks[19], (DEPTH, D_MODEL), 0.02)
    ple_gate_w = nrm(ks[20], (DEPTH, D_MODEL, D_MODEL), D_MODEL ** -0.5)
    ple_gate_b = nrm(ks[21], (DEPTH, D_MODEL), 0.01)
    final_norm_g = 1.0 + nrm(ks[22], (D_MODEL,), 0.02)
    return {'x_prompt': x_prompt, 'x_sample': x_sample, 'p_prompt': p_prompt, 'p_sample': p_sample,
            'state_ret': state_ret, 'state_delta': state_delta, 'state_conv': state_conv,
            'norm1_g': norm1_g, 'w_in': w_in, 'conv_w': conv_w, 'dt_bias': dt_bias, 'a_log': a_log,
            'ret_norm_g': ret_norm_g, 'dn_norm_g': dn_norm_g, 'w_out': w_out, 'norm2_g': norm2_g,
            'w_up': w_up, 'w_down': w_down, 'ple_proj': ple_proj, 'ple_norm_g': ple_norm_g,
            'ple_gate_w': ple_gate_w, 'ple_gate_b': ple_gate_b, 'final_norm_g': final_norm_g}


def reference(x_prompt, x_sample, p_prompt, p_sample, state_ret, state_delta, state_conv,
              norm1_g, w_in, conv_w, dt_bias, a_log, ret_norm_g, dn_norm_g, w_out, norm2_g,
              w_up, w_down, ple_proj, ple_norm_g, ple_gate_w, ple_gate_b, final_norm_g):
    layer_weights = (norm1_g, w_in, conv_w, dt_bias, a_log, ret_norm_g, dn_norm_g, w_out, norm2_g,
                     w_up, w_down, ple_proj, ple_norm_g, ple_gate_w, ple_gate_b)
    bp = x_prompt.shape[0]
    dt_ = x_prompt.dtype
    zero_ret = jnp.zeros((DEPTH, bp, RET_HEADS, RET_DK, RET_DV), dt_)
    zero_dn = jnp.zeros((DEPTH, bp, DN_HEADS, DN_DK, DN_DV), dt_)
    zero_conv = jnp.zeros((DEPTH, bp, CONV_W - 1, DN_CONV_CH), dt_)
    y_prompt, ret_p, dn_p, conv_p = _trunk(x_prompt, p_prompt, zero_conv, zero_ret, zero_dn, 0,
                                           layer_weights, final_norm_g)
    y_sample, ret_s, dn_s, conv_s = _trunk(x_sample, p_sample, state_conv, state_ret, state_delta, PAST_LEN,
                                           layer_weights, final_norm_g)
    return (y_prompt, y_sample, ret_p, dn_p, conv_p, ret_s, dn_s, conv_s)
```

```python
import functools

import jax
import jax.numpy as jnp
import numpy as np
from jax import lax
from jax.experimental import pallas as pl
from jax.experimental.pallas import tpu as pltpu

F32 = jnp.float32
BF16 = jnp.bfloat16

D_MODEL = 1024
CHUNK = 64
HEADS = 4
HEAD_DIM = 128
HEAD_W = HEADS * HEAD_DIM
PLE_DIM = 256
D_FF = 4 * D_MODEL
CONV_W = 4
CONV_CH = 3 * HEAD_W
ROPE_BASE = 10000.0
EPS = 1e-6
Z_MAIN = 8 * HEAD_W
LANE = 128
SUBLANE = 8
Z_COLS = Z_MAIN + LANE
CONV_PAD = SUBLANE
VMEM_LIMIT = 56 * 1024 * 1024


def _bdot(a, b):
    return jnp.dot(a.astype(BF16), b.astype(BF16), preferred_element_type=F32)


def _bdot_nt(a, b):
    return lax.dot_general(a.astype(BF16), b.astype(BF16), (((1,), (1,)), ((), ())),
                           preferred_element_type=F32)


def _bdot_tn(a, b):
    return lax.dot_general(a.astype(BF16), b.astype(BF16), (((0,), (0,)), ((), ())),
                           preferred_element_type=F32)


def _split3(x):
    hi = x.astype(BF16)
    r = x - hi.astype(F32)
    mid = r.astype(BF16)
    lo = (r - mid.astype(F32)).astype(BF16)
    return hi, mid, lo


def _split2(x):
    hi = x.astype(BF16)
    lo = (x - hi.astype(F32)).astype(BF16)
    return hi, lo


def _dot_hp(a, b):
    ah, al = _split2(a)
    bh, bl = _split2(b)
    d = functools.partial(jnp.dot, preferred_element_type=F32)
    return d(ah, bh) + (d(ah, bl) + d(al, bh))


def _rms(x, g):
    return x * lax.rsqrt(jnp.mean(x * x, axis=-1, keepdims=True) + EPS) * g


def _softplus(x):
    return jnp.maximum(x, 0.0) + jnp.log1p(jnp.exp(-jnp.abs(x)))


def _sigmoid(x):
    return 1.0 / (1.0 + jnp.exp(-x))


def _silu(x):
    return x * _sigmoid(x)


def _in_proj_kernel(x_ref, g_ref, w_ref, wt_ref, z_ref, zt_ref):
    xn = _rms(x_ref[...], g_ref[...]).astype(BF16)
    z_ref[...] = jnp.dot(xn, w_ref[...], preferred_element_type=F32)
    zt_ref[...] = lax.dot_general(wt_ref[...], xn, (((1,), (1,)), ((), ())),
                                  preferred_element_type=F32)


def _in_proj(x, g, w, wt, tm):
    m = x.shape[0]
    return pl.pallas_call(
        _in_proj_kernel,
        grid=(m // tm,),
        in_specs=[
            pl.BlockSpec((tm, D_MODEL), lambda i: (i, 0)),
            pl.BlockSpec((1, D_MODEL), lambda i: (0, 0)),
            pl.BlockSpec((D_MODEL, Z_COLS), lambda i: (0, 0)),
            pl.BlockSpec((SUBLANE, D_MODEL), lambda i: (0, 0)),
        ],
        out_specs=[
            pl.BlockSpec((tm, Z_COLS), lambda i: (i, 0)),
            pl.BlockSpec((SUBLANE, tm), lambda i: (0, i)),
        ],
        out_shape=[
            jax.ShapeDtypeStruct((m, Z_COLS), F32),
            jax.ShapeDtypeStruct((SUBLANE, m), F32),
        ],
        compiler_params=pltpu.CompilerParams(
            dimension_semantics=("arbitrary",), vmem_limit_bytes=VMEM_LIMIT),
        name="in_proj",
    )(x, g, w, wt)


def _unit_lower_inverse(a):
    n = a.shape[0]
    ri = lax.broadcasted_iota(jnp.int32, (n, n), 0)
    ci = lax.broadcasted_iota(jnp.int32, (n, n), 1)
    eye = (ri == ci).astype(F32)
    same16 = (ri // 16) == (ci // 16)
    same32 = (ri // 32) == (ci // 32)
    d = jnp.where(same16, a, 0.0)
    p = eye - d
    dk = d
    for _ in range(3):
        dk = _dot_hp(dk, dk)
        p = p + _dot_hp(p, dk)
    e32 = jnp.where(same32 & jnp.logical_not(same16), a, 0.0)
    p = p - _dot_hp(p, _dot_hp(e32, p))
    e64 = jnp.where(same32, 0.0, a)
    p = p - _dot_hp(p, _dot_hp(e64, p))
    return p


def _mixer_kernel(qa_ref, ka_ref, va_ref, ga_ref, qb_ref, kb_ref, vb_ref, gb_ref, ba_ref, zt_ref,
                  cos_ref, sin_ref, intra_ref, rc_ref, convw_ref, hpc_ref, hpr_ref,
                  retg_ref, dng_ref, sret0_ref, sdn0_ref, conv0_ref,
                  mix_ref, sret_out, sdn_out, conv_out,
                  sret_s, sdn_s, conv_s, *, bb, cb):
    step = pl.program_id(1)
    blk = cb * CHUNK

    @pl.when(step == 0)
    def _():
        sret_s[...] = sret0_ref[...]
        sdn_s[...] = sdn0_ref[...]
        for j in range(3):
            conv_s[:, j, CONV_PAD - 3:CONV_PAD, :] = conv0_ref[:, :, j * HEAD_W:(j + 1) * HEAD_W]

    ri = lax.broadcasted_iota(jnp.int32, (CHUNK, CHUNK), 0)
    ci = lax.broadcasted_iota(jnp.int32, (CHUNK, CHUNK), 1)
    causal = ri >= ci
    strict = ri > ci
    ltri = causal.astype(BF16)
    utri = (ri <= ci).astype(BF16)

    hpc = hpc_ref[...]
    hpr = hpr_ref[...]
    neg_a_col = -jnp.exp(hpc[0:1, :])
    neg_a_row = -jnp.exp(hpr[:, 0:1])
    rc = rc_ref[...]
    dn_g = dng_ref[...]
    ret_g = retg_ref[...]

    for b in range(bb):
        for j, ref in enumerate((qb_ref, kb_ref, vb_ref)):
            conv_s[b, j, CONV_PAD:CONV_PAD + blk, :] = ref[b]

    for c in range(cb):
        r0 = c * CHUNK
        rows = pl.ds(r0, CHUNK)
        cosv = cos_ref[rows, :]
        sinv = sin_ref[rows, :]
        for b in range(bb):
            ba = ba_ref[b, rows, :]
            zt = zt_ref[b, :, r0:r0 + CHUNK]
            beta_col = _sigmoid(ba)
            g_col = neg_a_col * _softplus(ba + hpc[1:2, :])
            g_row = neg_a_row * _softplus(zt + hpr[:, 1:2])
            gh, gm, gl = _split3(g_col)
            d = functools.partial(jnp.dot, preferred_element_type=F32)
            gc_col = d(ltri, gh) + (d(ltri, gm) + d(ltri, gl))
            gh, gm, gl = _split3(g_row)
            gc_row = d(gh, utri) + (d(gm, utri) + d(gl, utri))

            conv = []
            for j in range(3):
                acc = None
                for i in range(CONV_W):
                    term = (conv_s[b, j, pl.ds(CONV_PAD - 3 + r0 + i, CHUNK), :]
                            * convw_ref[i:i + 1, j * HEAD_W:(j + 1) * HEAD_W])
                    acc = term if acc is None else acc + term
                conv.append(_silu(acc))
            cq, ck, cv = conv

            qa = qa_ref[b, rows, :]
            ka = ka_ref[b, rows, :]
            va = va_ref[b, rows, :]
            ga = ga_ref[b, rows, :]
            gb = gb_ref[b, rows, :]

            for h in range(HEADS):
                hs = slice(h * HEAD_DIM, (h + 1) * HEAD_DIM)

                q = qa[:, hs]
                k = ka[:, hs]
                q = q * cosv + pltpu.roll(q, HEAD_DIM // 2, 1) * sinv
                k = (k * cosv + pltpu.roll(k, HEAD_DIM // 2, 1) * sinv) * (HEAD_DIM ** -0.5)
                v = va[:, hs]
                s_prev = sret_s[b, h]
                scores = _bdot_nt(q, k) * intra_ref[h]
                o = _bdot(scores, v) + _bdot(q, s_prev) * rc[:, h:h + 1]
                sret_s[b, h] = (s_prev * rc[0:1, 8 + h:9 + h]
                                + _bdot_tn(k, v * rc[:, 4 + h:5 + h]))
                mu = jnp.mean(o, axis=-1, keepdims=True)
                oc = o - mu
                var = jnp.mean(oc * oc, axis=-1, keepdims=True)
                o = oc * lax.rsqrt(var + EPS) * ret_g[:, hs] * _silu(ga[:, hs])
                mix_ref[b, rows, hs] = o

                q = cq[:, hs]
                k = ck[:, hs]
                v = cv[:, hs]
                q = q * lax.rsqrt(jnp.sum(q * q, axis=-1, keepdims=True) + EPS) * (HEAD_DIM ** -0.5)
                k = k * lax.rsqrt(jnp.sum(k * k, axis=-1, keepdims=True) + EPS)
                beta = beta_col[:, h:h + 1]
                gc_c = gc_col[:, 4 + h:5 + h]
                gc_r = gc_row[4 + h:5 + h, :]
                gc_end = gc_col[CHUNK - 1:CHUNK, 4 + h:5 + h]
                decay = jnp.exp(jnp.where(causal, gc_c - gc_r, -jnp.inf))
                e_gc = jnp.exp(gc_c)
                kbeta = k * beta
                kk = _bdot_nt(jnp.concatenate([kbeta, q], axis=0), k)
                a = jnp.where(strict, kk[:CHUNK] * decay, 0.0)
                attn = kk[CHUNK:] * decay
                tinv = _unit_lower_inverse(a)
                sol = _dot_hp(tinv, jnp.concatenate([v * beta, kbeta * e_gc], axis=1))
                u = sol[:, :HEAD_DIM]
                w = sol[:, HEAD_DIM:]
                s = sdn_s[b, h]
                ws = _bdot(jnp.concatenate([w, q * e_gc], axis=0), s)
                v_new = u - ws[:CHUNK]
                o = ws[CHUNK:] + _bdot(attn, v_new)
                kg = k * jnp.exp(gc_end - gc_c)
                sdn_s[b, h] = s * jnp.exp(gc_end) + _bdot_tn(kg, v_new)
                o = _rms(o, dn_g) * _silu(gb[:, hs])
                mix_ref[b, rows, HEAD_W + h * HEAD_DIM:HEAD_W + (h + 1) * HEAD_DIM] = o

    for b in range(bb):
        for j in range(3):
            conv_s[b, j, CONV_PAD - 3:CONV_PAD, :] = conv_s[b, j, CONV_PAD + blk - 3:CONV_PAD + blk, :]

    @pl.when(step == pl.num_programs(1) - 1)
    def _():
        sret_out[...] = sret_s[...]
        sdn_out[...] = sdn_s[...]
        for j in range(3):
            conv_out[:, :, j * HEAD_W:(j + 1) * HEAD_W] = conv_s[:, j, CONV_PAD - 3:CONV_PAD, :]


def _mixer(z, zt, cos_t, sin_t, intra, rc, conv_w, hpc, hpr, ret_g, dn_g, sret0, sdn0, conv0, bb, cb):
    bsz, t, _ = z.shape
    blk = cb * CHUNK
    grid = (bsz // bb, t // blk)

    def zspec(col):
        return pl.BlockSpec((bb, blk, HEAD_W), lambda i, j, col=col: (i, j, col))

    def const(shape):
        return pl.BlockSpec(shape, lambda i, j: (0,) * len(shape))

    state_spec = pl.BlockSpec((bb, HEADS, HEAD_DIM, HEAD_DIM), lambda i, j: (i, 0, 0, 0))
    conv_spec = pl.BlockSpec((bb, CONV_W - 1, CONV_CH), lambda i, j: (i, 0, 0))
    kern = functools.partial(_mixer_kernel, bb=bb, cb=cb)
    return pl.pallas_call(
        kern,
        grid=grid,
        in_specs=[zspec(c) for c in range(8)] + [
            pl.BlockSpec((bb, blk, LANE), lambda i, j: (i, j, Z_MAIN // LANE)),
            pl.BlockSpec((bb, SUBLANE, blk), lambda i, j: (i, 0, j)),
            pl.BlockSpec((blk, LANE), lambda i, j: (j, 0)),
            pl.BlockSpec((blk, LANE), lambda i, j: (j, 0)),
            const((HEADS, CHUNK, CHUNK)),
            const((CHUNK, LANE)),
            const((CONV_W, CONV_CH)),
            const((SUBLANE, LANE)),
            const((SUBLANE, LANE)),
            const((1, HEAD_W)),
            const((1, HEAD_DIM)),
            state_spec, state_spec, conv_spec,
        ],
        out_specs=[
            pl.BlockSpec((bb, blk, 2 * HEAD_W), lambda i, j: (i, j, 0)),
            state_spec, state_spec, conv_spec,
        ],
        out_shape=[
            jax.ShapeDtypeStruct((bsz, t, 2 * HEAD_W), F32),
            jax.ShapeDtypeStruct((bsz, HEADS, HEAD_DIM, HEAD_DIM), F32),
            jax.ShapeDtypeStruct((bsz, HEADS, HEAD_DIM, HEAD_DIM), F32),
            jax.ShapeDtypeStruct((bsz, CONV_W - 1, CONV_CH), F32),
        ],
        scratch_shapes=[
            pltpu.VMEM((bb, HEADS, HEAD_DIM, HEAD_DIM), F32),
            pltpu.VMEM((bb, HEADS, HEAD_DIM, HEAD_DIM), F32),
            pltpu.VMEM((bb, 3, CONV_PAD + blk, HEAD_W), F32),
        ],
        compiler_params=pltpu.CompilerParams(
            dimension_semantics=("arbitrary", "arbitrary"), vmem_limit_bytes=VMEM_LIMIT),
        name="mixer",
    )(*([z] * 9), zt, cos_t, sin_t, intra, rc, conv_w, hpc, hpr, ret_g, dn_g, sret0, sdn0, conv0)


def _post_kernel(h_ref, mix_ref, p_ref, wout_ref, n2g_ref, wup_ref, wdown_ref, proj_ref,
                 pleg_ref, gatew_ref, gateb_ref, fing_ref, out_ref, acc_s, xn_s, *, final):
    j = pl.program_id(1)

    @pl.when(j == 0)
    def _():
        h1 = h_ref[...] + jnp.dot(mix_ref[...].astype(BF16), wout_ref[...],
                                  preferred_element_type=F32)
        acc_s[...] = h1
        xn_s[...] = _rms(h1, n2g_ref[...]).astype(BF16)

    up = jnp.maximum(jnp.dot(xn_s[...], wup_ref[...], preferred_element_type=F32), 0.0)
    acc_s[...] += jnp.dot((up * up).astype(BF16), wdown_ref[...], preferred_element_type=F32)

    @pl.when(j == pl.num_programs(1) - 1)
    def _():
        h2 = acc_s[...]
        gate = _sigmoid(jnp.dot(_rms(h2, pleg_ref[...]).astype(BF16), gatew_ref[...],
                                preferred_element_type=F32) + gateb_ref[...])
        h3 = h2 + jnp.dot(p_ref[...].astype(BF16), proj_ref[...],
                          preferred_element_type=F32) * gate
        if final:
            h3 = _rms(h3, fing_ref[...])
        out_ref[...] = h3


def _post(h, mix, p, w_out, n2g, w_up, w_down, proj, pleg, gate_w, gate_b, fin_g, final, tm, tf):
    m = h.shape[0]

    def row(width):
        return pl.BlockSpec((tm, width), lambda i, j: (i, 0))

    def const(shape):
        return pl.BlockSpec(shape, lambda i, j: (0,) * len(shape))

    return pl.pallas_call(
        functools.partial(_post_kernel, final=final),
        grid=(m // tm, D_FF // tf),
        in_specs=[
            row(D_MODEL), row(2 * HEAD_W), row(PLE_DIM),
            const((2 * HEAD_W, D_MODEL)),
            const((1, D_MODEL)),
            pl.BlockSpec((D_MODEL, tf), lambda i, j: (0, j)),
            pl.BlockSpec((tf, D_MODEL), lambda i, j: (j, 0)),
            const((PLE_DIM, D_MODEL)),
            const((1, D_MODEL)),
            const((D_MODEL, D_MODEL)),
            const((1, D_MODEL)),
            const((1, D_MODEL)),
        ],
        out_specs=row(D_MODEL),
        out_shape=jax.ShapeDtypeStruct((m, D_MODEL), F32),
        scratch_shapes=[pltpu.VMEM((tm, D_MODEL), F32), pltpu.VMEM((tm, D_MODEL), BF16)],
        compiler_params=pltpu.CompilerParams(
            dimension_semantics=("arbitrary", "arbitrary"), vmem_limit_bytes=VMEM_LIMIT),
        name="post",
    )(h, mix, p, w_out, n2g, w_up, w_down, proj, pleg, gate_w, gate_b, fin_g)


def _rope_tables(pos0, t):
    half = HEAD_DIM // 2
    inv = ROPE_BASE ** (-jnp.arange(half, dtype=F32) / half)
    ang = (pos0 + jnp.arange(t)).astype(F32)[:, None] * inv[None, :]
    cos, sin = jnp.cos(ang), jnp.sin(ang)
    return jnp.concatenate([cos, cos], axis=-1), jnp.concatenate([-sin, sin], axis=-1)


def _retention_tables():
    log_gamma = jnp.log(1.0 - 2.0 ** (-5.0 - jnp.arange(HEADS, dtype=F32)))
    idx = jnp.arange(CHUNK, dtype=F32)
    intra = jnp.exp(jnp.abs(idx[:, None] - idx[None, :])[None] * log_gamma[:, None, None])
    cross = jnp.exp((idx + 1.0)[None, :] * log_gamma[:, None])
    inject = jnp.exp((CHUNK - 1.0 - idx)[None, :] * log_gamma[:, None])
    carry = jnp.exp(CHUNK * log_gamma)
    rc = jnp.zeros((CHUNK, LANE), F32)
    rc = rc.at[:, 0:HEADS].set(cross.T)
    rc = rc.at[:, HEADS:2 * HEADS].set(inject.T)
    rc = rc.at[:, 2 * HEADS:3 * HEADS].set(jnp.broadcast_to(carry[None, :], (CHUNK, HEADS)))
    return intra, rc


def _layer_params(l, norm1_g, w_in, conv_w, dt_bias, a_log, ret_norm_g, dn_norm_g, w_out, norm2_g,
                  w_up, w_down, ple_proj, ple_norm_g, ple_gate_w, ple_gate_b):
    w = w_in[l]
    w_pad = jnp.concatenate(
        [w, jnp.zeros((D_MODEL, Z_COLS - w.shape[1]), w.dtype)], axis=1).astype(BF16)
    wt = w[:, Z_MAIN:].T.astype(BF16)
    hpc = jnp.zeros((SUBLANE, LANE), F32)
    hpc = hpc.at[0, HEADS:2 * HEADS].set(a_log[l]).at[1, HEADS:2 * HEADS].set(dt_bias[l])
    hpr = jnp.zeros((SUBLANE, LANE), F32)
    hpr = hpr.at[HEADS:2 * HEADS, 0].set(a_log[l]).at[HEADS:2 * HEADS, 1].set(dt_bias[l])
    return dict(
        n1g=norm1_g[l][None, :], w_in=w_pad, w_in_t=wt, conv_w=conv_w[l], hpc=hpc, hpr=hpr,
        ret_g=ret_norm_g[l][None, :], dn_g=dn_norm_g[l][None, :],
        w_out=w_out[l].astype(BF16), n2g=norm2_g[l][None, :],
        w_up=w_up[l].astype(BF16), w_down=w_down[l].astype(BF16),
        proj=ple_proj[l].astype(BF16), pleg=ple_norm_g[l][None, :],
        gate_w=ple_gate_w[l].astype(BF16), gate_b=ple_gate_b[l][None, :])


def _trunk(x, p, conv0, sret0, sdn0, pos0, layers, fin_g, intra, rc, bb, cb, tm, tf):
    bsz, t, _ = x.shape
    m = bsz * t
    cos_t, sin_t = _rope_tables(pos0, t)
    h = x.reshape(m, D_MODEL)
    rets, dns, convs = [], [], []
    depth = len(layers)
    for l, lp in enumerate(layers):
        z, zt = _in_proj(h, lp["n1g"], lp["w_in"], lp["w_in_t"], tm)
        z = z.reshape(bsz, t, Z_COLS)
        zt = zt.reshape(SUBLANE, bsz, t).transpose(1, 0, 2)
        mix, sret, sdn, conv = _mixer(z, zt, cos_t, sin_t, intra, rc, lp["conv_w"], lp["hpc"],
                                      lp["hpr"], lp["ret_g"], lp["dn_g"], sret0[l], sdn0[l],
                                      conv0[l], bb, cb)
        h = _post(h, mix.reshape(m, 2 * HEAD_W), p[l].reshape(m, PLE_DIM), lp["w_out"], lp["n2g"],
                  lp["w_up"], lp["w_down"], lp["proj"], lp["pleg"], lp["gate_w"], lp["gate_b"],
                  fin_g, l == depth - 1, tm, tf)
        rets.append(sret)
        dns.append(sdn)
        convs.append(conv)
    return h.reshape(bsz, t, D_MODEL), jnp.stack(rets), jnp.stack(dns), jnp.stack(convs)


def kernel(x_prompt, x_sample, p_prompt, p_sample, state_ret, state_delta, state_conv, norm1_g, w_in, conv_w, dt_bias, a_log, ret_norm_g, dn_norm_g, w_out, norm2_g, w_up, w_down, ple_proj, ple_norm_g, ple_gate_w, ple_gate_b, final_norm_g):
    depth = w_in.shape[0]
    layers = [_layer_params(l, norm1_g, w_in, conv_w, dt_bias, a_log, ret_norm_g, dn_norm_g, w_out,
                            norm2_g, w_up, w_down, ple_proj, ple_norm_g, ple_gate_w, ple_gate_b)
              for l in range(depth)]
    fin_g = final_norm_g[None, :]
    intra, rc = _retention_tables()
    bp = x_prompt.shape[0]
    past_len = 4096
    zero_state = jnp.zeros((depth, bp, HEADS, HEAD_DIM, HEAD_DIM), F32)
    zero_conv = jnp.zeros((depth, bp, CONV_W - 1, CONV_CH), F32)
    y_p, ret_p, dn_p, conv_p = _trunk(x_prompt, p_prompt, zero_conv, zero_state, zero_state, 0,
                                      layers, fin_g, intra, rc, bb=2, cb=2, tm=512, tf=1024)
    y_s, ret_s, dn_s, conv_s = _trunk(x_sample, p_sample, state_conv, state_ret, state_delta,
                                      past_len, layers, fin_g, intra, rc, bb=2, cb=1, tm=512, tf=1024)
    return (y_p, y_s, ret_p, dn_p, conv_p, ret_s, dn_s, conv_s)
```

```python
import functools

import jax
import jax.numpy as jnp
import numpy as np
from jax import lax
from jax.experimental import pallas as pl
from jax.experimental.pallas import tpu as pltpu

F32 = jnp.float32
BF16 = jnp.bfloat16

D_MODEL = 1024
CHUNK = 64
HEADS = 4
HEAD_DIM = 128
HEAD_W = HEADS * HEAD_DIM
PLE_DIM = 256
D_FF = 4 * D_MODEL
CONV_W = 4
CONV_CH = 3 * HEAD_W
ROPE_BASE = 10000.0
EPS = 1e-6
Z_MAIN = 8 * HEAD_W
LANE = 128
SUBLANE = 8
Z_COLS = Z_MAIN + LANE
CONV_PAD = SUBLANE
VMEM_LIMIT = 56 * 1024 * 1024


def _split3(x):
    hi = x.astype(BF16)
    r = x - hi.astype(F32)
    mid = r.astype(BF16)
    lo = (r - mid.astype(F32)).astype(BF16)
    return hi, mid, lo


def _split2(x):
    hi = x.astype(BF16)
    lo = (x - hi.astype(F32)).astype(BF16)
    return hi, lo


def _rms(x, g):
    return x * lax.rsqrt(jnp.mean(x * x, axis=-1, keepdims=True) + EPS) * g


def _softplus(x):
    return jnp.maximum(x, 0.0) + jnp.log1p(jnp.exp(-jnp.abs(x)))


def _sigmoid(x):
    return 1.0 / (1.0 + jnp.exp(-x))


def _silu(x):
    return x * _sigmoid(x)


def _in_proj_kernel(x_ref, g_ref, w_ref, wt_ref, z_ref, zt_ref):
    xn = _rms(x_ref[...], g_ref[...]).astype(BF16)
    z_ref[...] = jnp.dot(xn, w_ref[...], preferred_element_type=F32)
    zt_ref[...] = lax.dot_general(wt_ref[...], xn, (((1,), (1,)), ((), ())),
                                  preferred_element_type=F32)


def _in_proj(x, g, w, wt, tm):
    m = x.shape[0]
    return pl.pallas_call(
        _in_proj_kernel,
        grid=(m // tm,),
        in_specs=[
            pl.BlockSpec((tm, D_MODEL), lambda i: (i, 0)),
            pl.BlockSpec((1, D_MODEL), lambda i: (0, 0)),
            pl.BlockSpec((D_MODEL, Z_COLS), lambda i: (0, 0)),
            pl.BlockSpec((SUBLANE, D_MODEL), lambda i: (0, 0)),
        ],
        out_specs=[
            pl.BlockSpec((tm, Z_COLS), lambda i: (i, 0)),
            pl.BlockSpec((SUBLANE, tm), lambda i: (0, i)),
        ],
        out_shape=[
            jax.ShapeDtypeStruct((m, Z_COLS), F32),
            jax.ShapeDtypeStruct((SUBLANE, m), F32),
        ],
        compiler_params=pltpu.CompilerParams(
            dimension_semantics=("arbitrary",), vmem_limit_bytes=VMEM_LIMIT),
        name="in_proj",
    )(x, g, w, wt)


def _bmm(a, b):
    return jnp.einsum("gmk,gkn->gmn", a.astype(BF16), b.astype(BF16), preferred_element_type=F32)


def _bmm_nt(a, b):
    return jnp.einsum("gmk,gnk->gmn", a.astype(BF16), b.astype(BF16), preferred_element_type=F32)


def _bmm_tn(a, b):
    return jnp.einsum("gkm,gkn->gmn", a.astype(BF16), b.astype(BF16), preferred_element_type=F32)


def _bmm_hp(a, b):
    ah, al = _split2(a)
    bh, bl = _split2(b)
    d = functools.partial(jnp.einsum, "gmk,gkn->gmn", preferred_element_type=F32)
    return d(ah, bh) + (d(ah, bl) + d(al, bh))


def _unit_lower_inverse(a):
    n = a.shape[-1]
    ri = lax.broadcasted_iota(jnp.int32, (n, n), 0)
    ci = lax.broadcasted_iota(jnp.int32, (n, n), 1)
    eye = (ri == ci).astype(F32)[None]
    same16 = ((ri // 16) == (ci // 16))[None]
    same32 = ((ri // 32) == (ci // 32))[None]
    d = jnp.where(same16, a, 0.0)
    p = eye - d
    dk = d
    for _ in range(3):
        dk = _bmm_hp(dk, dk)
        p = p + _bmm_hp(p, dk)
    e32 = jnp.where(same32 & jnp.logical_not(same16), a, 0.0)
    p = p - _bmm_hp(p, _bmm_hp(e32, p))
    e64 = jnp.where(same32, 0.0, a)
    p = p - _bmm_hp(p, _bmm_hp(e64, p))
    return p


def _mixer_kernel(qa_ref, ka_ref, va_ref, ga_ref, qb_ref, kb_ref, vb_ref, gb_ref, ba_ref, zt_ref,
                  cos_ref, sin_ref, intra_ref, rc_ref, convw_ref, hpc_ref, hpr_ref,
                  retg_ref, dng_ref, sret0_ref, sdn0_ref, conv0_ref,
                  mix_ref, sret_out, sdn_out, conv_out,
                  sret_s, sdn_s, conv_s, *, bb, cb):
    step = pl.program_id(1)
    blk = cb * CHUNK
    ng = bb * HEADS
    state_shape = (ng, HEAD_DIM, HEAD_DIM)

    @pl.when(step == 0)
    def _():
        sret_s[...] = sret0_ref[...].reshape(state_shape)
        sdn_s[...] = sdn0_ref[...].reshape(state_shape)
        for j in range(3):
            conv_s[:, j, CONV_PAD - 3:CONV_PAD, :] = conv0_ref[:, :, j * HEAD_W:(j + 1) * HEAD_W]

    ri = lax.broadcasted_iota(jnp.int32, (CHUNK, CHUNK), 0)
    ci = lax.broadcasted_iota(jnp.int32, (CHUNK, CHUNK), 1)
    causal = ri >= ci
    strict = ri > ci
    ltri = causal.astype(BF16)
    utri = (ri <= ci).astype(BF16)
    dot = functools.partial(jnp.dot, preferred_element_type=F32)

    hpc = hpc_ref[...]
    hpr = hpr_ref[...]
    neg_a_col = -jnp.exp(hpc[0:1, :])
    neg_a_row = -jnp.exp(hpr[:, 0:1])
    rc = rc_ref[...]
    dn_g = dng_ref[...]
    ret_g = retg_ref[...]
    heads = [slice(h * HEAD_DIM, (h + 1) * HEAD_DIM) for h in range(HEADS)]
    intra = jnp.concatenate([intra_ref[...]] * bb, axis=0)
    cross = jnp.stack([rc[:, h:h + 1] for h in range(HEADS)] * bb)
    inject = jnp.stack([rc[:, HEADS + h:HEADS + h + 1] for h in range(HEADS)] * bb)
    carry = jnp.stack([rc[0:1, 2 * HEADS + h:2 * HEADS + h + 1] for h in range(HEADS)] * bb)
    ret_gain = jnp.stack([ret_g[:, hs] for hs in heads] * bb)

    for b in range(bb):
        for j, ref in enumerate((qb_ref, kb_ref, vb_ref)):
            conv_s[b, j, CONV_PAD:CONV_PAD + blk, :] = ref[b]

    for c in range(cb):
        r0 = c * CHUNK
        rows = pl.ds(r0, CHUNK)
        cosv = cos_ref[rows, :]
        sinv = sin_ref[rows, :]
        q_r, k_r, v_r, gate_a = [], [], [], []
        q_d, k_d, v_d, gate_b = [], [], [], []
        beta, gc_c, gc_r, e_gc, kg_scale, g_end = [], [], [], [], [], []
        for b in range(bb):
            ba = ba_ref[b, rows, :]
            zt = zt_ref[b, :, r0:r0 + CHUNK]
            beta_col = _sigmoid(ba)
            g_col = neg_a_col * _softplus(ba + hpc[1:2, :])
            g_row = neg_a_row * _softplus(zt + hpr[:, 1:2])
            gh, gm, gl = _split3(g_col)
            gc_col = dot(ltri, gh) + (dot(ltri, gm) + dot(ltri, gl))
            gh, gm, gl = _split3(g_row)
            gc_row = dot(gh, utri) + (dot(gm, utri) + dot(gl, utri))
            gc_last = gc_col[CHUNK - 1:CHUNK, :]
            e_gc_col = jnp.exp(gc_col)
            kg_col = jnp.exp(gc_last - gc_col)
            g_end_row = jnp.exp(gc_last)

            conv = []
            for j in range(3):
                acc = None
                for i in range(CONV_W):
                    term = (conv_s[b, j, pl.ds(CONV_PAD - 3 + r0 + i, CHUNK), :]
                            * convw_ref[i:i + 1, j * HEAD_W:(j + 1) * HEAD_W])
                    acc = term if acc is None else acc + term
                conv.append(_silu(acc))
            cq, ck, cv = conv

            qa = qa_ref[b, rows, :]
            ka = ka_ref[b, rows, :]
            va = va_ref[b, rows, :]
            ga = _silu(ga_ref[b, rows, :])
            gb = _silu(gb_ref[b, rows, :])
            for h, hs in enumerate(heads):
                q = qa[:, hs]
                k = ka[:, hs]
                q_r.append(q * cosv + pltpu.roll(q, HEAD_DIM // 2, 1) * sinv)
                k_r.append((k * cosv + pltpu.roll(k, HEAD_DIM // 2, 1) * sinv) * (HEAD_DIM ** -0.5))
                v_r.append(va[:, hs])
                gate_a.append(ga[:, hs])
                q_d.append(cq[:, hs])
                k_d.append(ck[:, hs])
                v_d.append(cv[:, hs])
                gate_b.append(gb[:, hs])
                beta.append(beta_col[:, h:h + 1])
                gc_c.append(gc_col[:, HEADS + h:HEADS + h + 1])
                gc_r.append(gc_row[HEADS + h:HEADS + h + 1, :])
                e_gc.append(e_gc_col[:, HEADS + h:HEADS + h + 1])
                kg_scale.append(kg_col[:, HEADS + h:HEADS + h + 1])
                g_end.append(g_end_row[:, HEADS + h:HEADS + h + 1])

        q, k, v = jnp.stack(q_r), jnp.stack(k_r), jnp.stack(v_r)
        s_prev = sret_s[...]
        scores = _bmm_nt(q, k) * intra
        o = _bmm(scores, v) + _bmm(q, s_prev) * cross
        sret_s[...] = s_prev * carry + _bmm_tn(k, v * inject)
        mu = jnp.mean(o, axis=-1, keepdims=True)
        oc = o - mu
        var = jnp.mean(oc * oc, axis=-1, keepdims=True)
        o_a = oc * lax.rsqrt(var + EPS) * ret_gain * jnp.stack(gate_a)

        q, k, v = jnp.stack(q_d), jnp.stack(k_d), jnp.stack(v_d)
        q = q * lax.rsqrt(jnp.sum(q * q, axis=-1, keepdims=True) + EPS) * (HEAD_DIM ** -0.5)
        k = k * lax.rsqrt(jnp.sum(k * k, axis=-1, keepdims=True) + EPS)
        beta, e_gc = jnp.stack(beta), jnp.stack(e_gc)
        decay = jnp.exp(jnp.where(causal[None], jnp.stack(gc_c) - jnp.stack(gc_r), -jnp.inf))
        kbeta = k * beta
        kk = _bmm_nt(jnp.concatenate([kbeta, q], axis=1), k)
        a = jnp.where(strict[None], kk[:, :CHUNK] * decay, 0.0)
        attn = kk[:, CHUNK:] * decay
        tinv = _unit_lower_inverse(a)
        sol = _bmm_hp(tinv, jnp.concatenate([v * beta, kbeta * e_gc], axis=2))
        u = sol[:, :, :HEAD_DIM]
        w = sol[:, :, HEAD_DIM:]
        s = sdn_s[...]
        ws = _bmm(jnp.concatenate([w, q * e_gc], axis=1), s)
        v_new = u - ws[:, :CHUNK]
        o = ws[:, CHUNK:] + _bmm(attn, v_new)
        sdn_s[...] = s * jnp.stack(g_end) + _bmm_tn(k * jnp.stack(kg_scale), v_new)
        o_b = _rms(o, dn_g[None]) * jnp.stack(gate_b)

        for b in range(bb):
            for h, hs in enumerate(heads):
                g = b * HEADS + h
                mix_ref[b, rows, hs] = o_a[g]
                mix_ref[b, rows, HEAD_W + h * HEAD_DIM:HEAD_W + (h + 1) * HEAD_DIM] = o_b[g]

    for b in range(bb):
        for j in range(3):
            conv_s[b, j, CONV_PAD - 3:CONV_PAD, :] = conv_s[b, j, CONV_PAD + blk - 3:CONV_PAD + blk, :]

    @pl.when(step == pl.num_programs(1) - 1)
    def _():
        sret_out[...] = sret_s[...].reshape(sret_out.shape)
        sdn_out[...] = sdn_s[...].reshape(sdn_out.shape)
        for j in range(3):
            conv_out[:, :, j * HEAD_W:(j + 1) * HEAD_W] = conv_s[:, j, CONV_PAD - 3:CONV_PAD, :]


def _mixer(z, zt, cos_t, sin_t, intra, rc, conv_w, hpc, hpr, ret_g, dn_g, sret0, sdn0, conv0, bb, cb):
    bsz, t, _ = z.shape
    blk = cb * CHUNK
    grid = (bsz // bb, t // blk)

    def zspec(col):
        return pl.BlockSpec((bb, blk, HEAD_W), lambda i, j, col=col: (i, j, col))

    def const(shape):
        return pl.BlockSpec(shape, lambda i, j: (0,) * len(shape))

    state_spec = pl.BlockSpec((bb, HEADS, HEAD_DIM, HEAD_DIM), lambda i, j: (i, 0, 0, 0))
    conv_spec = pl.BlockSpec((bb, CONV_W - 1, CONV_CH), lambda i, j: (i, 0, 0))
    kern = functools.partial(_mixer_kernel, bb=bb, cb=cb)
    return pl.pallas_call(
        kern,
        grid=grid,
        in_specs=[zspec(c) for c in range(8)] + [
            pl.BlockSpec((bb, blk, LANE), lambda i, j: (i, j, Z_MAIN // LANE)),
            pl.BlockSpec((bb, SUBLANE, blk), lambda i, j: (i, 0, j)),
            pl.BlockSpec((blk, LANE), lambda i, j: (j, 0)),
            pl.BlockSpec((blk, LANE), lambda i, j: (j, 0)),
            const((HEADS, CHUNK, CHUNK)),
            const((CHUNK, LANE)),
            const((CONV_W, CONV_CH)),
            const((SUBLANE, LANE)),
            const((SUBLANE, LANE)),
            const((1, HEAD_W)),
            const((1, HEAD_DIM)),
            state_spec, state_spec, conv_spec,
        ],
        out_specs=[
            pl.BlockSpec((bb, blk, 2 * HEAD_W), lambda i, j: (i, j, 0)),
            state_spec, state_spec, conv_spec,
        ],
        out_shape=[
            jax.ShapeDtypeStruct((bsz, t, 2 * HEAD_W), F32),
            jax.ShapeDtypeStruct((bsz, HEADS, HEAD_DIM, HEAD_DIM), F32),
            jax.ShapeDtypeStruct((bsz, HEADS, HEAD_DIM, HEAD_DIM), F32),
            jax.ShapeDtypeStruct((bsz, CONV_W - 1, CONV_CH), F32),
        ],
        scratch_shapes=[
            pltpu.VMEM((bb * HEADS, HEAD_DIM, HEAD_DIM), F32),
            pltpu.VMEM((bb * HEADS, HEAD_DIM, HEAD_DIM), F32),
            pltpu.VMEM((bb, 3, CONV_PAD + blk, HEAD_W), F32),
        ],
        compiler_params=pltpu.CompilerParams(
            dimension_semantics=("arbitrary", "arbitrary"), vmem_limit_bytes=VMEM_LIMIT),
        name="mixer",
    )(*([z] * 9), zt, cos_t, sin_t, intra, rc, conv_w, hpc, hpr, ret_g, dn_g, sret0, sdn0, conv0)


def _post_kernel(h_ref, mix_ref, p_ref, wout_ref, n2g_ref, wup_ref, wdown_ref, proj_ref,
                 pleg_ref, gatew_ref, gateb_ref, fing_ref, out_ref, acc_s, xn_s, *, final):
    j = pl.program_id(1)

    @pl.when(j == 0)
    def _():
        h1 = h_ref[...] + jnp.dot(mix_ref[...].astype(BF16), wout_ref[...],
                                  preferred_element_type=F32)
        acc_s[...] = h1
        xn_s[...] = _rms(h1, n2g_ref[...]).astype(BF16)

    up = jnp.maximum(jnp.dot(xn_s[...], wup_ref[...], preferred_element_type=F32), 0.0)
    acc_s[...] += jnp.dot((up * up).astype(BF16), wdown_ref[...], preferred_element_type=F32)

    @pl.when(j == pl.num_programs(1) - 1)
    def _():
        h2 = acc_s[...]
        gate = _sigmoid(jnp.dot(_rms(h2, pleg_ref[...]).astype(BF16), gatew_ref[...],
                                preferred_element_type=F32) + gateb_ref[...])
        h3 = h2 + jnp.dot(p_ref[...].astype(BF16), proj_ref[...],
                          preferred_element_type=F32) * gate
        if final:
            h3 = _rms(h3, fing_ref[...])
        out_ref[...] = h3


def _post(h, mix, p, w_out, n2g, w_up, w_down, proj, pleg, gate_w, gate_b, fin_g, final, tm, tf):
    m = h.shape[0]

    def row(width):
        return pl.BlockSpec((tm, width), lambda i, j: (i, 0))

    def const(shape):
        return pl.BlockSpec(shape, lambda i, j: (0,) * len(shape))

    return pl.pallas_call(
        functools.partial(_post_kernel, final=final),
        grid=(m // tm, D_FF // tf),
        in_specs=[
            row(D_MODEL), row(2 * HEAD_W), row(PLE_DIM),
            const((2 * HEAD_W, D_MODEL)),
            const((1, D_MODEL)),
            pl.BlockSpec((D_MODEL, tf), lambda i, j: (0, j)),
            pl.BlockSpec((tf, D_MODEL), lambda i, j: (j, 0)),
            const((PLE_DIM, D_MODEL)),
            const((1, D_MODEL)),
            const((D_MODEL, D_MODEL)),
            const((1, D_MODEL)),
            const((1, D_MODEL)),
        ],
        out_specs=row(D_MODEL),
        out_shape=jax.ShapeDtypeStruct((m, D_MODEL), F32),
        scratch_shapes=[pltpu.VMEM((tm, D_MODEL), F32), pltpu.VMEM((tm, D_MODEL), BF16)],
        compiler_params=pltpu.CompilerParams(
            dimension_semantics=("arbitrary", "arbitrary"), vmem_limit_bytes=VMEM_LIMIT),
        name="post",
    )(h, mix, p, w_out, n2g, w_up, w_down, proj, pleg, gate_w, gate_b, fin_g)


def _rope_tables(pos0, t):
    half = HEAD_DIM // 2
    inv = ROPE_BASE ** (-jnp.arange(half, dtype=F32) / half)
    ang = (pos0 + jnp.arange(t)).astype(F32)[:, None] * inv[None, :]
    cos, sin = jnp.cos(ang), jnp.sin(ang)
    return jnp.concatenate([cos, cos], axis=-1), jnp.concatenate([-sin, sin], axis=-1)


def _retention_tables():
    log_gamma = jnp.log(1.0 - 2.0 ** (-5.0 - jnp.arange(HEADS, dtype=F32)))
    idx = jnp.arange(CHUNK, dtype=F32)
    intra = jnp.exp(jnp.abs(idx[:, None] - idx[None, :])[None] * log_gamma[:, None, None])
    cross = jnp.exp((idx + 1.0)[None, :] * log_gamma[:, None])
    inject = jnp.exp((CHUNK - 1.0 - idx)[None, :] * log_gamma[:, None])
    carry = jnp.exp(CHUNK * log_gamma)
    rc = jnp.zeros((CHUNK, LANE), F32)
    rc = rc.at[:, 0:HEADS].set(cross.T)
    rc = rc.at[:, HEADS:2 * HEADS].set(inject.T)
    rc = rc.at[:, 2 * HEADS:3 * HEADS].set(jnp.broadcast_to(carry[None, :], (CHUNK, HEADS)))
    return intra, rc


def _layer_params(l, norm1_g, w_in, conv_w, dt_bias, a_log, ret_norm_g, dn_norm_g, w_out, norm2_g,
                  w_up, w_down, ple_proj, ple_norm_g, ple_gate_w, ple_gate_b):
    w = w_in[l]
    w_pad = jnp.concatenate(
        [w, jnp.zeros((D_MODEL, Z_COLS - w.shape[1]), w.dtype)], axis=1).astype(BF16)
    wt = w[:, Z_MAIN:].T.astype(BF16)
    hpc = jnp.zeros((SUBLANE, LANE), F32)
    hpc = hpc.at[0, HEADS:2 * HEADS].set(a_log[l]).at[1, HEADS:2 * HEADS].set(dt_bias[l])
    hpr = jnp.zeros((SUBLANE, LANE), F32)
    hpr = hpr.at[HEADS:2 * HEADS, 0].set(a_log[l]).at[HEADS:2 * HEADS, 1].set(dt_bias[l])
    return dict(
        n1g=norm1_g[l][None, :], w_in=w_pad, w_in_t=wt, conv_w=conv_w[l], hpc=hpc, hpr=hpr,
        ret_g=ret_norm_g[l][None, :], dn_g=dn_norm_g[l][None, :],
        w_out=w_out[l].astype(BF16), n2g=norm2_g[l][None, :],
        w_up=w_up[l].astype(BF16), w_down=w_down[l].astype(BF16),
        proj=ple_proj[l].astype(BF16), pleg=ple_norm_g[l][None, :],
        gate_w=ple_gate_w[l].astype(BF16), gate_b=ple_gate_b[l][None, :])


def _trunk(x, p, conv0, sret0, sdn0, pos0, layers, fin_g, intra, rc, bb, cb, tm, tf):
    bsz, t, _ = x.shape
    m = bsz * t
    cos_t, sin_t = _rope_tables(pos0, t)
    h = x.reshape(m, D_MODEL)
    rets, dns, convs = [], [], []
    depth = len(layers)
    for l, lp in enumerate(layers):
        z, zt = _in_proj(h, lp["n1g"], lp["w_in"], lp["w_in_t"], tm)
        z = z.reshape(bsz, t, Z_COLS)
        zt = zt.reshape(SUBLANE, bsz, t).transpose(1, 0, 2)
        mix, sret, sdn, conv = _mixer(z, zt, cos_t, sin_t, intra, rc, lp["conv_w"], lp["hpc"],
                                      lp["hpr"], lp["ret_g"], lp["dn_g"], sret0[l], sdn0[l],
                                      conv0[l], bb, cb)
        h = _post(h, mix.reshape(m, 2 * HEAD_W), p[l].reshape(m, PLE_DIM), lp["w_out"], lp["n2g"],
                  lp["w_up"], lp["w_down"], lp["proj"], lp["pleg"], lp["gate_w"], lp["gate_b"],
                  fin_g, l == depth - 1, tm, tf)
        rets.append(sret)
        dns.append(sdn)
        convs.append(conv)
    return h.reshape(bsz, t, D_MODEL), jnp.stack(rets), jnp.stack(dns), jnp.stack(convs)


def kernel(x_prompt, x_sample, p_prompt, p_sample, state_ret, state_delta, state_conv, norm1_g, w_in, conv_w, dt_bias, a_log, ret_norm_g, dn_norm_g, w_out, norm2_g, w_up, w_down, ple_proj, ple_norm_g, ple_gate_w, ple_gate_b, final_norm_g):
    depth = w_in.shape[0]
    layers = [_layer_params(l, norm1_g, w_in, conv_w, dt_bias, a_log, ret_norm_g, dn_norm_g, w_out,
                            norm2_g, w_up, w_down, ple_proj, ple_norm_g, ple_gate_w, ple_gate_b)
              for l in range(depth)]
    fin_g = final_norm_g[None, :]
    intra, rc = _retention_tables()
    bp = x_prompt.shape[0]
    past_len = 4096
    zero_state = jnp.zeros((depth, bp, HEADS, HEAD_DIM, HEAD_DIM), F32)
    zero_conv = jnp.zeros((depth, bp, CONV_W - 1, CONV_CH), F32)
    y_p, ret_p, dn_p, conv_p = _trunk(x_prompt, p_prompt, zero_conv, zero_state, zero_state, 0,
                                      layers, fin_g, intra, rc, bb=2, cb=2, tm=512, tf=1024)
    y_s, ret_s, dn_s, conv_s = _trunk(x_sample, p_sample, state_conv, state_ret, state_delta,
                                      past_len, layers, fin_g, intra, rc, bb=2, cb=1, tm=512, tf=1024)
    return (y_p, y_s, ret_p, dn_p, conv_p, ret_s, dn_s, conv_s)
```

```python
import functools

import jax
import jax.numpy as jnp
import numpy as np
from jax import lax
from jax.experimental import pallas as pl
from jax.experimental.pallas import tpu as pltpu

F32 = jnp.float32
BF16 = jnp.bfloat16

D_MODEL = 1024
CHUNK = 64
HEADS = 4
HEAD_DIM = 128
HEAD_W = HEADS * HEAD_DIM
PLE_DIM = 256
D_FF = 4 * D_MODEL
CONV_W = 4
CONV_CH = 3 * HEAD_W
ROPE_BASE = 10000.0
EPS = 1e-6
Z_MAIN = 8 * HEAD_W
LANE = 128
SUBLANE = 8
Z_COLS = Z_MAIN + LANE
CONV_PAD = SUBLANE
VMEM_LIMIT = 56 * 1024 * 1024


def _split3(x):
    hi = x.astype(BF16)
    r = x - hi.astype(F32)
    mid = r.astype(BF16)
    lo = (r - mid.astype(F32)).astype(BF16)
    return hi, mid, lo


def _rms(x, g):
    return x * lax.rsqrt(jnp.mean(x * x, axis=-1, keepdims=True) + EPS) * g


def _softplus(x):
    return jnp.maximum(x, 0.0) + jnp.log1p(jnp.exp(-jnp.abs(x)))


def _sigmoid(x):
    return 1.0 / (1.0 + jnp.exp(-x))


def _silu(x):
    return x * _sigmoid(x)


def _in_proj_kernel(x_ref, g_ref, w_ref, wt_ref, z_ref, zt_ref):
    xn = _rms(x_ref[...], g_ref[...]).astype(BF16)
    z_ref[...] = jnp.dot(xn, w_ref[...], preferred_element_type=F32)
    zt_ref[...] = lax.dot_general(wt_ref[...], xn, (((1,), (1,)), ((), ())),
                                  preferred_element_type=F32)


def _in_proj(x, g, w, wt, tm):
    m = x.shape[0]
    return pl.pallas_call(
        _in_proj_kernel,
        grid=(m // tm,),
        in_specs=[
            pl.BlockSpec((tm, D_MODEL), lambda i: (i, 0)),
            pl.BlockSpec((1, D_MODEL), lambda i: (0, 0)),
            pl.BlockSpec((D_MODEL, Z_COLS), lambda i: (0, 0)),
            pl.BlockSpec((SUBLANE, D_MODEL), lambda i: (0, 0)),
        ],
        out_specs=[
            pl.BlockSpec((tm, Z_COLS), lambda i: (i, 0)),
            pl.BlockSpec((SUBLANE, tm), lambda i: (0, i)),
        ],
        out_shape=[
            jax.ShapeDtypeStruct((m, Z_COLS), F32),
            jax.ShapeDtypeStruct((SUBLANE, m), F32),
        ],
        compiler_params=pltpu.CompilerParams(
            dimension_semantics=("arbitrary",), vmem_limit_bytes=VMEM_LIMIT),
        name="in_proj",
    )(x, g, w, wt)


def _bmm(a, b):
    return jnp.einsum("gmk,gkn->gmn", a.astype(BF16), b.astype(BF16), preferred_element_type=F32)


def _bmm_nt(a, b):
    return jnp.einsum("gmk,gnk->gmn", a.astype(BF16), b.astype(BF16), preferred_element_type=F32)


def _bmm_tn(a, b):
    return jnp.einsum("gkm,gkn->gmn", a.astype(BF16), b.astype(BF16), preferred_element_type=F32)


def _unit_lower_inverse(a):
    n = a.shape[-1]
    ri = lax.broadcasted_iota(jnp.int32, (n, n), 0)
    ci = lax.broadcasted_iota(jnp.int32, (n, n), 1)
    eye = (ri == ci).astype(F32)[None]
    inner = ((ri // 2) == (ci // 2))[None]
    p = eye - jnp.where(inner, a, 0.0)
    size = 2
    while size < n:
        size *= 2
        outer = ((ri // size) == (ci // size))[None]
        e = jnp.where(outer & jnp.logical_not(inner), a, 0.0)
        p = p - _bmm(p, _bmm(e, p))
        inner = outer
    return p


def _mixer_kernel(qa_ref, ka_ref, va_ref, ga_ref, qb_ref, kb_ref, vb_ref, gb_ref, ba_ref, zt_ref,
                  cos_ref, sin_ref, intra_ref, rc_ref, convw_ref, hpc_ref, hpr_ref,
                  retg_ref, dng_ref, sret0_ref, sdn0_ref, conv0_ref,
                  mix_ref, sret_out, sdn_out, conv_out,
                  sret_s, sdn_s, conv_s, *, bb, cb):
    step = pl.program_id(1)
    blk = cb * CHUNK
    ng = bb * HEADS
    state_shape = (ng, HEAD_DIM, HEAD_DIM)

    @pl.when(step == 0)
    def _():
        sret_s[...] = sret0_ref[...].reshape(state_shape)
        sdn_s[...] = sdn0_ref[...].reshape(state_shape)
        for j in range(3):
            conv_s[:, j, CONV_PAD - 3:CONV_PAD, :] = conv0_ref[:, :, j * HEAD_W:(j + 1) * HEAD_W]

    ri = lax.broadcasted_iota(jnp.int32, (CHUNK, CHUNK), 0)
    ci = lax.broadcasted_iota(jnp.int32, (CHUNK, CHUNK), 1)
    causal = ri >= ci
    strict = ri > ci
    ltri = causal.astype(BF16)
    utri = (ri <= ci).astype(BF16)
    dot = functools.partial(jnp.dot, preferred_element_type=F32)

    hpc = hpc_ref[...]
    hpr = hpr_ref[...]
    neg_a_col = -jnp.exp(hpc[0:1, :])
    neg_a_row = -jnp.exp(hpr[:, 0:1])
    dn_g = dng_ref[...]
    ret_g = retg_ref[...]
    heads = [slice(h * HEAD_DIM, (h + 1) * HEAD_DIM) for h in range(HEADS)]
    intra = jnp.concatenate([intra_ref[...]] * bb, axis=0)
    cross = jnp.concatenate([rc_ref[0]] * bb, axis=0)
    inject = jnp.concatenate([rc_ref[1]] * bb, axis=0)
    carry = jnp.concatenate([rc_ref[2, :, 0:1, :]] * bb, axis=0)
    ret_gain = jnp.stack([ret_g[:, hs] for hs in heads] * bb)

    def lanes(col, j):
        return jnp.broadcast_to(col[:, j:j + 1], (CHUNK, HEAD_DIM))

    for b in range(bb):
        for j, ref in enumerate((qb_ref, kb_ref, vb_ref)):
            conv_s[b, j, CONV_PAD:CONV_PAD + blk, :] = ref[b]

    for c in range(cb):
        r0 = c * CHUNK
        rows = pl.ds(r0, CHUNK)
        cosv = cos_ref[rows, :]
        sinv = sin_ref[rows, :]
        q_r, k_r, v_r, gate_a = [], [], [], []
        q_d, k_d, v_d, gate_b = [], [], [], []
        beta, gc_c, gc_r = [], [], []
        for b in range(bb):
            ba = ba_ref[b, rows, :]
            zt = zt_ref[b, :, r0:r0 + CHUNK]
            beta_col = _sigmoid(ba)
            g_col = neg_a_col * _softplus(ba + hpc[1:2, :])
            g_row = neg_a_row * _softplus(zt + hpr[:, 1:2])
            gh, gm, gl = _split3(g_col)
            gc_col = dot(ltri, gh) + (dot(ltri, gm) + dot(ltri, gl))
            gh, gm, gl = _split3(g_row)
            gc_row = dot(gh, utri) + (dot(gm, utri) + dot(gl, utri))

            conv = []
            for j in range(3):
                xp = conv_s[b, j, pl.ds(r0, CONV_PAD + CHUNK), :]
                acc = None
                for i in range(CONV_W):
                    tap = xp if i == CONV_W - 1 else pltpu.roll(xp, CONV_W - 1 - i, 0)
                    term = tap[CONV_PAD:] * convw_ref[i:i + 1, j * HEAD_W:(j + 1) * HEAD_W]
                    acc = term if acc is None else acc + term
                conv.append(_silu(acc))
            cq, ck, cv = conv

            qa = qa_ref[b, rows, :]
            ka = ka_ref[b, rows, :]
            va = va_ref[b, rows, :]
            ga = _silu(ga_ref[b, rows, :])
            gb = _silu(gb_ref[b, rows, :])
            for h, hs in enumerate(heads):
                q = qa[:, hs]
                k = ka[:, hs]
                q_r.append(q * cosv + pltpu.roll(q, HEAD_DIM // 2, 1) * sinv)
                k_r.append((k * cosv + pltpu.roll(k, HEAD_DIM // 2, 1) * sinv) * (HEAD_DIM ** -0.5))
                v_r.append(va[:, hs])
                gate_a.append(ga[:, hs])
                q_d.append(cq[:, hs])
                k_d.append(ck[:, hs])
                v_d.append(cv[:, hs])
                gate_b.append(gb[:, hs])
                beta.append(lanes(beta_col, h))
                gc_c.append(lanes(gc_col, HEADS + h))
                gc_r.append(gc_row[HEADS + h:HEADS + h + 1, :])

        q, k, v = jnp.stack(q_r), jnp.stack(k_r), jnp.stack(v_r)
        s_prev = sret_s[...]
        scores = _bmm_nt(q, k) * intra
        o = _bmm(scores, v) + _bmm(q, s_prev) * cross
        sret_s[...] = s_prev * carry + _bmm_tn(k, v * inject)
        mu = jnp.mean(o, axis=-1, keepdims=True)
        oc = o - mu
        var = jnp.mean(oc * oc, axis=-1, keepdims=True)
        o_a = oc * lax.rsqrt(var + EPS) * ret_gain * jnp.stack(gate_a)

        q, k, v = jnp.stack(q_d), jnp.stack(k_d), jnp.stack(v_d)
        q = q * lax.rsqrt(jnp.sum(q * q, axis=-1, keepdims=True) + EPS) * (HEAD_DIM ** -0.5)
        k = k * lax.rsqrt(jnp.sum(k * k, axis=-1, keepdims=True) + EPS)
        beta, gc_c = jnp.stack(beta), jnp.stack(gc_c)
        gc_last = gc_c[:, CHUNK - 1:CHUNK, :]
        e_gc = jnp.exp(gc_c)
        decay = jnp.exp(jnp.where(causal[None], gc_c[:, :, :CHUNK] - jnp.stack(gc_r), -jnp.inf))
        kbeta = k * beta
        kk = _bmm_nt(jnp.concatenate([kbeta, q], axis=1), k)
        a = jnp.where(strict[None], kk[:, :CHUNK] * decay, 0.0)
        attn = kk[:, CHUNK:] * decay
        tinv = _unit_lower_inverse(a)
        sol = _bmm(tinv, jnp.concatenate([v * beta, kbeta * e_gc], axis=2))
        u = sol[:, :, :HEAD_DIM]
        w = sol[:, :, HEAD_DIM:]
        s = sdn_s[...]
        ws = _bmm(jnp.concatenate([w, q * e_gc], axis=1), s)
        v_new = u - ws[:, :CHUNK]
        o = ws[:, CHUNK:] + _bmm(attn, v_new)
        sdn_s[...] = s * jnp.exp(gc_last) + _bmm_tn(k * jnp.exp(gc_last - gc_c), v_new)
        o_b = _rms(o, dn_g[None]) * jnp.stack(gate_b)

        for b in range(bb):
            for h, hs in enumerate(heads):
                g = b * HEADS + h
                mix_ref[b, rows, hs] = o_a[g]
                mix_ref[b, rows, HEAD_W + h * HEAD_DIM:HEAD_W + (h + 1) * HEAD_DIM] = o_b[g]

    for b in range(bb):
        for j in range(3):
            conv_s[b, j, CONV_PAD - 3:CONV_PAD, :] = conv_s[b, j, CONV_PAD + blk - 3:CONV_PAD + blk, :]

    @pl.when(step == pl.num_programs(1) - 1)
    def _():
        sret_out[...] = sret_s[...].reshape(sret_out.shape)
        sdn_out[...] = sdn_s[...].reshape(sdn_out.shape)
        for j in range(3):
            conv_out[:, :, j * HEAD_W:(j + 1) * HEAD_W] = conv_s[:, j, CONV_PAD - 3:CONV_PAD, :]


def _mixer(z, zt, cos_t, sin_t, intra, rc, conv_w, hpc, hpr, ret_g, dn_g, sret0, sdn0, conv0, bb, cb):
    bsz, t, _ = z.shape
    blk = cb * CHUNK
    grid = (bsz // bb, t // blk)

    def zspec(col):
        return pl.BlockSpec((bb, blk, HEAD_W), lambda i, j, col=col: (i, j, col))

    def const(shape):
        return pl.BlockSpec(shape, lambda i, j: (0,) * len(shape))

    state_spec = pl.BlockSpec((bb, HEADS, HEAD_DIM, HEAD_DIM), lambda i, j: (i, 0, 0, 0))
    conv_spec = pl.BlockSpec((bb, CONV_W - 1, CONV_CH), lambda i, j: (i, 0, 0))
    kern = functools.partial(_mixer_kernel, bb=bb, cb=cb)
    return pl.pallas_call(
        kern,
        grid=grid,
        in_specs=[zspec(c) for c in range(8)] + [
            pl.BlockSpec((bb, blk, LANE), lambda i, j: (i, j, Z_MAIN // LANE)),
            pl.BlockSpec((bb, SUBLANE, blk), lambda i, j: (i, 0, j)),
            pl.BlockSpec((blk, LANE), lambda i, j: (j, 0)),
            pl.BlockSpec((blk, LANE), lambda i, j: (j, 0)),
            const((HEADS, CHUNK, CHUNK)),
            const((3, HEADS, CHUNK, LANE)),
            const((CONV_W, CONV_CH)),
            const((SUBLANE, LANE)),
            const((SUBLANE, LANE)),
            const((1, HEAD_W)),
            const((1, HEAD_DIM)),
            state_spec, state_spec, conv_spec,
        ],
        out_specs=[
            pl.BlockSpec((bb, blk, 2 * HEAD_W), lambda i, j: (i, j, 0)),
            state_spec, state_spec, conv_spec,
        ],
        out_shape=[
            jax.ShapeDtypeStruct((bsz, t, 2 * HEAD_W), F32),
            jax.ShapeDtypeStruct((bsz, HEADS, HEAD_DIM, HEAD_DIM), F32),
            jax.ShapeDtypeStruct((bsz, HEADS, HEAD_DIM, HEAD_DIM), F32),
            jax.ShapeDtypeStruct((bsz, CONV_W - 1, CONV_CH), F32),
        ],
        scratch_shapes=[
            pltpu.VMEM((bb * HEADS, HEAD_DIM, HEAD_DIM), F32),
            pltpu.VMEM((bb * HEADS, HEAD_DIM, HEAD_DIM), F32),
            pltpu.VMEM((bb, 3, CONV_PAD + blk, HEAD_W), F32),
        ],
        compiler_params=pltpu.CompilerParams(
            dimension_semantics=("arbitrary", "arbitrary"), vmem_limit_bytes=VMEM_LIMIT),
        name="mixer",
    )(*([z] * 9), zt, cos_t, sin_t, intra, rc, conv_w, hpc, hpr, ret_g, dn_g, sret0, sdn0, conv0)


def _post_kernel(h_ref, mix_ref, p_ref, wout_ref, n2g_ref, wup_ref, wdown_ref, proj_ref,
                 pleg_ref, gatew_ref, gateb_ref, fing_ref, out_ref, acc_s, xn_s, *, final):
    j = pl.program_id(1)

    @pl.when(j == 0)
    def _():
        h1 = h_ref[...] + jnp.dot(mix_ref[...].astype(BF16), wout_ref[...],
                                  preferred_element_type=F32)
        acc_s[...] = h1
        xn_s[...] = _rms(h1, n2g_ref[...]).astype(BF16)

    up = jnp.maximum(jnp.dot(xn_s[...], wup_ref[...], preferred_element_type=F32), 0.0)
    acc_s[...] += jnp.dot((up * up).astype(BF16), wdown_ref[...], preferred_element_type=F32)

    @pl.when(j == pl.num_programs(1) - 1)
    def _():
        h2 = acc_s[...]
        gate = _sigmoid(jnp.dot(_rms(h2, pleg_ref[...]).astype(BF16), gatew_ref[...],
                                preferred_element_type=F32) + gateb_ref[...])
        h3 = h2 + jnp.dot(p_ref[...].astype(BF16), proj_ref[...],
                          preferred_element_type=F32) * gate
        if final:
            h3 = _rms(h3, fing_ref[...])
        out_ref[...] = h3


def _post(h, mix, p, w_out, n2g, w_up, w_down, proj, pleg, gate_w, gate_b, fin_g, final, tm, tf):
    m = h.shape[0]

    def row(width):
        return pl.BlockSpec((tm, width), lambda i, j: (i, 0))

    def const(shape):
        return pl.BlockSpec(shape, lambda i, j: (0,) * len(shape))

    return pl.pallas_call(
        functools.partial(_post_kernel, final=final),
        grid=(m // tm, D_FF // tf),
        in_specs=[
            row(D_MODEL), row(2 * HEAD_W), row(PLE_DIM),
            const((2 * HEAD_W, D_MODEL)),
            const((1, D_MODEL)),
            pl.BlockSpec((D_MODEL, tf), lambda i, j: (0, j)),
            pl.BlockSpec((tf, D_MODEL), lambda i, j: (j, 0)),
            const((PLE_DIM, D_MODEL)),
            const((1, D_MODEL)),
            const((D_MODEL, D_MODEL)),
            const((1, D_MODEL)),
            const((1, D_MODEL)),
        ],
        out_specs=row(D_MODEL),
        out_shape=jax.ShapeDtypeStruct((m, D_MODEL), F32),
        scratch_shapes=[pltpu.VMEM((tm, D_MODEL), F32), pltpu.VMEM((tm, D_MODEL), BF16)],
        compiler_params=pltpu.CompilerParams(
            dimension_semantics=("arbitrary", "arbitrary"), vmem_limit_bytes=VMEM_LIMIT),
        name="post",
    )(h, mix, p, w_out, n2g, w_up, w_down, proj, pleg, gate_w, gate_b, fin_g)


def _rope_tables(pos0, t):
    half = HEAD_DIM // 2
    inv = ROPE_BASE ** (-jnp.arange(half, dtype=F32) / half)
    ang = (pos0 + jnp.arange(t)).astype(F32)[:, None] * inv[None, :]
    cos, sin = jnp.cos(ang), jnp.sin(ang)
    return jnp.concatenate([cos, cos], axis=-1), jnp.concatenate([-sin, sin], axis=-1)


def _retention_tables():
    log_gamma = jnp.log(1.0 - 2.0 ** (-5.0 - jnp.arange(HEADS, dtype=F32)))
    idx = jnp.arange(CHUNK, dtype=F32)
    intra = jnp.exp(jnp.abs(idx[:, None] - idx[None, :])[None] * log_gamma[:, None, None])
    cross = jnp.exp((idx + 1.0)[None, :] * log_gamma[:, None])
    inject = jnp.exp((CHUNK - 1.0 - idx)[None, :] * log_gamma[:, None])
    carry = jnp.exp(CHUNK * log_gamma)
    carry = jnp.broadcast_to(carry[:, None], (HEADS, CHUNK))
    rc = jnp.broadcast_to(jnp.stack([cross, inject, carry])[..., None], (3, HEADS, CHUNK, LANE))
    return intra, rc


def _layer_params(l, norm1_g, w_in, conv_w, dt_bias, a_log, ret_norm_g, dn_norm_g, w_out, norm2_g,
                  w_up, w_down, ple_proj, ple_norm_g, ple_gate_w, ple_gate_b):
    w = w_in[l]
    w_pad = jnp.concatenate(
        [w, jnp.zeros((D_MODEL, Z_COLS - w.shape[1]), w.dtype)], axis=1).astype(BF16)
    wt = w[:, Z_MAIN:].T.astype(BF16)
    hpc = jnp.zeros((SUBLANE, LANE), F32)
    hpc = hpc.at[0, HEADS:2 * HEADS].set(a_log[l]).at[1, HEADS:2 * HEADS].set(dt_bias[l])
    hpr = jnp.zeros((SUBLANE, LANE), F32)
    hpr = hpr.at[HEADS:2 * HEADS, 0].set(a_log[l]).at[HEADS:2 * HEADS, 1].set(dt_bias[l])
    return dict(
        n1g=norm1_g[l][None, :], w_in=w_pad, w_in_t=wt, conv_w=conv_w[l], hpc=hpc, hpr=hpr,
        ret_g=ret_norm_g[l][None, :], dn_g=dn_norm_g[l][None, :],
        w_out=w_out[l].astype(BF16), n2g=norm2_g[l][None, :],
        w_up=w_up[l].astype(BF16), w_down=w_down[l].astype(BF16),
        proj=ple_proj[l].astype(BF16), pleg=ple_norm_g[l][None, :],
        gate_w=ple_gate_w[l].astype(BF16), gate_b=ple_gate_b[l][None, :])


def _trunk(x, p, conv0, sret0, sdn0, pos0, layers, fin_g, intra, rc, bb, cb, tm, tf):
    bsz, t, _ = x.shape
    m = bsz * t
    cos_t, sin_t = _rope_tables(pos0, t)
    h = x.reshape(m, D_MODEL)
    rets, dns, convs = [], [], []
    depth = len(layers)
    for l, lp in enumerate(layers):
        z, zt = _in_proj(h, lp["n1g"], lp["w_in"], lp["w_in_t"], tm)
        z = z.reshape(bsz, t, Z_COLS)
        zt = zt.reshape(SUBLANE, bsz, t).transpose(1, 0, 2)
        mix, sret, sdn, conv = _mixer(z, zt, cos_t, sin_t, intra, rc, lp["conv_w"], lp["hpc"],
                                      lp["hpr"], lp["ret_g"], lp["dn_g"], sret0[l], sdn0[l],
                                      conv0[l], bb, cb)
        h = _post(h, mix.reshape(m, 2 * HEAD_W), p[l].reshape(m, PLE_DIM), lp["w_out"], lp["n2g"],
                  lp["w_up"], lp["w_down"], lp["proj"], lp["pleg"], lp["gate_w"], lp["gate_b"],
                  fin_g, l == depth - 1, tm, tf)
        rets.append(sret)
        dns.append(sdn)
        convs.append(conv)
    return h.reshape(bsz, t, D_MODEL), jnp.stack(rets), jnp.stack(dns), jnp.stack(convs)


def kernel(x_prompt, x_sample, p_prompt, p_sample, state_ret, state_delta, state_conv, norm1_g, w_in, conv_w, dt_bias, a_log, ret_norm_g, dn_norm_g, w_out, norm2_g, w_up, w_down, ple_proj, ple_norm_g, ple_gate_w, ple_gate_b, final_norm_g):
    depth = w_in.shape[0]
    layers = [_layer_params(l, norm1_g, w_in, conv_w, dt_bias, a_log, ret_norm_g, dn_norm_g, w_out,
                            norm2_g, w_up, w_down, ple_proj, ple_norm_g, ple_gate_w, ple_gate_b)
              for l in range(depth)]
    fin_g = final_norm_g[None, :]
    intra, rc = _retention_tables()
    bp = x_prompt.shape[0]
    past_len = 4096
    zero_state = jnp.zeros((depth, bp, HEADS, HEAD_DIM, HEAD_DIM), F32)
    zero_conv = jnp.zeros((depth, bp, CONV_W - 1, CONV_CH), F32)
    y_p, ret_p, dn_p, conv_p = _trunk(x_prompt, p_prompt, zero_conv, zero_state, zero_state, 0,
                                      layers, fin_g, intra, rc, bb=2, cb=2, tm=512, tf=1024)
    y_s, ret_s, dn_s, conv_s = _trunk(x_sample, p_sample, state_conv, state_ret, state_delta,
                                      past_len, layers, fin_g, intra, rc, bb=2, cb=1, tm=512, tf=1024)
    return (y_p, y_s, ret_p, dn_p, conv_p, ret_s, dn_s, conv_s)
```

```python
import functools

import jax
import jax.numpy as jnp
from jax import lax
from jax.experimental import pallas as pl
from jax.experimental.pallas import tpu as pltpu

F32 = jnp.float32
BF16 = jnp.bfloat16

D_MODEL = 1024
CHUNK = 64
HEADS = 4
HEAD_DIM = 128
HEAD_W = HEADS * HEAD_DIM
PLE_DIM = 256
D_FF = 4 * D_MODEL
CONV_W = 4
CONV_CH = 3 * HEAD_W
ROPE_BASE = 10000.0
EPS = 1e-6
PAST_LEN = 4096
Z_MAIN = 8 * HEAD_W
N_GATE = 2 * HEADS
LANE = 128
SUBLANE = 8
REST_W = 6 * HEAD_W + LANE
VMEM_LIMIT = 56 * 1024 * 1024


def _split3(x):
    hi = x.astype(BF16)
    r = x - hi.astype(F32)
    mid = r.astype(BF16)
    lo = (r - mid.astype(F32)).astype(BF16)
    return hi, mid, lo


def _rms(x, g):
    return x * lax.rsqrt(jnp.mean(x * x, axis=-1, keepdims=True) + EPS) * g


def _softplus(x):
    return jnp.maximum(x, 0.0) + jnp.log1p(jnp.exp(-jnp.abs(x)))


def _sigmoid(x):
    return 0.5 + 0.5 * jnp.tanh(0.5 * x)


def _silu(x):
    h = 0.5 * x
    return h + h * jnp.tanh(h)


def _in_proj_kernel(x_ref, g_ref, w_ref, wba_ref, wbat_ref, cos_ref, sin_ref, convw_ref, conv0_ref,
                    qk_ref, rest_ref, zt_ref, convnew_ref, hist_s):
    j = pl.program_id(1)
    tm = x_ref.shape[0]
    xn = _rms(x_ref[...], g_ref[...]).astype(BF16)

    def proj(col):
        return jnp.dot(xn, w_ref[:, col * HEAD_W:(col + 1) * HEAD_W], preferred_element_type=F32)

    @pl.when(j == 0)
    def _():
        hist_s[SUBLANE - (CONV_W - 1):SUBLANE, :] = conv0_ref[...]

    heads = [slice(h * HEAD_DIM, (h + 1) * HEAD_DIM) for h in range(HEADS)]

    def rope(col, z):
        scale = 1.0 if col == 0 else HEAD_DIM ** -0.5
        for h, hs in enumerate(heads):
            zh = z[:, hs]
            r = zh * cos_ref[...] + pltpu.roll(zh, HEAD_DIM // 2, 1) * sin_ref[...]
            qk_ref[:, col * HEAD_W + h * HEAD_DIM:col * HEAD_W + (h + 1) * HEAD_DIM] = (
                r * scale).astype(BF16)

    def conv(col, z):
        n = col - 4
        cs = slice(n * HEAD_W, (n + 1) * HEAD_W)
        xp = jnp.concatenate([hist_s[:, cs], z], axis=0)
        acc = z * convw_ref[CONV_W - 1:CONV_W, cs]
        for i in range(CONV_W - 1):
            acc = acc + pltpu.roll(xp, CONV_W - 1 - i, 0)[SUBLANE:] * convw_ref[i:i + 1, cs]
        hist_s[:, cs] = z[tm - SUBLANE:, :]
        c = _silu(acc)
        for h, hs in enumerate(heads):
            ch = c[:, hs]
            if n == 0:
                ch = ch * lax.rsqrt(jnp.sum(ch * ch, axis=-1, keepdims=True) + EPS) * (HEAD_DIM ** -0.5)
            elif n == 1:
                ch = ch * lax.rsqrt(jnp.sum(ch * ch, axis=-1, keepdims=True) + EPS)
            lo = (col - 2) * HEAD_W + h * HEAD_DIM
            rest_ref[:, lo:lo + HEAD_DIM] = ch

    def plain(col, z):
        rest_ref[:, 0:HEAD_W] = z

    def gate(col, z):
        lo = HEAD_W if col == 3 else 5 * HEAD_W
        rest_ref[:, lo:lo + HEAD_W] = _silu(z)

    plan = ((4, conv), (5, conv), (6, conv), (0, rope), (1, rope), (3, gate), (7, gate), (2, plain))
    z_next = proj(plan[0][0])
    for idx, (col, finish) in enumerate(plan):
        z = z_next
        if idx + 1 < len(plan):
            z_next = proj(plan[idx + 1][0])
        finish(col, z)

    rest_ref[:, 6 * HEAD_W:] = jnp.dot(xn, wba_ref[...], preferred_element_type=F32)
    zt_ref[...] = lax.dot_general(wbat_ref[...], xn, (((1,), (1,)), ((), ())),
                                  preferred_element_type=F32)

    @pl.when(j == pl.num_programs(1) - 1)
    def _():
        convnew_ref[...] = hist_s[SUBLANE - (CONV_W - 1):SUBLANE, :]


def _in_proj(x, l, n1g, w, wba, wbat, cos_t, sin_t, conv_w, conv0, l0, tm):
    bsz, t, _ = x.shape

    def layer(shape):
        return pl.BlockSpec((None,) + shape, lambda i, j: (l,) + (0,) * len(shape))

    return pl.pallas_call(
        _in_proj_kernel,
        grid=(bsz, t // tm),
        in_specs=[
            pl.BlockSpec((None, tm, D_MODEL), lambda i, j: (i, j, 0)),
            layer((1, D_MODEL)),
            layer((D_MODEL, Z_MAIN)),
            layer((D_MODEL, LANE)),
            layer((SUBLANE, D_MODEL)),
            pl.BlockSpec((tm, LANE), lambda i, j: (j, 0)),
            pl.BlockSpec((tm, LANE), lambda i, j: (j, 0)),
            layer((CONV_W, CONV_CH)),
            pl.BlockSpec((None, None, CONV_W - 1, CONV_CH), lambda i, j: (l0, i, 0, 0)),
        ],
        out_specs=[
            pl.BlockSpec((None, tm, 2 * HEAD_W), lambda i, j: (i, j, 0)),
            pl.BlockSpec((None, tm, REST_W), lambda i, j: (i, j, 0)),
            pl.BlockSpec((None, SUBLANE, tm), lambda i, j: (i, 0, j)),
            pl.BlockSpec((None, CONV_W - 1, CONV_CH), lambda i, j: (i, 0, 0)),
        ],
        out_shape=[
            jax.ShapeDtypeStruct((bsz, t, 2 * HEAD_W), BF16),
            jax.ShapeDtypeStruct((bsz, t, REST_W), F32),
            jax.ShapeDtypeStruct((bsz, SUBLANE, t), F32),
            jax.ShapeDtypeStruct((bsz, CONV_W - 1, CONV_CH), F32),
        ],
        scratch_shapes=[pltpu.VMEM((SUBLANE, CONV_CH), F32)],
        compiler_params=pltpu.CompilerParams(
            dimension_semantics=("arbitrary", "arbitrary"), vmem_limit_bytes=VMEM_LIMIT),
        name="in_proj",
    )(x, n1g, w, wba, wbat, cos_t, sin_t, conv_w, conv0)


def _bmm(a, b):
    return jnp.einsum("gmk,gkn->gmn", a.astype(BF16), b.astype(BF16), preferred_element_type=F32)


def _bmm_nt(a, b):
    return jnp.einsum("gmk,gnk->gmn", a.astype(BF16), b.astype(BF16), preferred_element_type=F32)


def _bmm_tn(a, b):
    return jnp.einsum("gkm,gkn->gmn", a.astype(BF16), b.astype(BF16), preferred_element_type=F32)


def _unit_lower_inverse(a):
    n = a.shape[-1]
    ri = lax.broadcasted_iota(jnp.int32, (n, n), 0)
    ci = lax.broadcasted_iota(jnp.int32, (n, n), 1)
    eye = (ri == ci).astype(F32)[None]
    inner = ((ri // 2) == (ci // 2))[None]
    p = eye - jnp.where(inner, a, 0.0)
    size = 2
    while size < n:
        size *= 2
        outer = ((ri // size) == (ci // size))[None]
        e = jnp.where(outer & jnp.logical_not(inner), a, 0.0)
        p = p - _bmm(p, _bmm(e, p))
        inner = outer
    return p


def _mixer_kernel(qr_ref, kr_ref, va_ref, ga_ref, qb_ref, kb_ref, vb_ref, gb_ref, ba_ref, zt_ref,
                  intra_ref, rc_ref, hpc_ref, hpr_ref, retg_ref, dng_ref, sret0_ref, sdn0_ref,
                  mix_ref, sret_out, sdn_out, sret_s, sdn_s, *, bb, cb):
    step = pl.program_id(1)
    ng = bb * HEADS
    state_shape = (ng, HEAD_DIM, HEAD_DIM)

    @pl.when(step == 0)
    def _():
        sret_s[...] = sret0_ref[...].reshape(state_shape)
        sdn_s[...] = sdn0_ref[...].reshape(state_shape)

    ri = lax.broadcasted_iota(jnp.int32, (CHUNK, CHUNK), 0)
    ci = lax.broadcasted_iota(jnp.int32, (CHUNK, CHUNK), 1)
    causal = ri >= ci
    strict = ri > ci
    ltri = causal.astype(BF16)
    utri = (ri <= ci).astype(BF16)
    dot = functools.partial(jnp.dot, preferred_element_type=F32)

    hpc = hpc_ref[...]
    hpr = hpr_ref[...]
    neg_a_col = -jnp.exp(hpc[0:1, :])
    neg_a_row = -jnp.exp(hpr[:, 0:1])
    dn_g = dng_ref[...]
    ret_g = retg_ref[...]
    heads = [slice(h * HEAD_DIM, (h + 1) * HEAD_DIM) for h in range(HEADS)]
    reps = cb * bb
    intra = jnp.concatenate([intra_ref[...]] * reps, axis=0)
    inject = jnp.concatenate([rc_ref[1]] * reps, axis=0)
    cross = jnp.concatenate([rc_ref[0]] * bb, axis=0)
    carry = jnp.concatenate([rc_ref[2, :, 0:1, :]] * bb, axis=0)
    ret_gain = jnp.stack([ret_g[:, hs] for hs in heads] * reps)

    def lanes(col, j):
        return jnp.broadcast_to(col[:, j:j + 1], (CHUNK, HEAD_DIM))

    def heads_of(ref):
        return jnp.stack([ref[b, pl.ds(c * CHUNK, CHUNK), hs]
                          for c in range(cb) for b in range(bb) for hs in heads])

    beta, gc_c, gc_r = [], [], []
    for c in range(cb):
        r0 = c * CHUNK
        for b in range(bb):
            ba = ba_ref[b, pl.ds(r0, CHUNK), :]
            zt = zt_ref[b, :, r0:r0 + CHUNK]
            beta_col = _sigmoid(ba)
            g_col = neg_a_col * _softplus(ba + hpc[1:2, :])
            g_row = neg_a_row * _softplus(zt + hpr[:, 1:2])
            gh, gm, gl = _split3(g_col)
            gc_col = dot(ltri, gh) + (dot(ltri, gm) + dot(ltri, gl))
            gh, gm, gl = _split3(g_row)
            gc_row = dot(gh, utri) + (dot(gm, utri) + dot(gl, utri))
            for h in range(HEADS):
                beta.append(lanes(beta_col, h))
                gc_c.append(lanes(gc_col, HEADS + h))
                gc_r.append(gc_row[HEADS + h:HEADS + h + 1, :])

    q_r, k_r, v_r = heads_of(qr_ref), heads_of(kr_ref), heads_of(va_ref)
    o_intra = _bmm(_bmm_nt(q_r, k_r) * intra, v_r)
    s_inc = _bmm_tn(k_r, v_r * inject)

    q, k, v = heads_of(qb_ref), heads_of(kb_ref), heads_of(vb_ref)
    beta, gc_c = jnp.stack(beta), jnp.stack(gc_c)
    gc_last = gc_c[:, CHUNK - 1:CHUNK, :]
    e_gc = jnp.exp(gc_c)
    decay = jnp.exp(jnp.where(causal[None], gc_c[:, :, :CHUNK] - jnp.stack(gc_r), -jnp.inf))
    kbeta = k * beta
    kk = _bmm_nt(jnp.concatenate([kbeta, q], axis=1), k)
    a = jnp.where(strict[None], kk[:, :CHUNK] * decay, 0.0)
    attn = kk[:, CHUNK:] * decay
    tinv = _unit_lower_inverse(a)
    sol = _bmm(tinv, jnp.concatenate([v * beta, kbeta * e_gc], axis=2))
    u = sol[:, :, :HEAD_DIM]
    wq = jnp.concatenate([sol[:, :, HEAD_DIM:], q * e_gc], axis=1)
    kg = k * jnp.exp(gc_last - gc_c)
    g_end = jnp.exp(gc_last)

    s_ret = sret_s[...]
    s_dn = sdn_s[...]
    o_a, o_b = [], []
    for c in range(cb):
        sl = slice(c * ng, (c + 1) * ng)
        o_a.append(o_intra[sl] + _bmm(q_r[sl], s_ret) * cross)
        s_ret = s_ret * carry + s_inc[sl]
        ws = _bmm(wq[sl], s_dn)
        v_new = u[sl] - ws[:, :CHUNK]
        o_b.append(ws[:, CHUNK:] + _bmm(attn[sl], v_new))
        s_dn = s_dn * g_end[sl] + _bmm_tn(kg[sl], v_new)
    sret_s[...] = s_ret
    sdn_s[...] = s_dn

    o = jnp.concatenate(o_a, axis=0)
    mu = jnp.mean(o, axis=-1, keepdims=True)
    oc = o - mu
    var = jnp.mean(oc * oc, axis=-1, keepdims=True)
    o_a = oc * lax.rsqrt(var + EPS) * ret_gain * heads_of(ga_ref)
    o_b = _rms(jnp.concatenate(o_b, axis=0), dn_g[None]) * heads_of(gb_ref)
    for c in range(cb):
        rows = pl.ds(c * CHUNK, CHUNK)
        for b in range(bb):
            for h, hs in enumerate(heads):
                g = (c * bb + b) * HEADS + h
                mix_ref[b, rows, hs] = o_a[g]
                mix_ref[b, rows, HEAD_W + h * HEAD_DIM:HEAD_W + (h + 1) * HEAD_DIM] = o_b[g]

    @pl.when(step == pl.num_programs(1) - 1)
    def _():
        sret_out[...] = sret_s[...].reshape(sret_out.shape)
        sdn_out[...] = sdn_s[...].reshape(sdn_out.shape)


def _mixer(qk, rest, zt, l, intra, rc, hpc, hpr, ret_g, dn_g, sret0, sdn0, l0, bb, cb):
    bsz, t, _ = qk.shape
    blk = cb * CHUNK
    grid = (bsz // bb, t // blk)

    def cols(col):
        return pl.BlockSpec((bb, blk, HEAD_W), lambda i, j: (i, j, col))

    def const(shape):
        return pl.BlockSpec(shape, lambda i, j: (0,) * len(shape))

    def layer(shape):
        return pl.BlockSpec((None,) + shape, lambda i, j: (l,) + (0,) * len(shape))

    state_shape = (bb, HEADS, HEAD_DIM, HEAD_DIM)
    state_in = pl.BlockSpec((None,) + state_shape, lambda i, j: (l0, i, 0, 0, 0))
    state_out = pl.BlockSpec(state_shape, lambda i, j: (i, 0, 0, 0))
    kern = functools.partial(_mixer_kernel, bb=bb, cb=cb)
    return pl.pallas_call(
        kern,
        grid=grid,
        in_specs=[cols(0), cols(1)] + [cols(c) for c in range(6)] + [
            pl.BlockSpec((bb, blk, LANE), lambda i, j: (i, j, 6 * HEAD_W // LANE)),
            pl.BlockSpec((bb, SUBLANE, blk), lambda i, j: (i, 0, j)),
            const((HEADS, CHUNK, CHUNK)),
            const((3, HEADS, CHUNK, LANE)),
            layer((SUBLANE, LANE)),
            layer((SUBLANE, LANE)),
            layer((1, HEAD_W)),
            layer((1, HEAD_DIM)),
            state_in, state_in,
        ],
        out_specs=[
            pl.BlockSpec((bb, blk, 2 * HEAD_W), lambda i, j: (i, j, 0)),
            state_out, state_out,
        ],
        out_shape=[
            jax.ShapeDtypeStruct((bsz, t, 2 * HEAD_W), F32),
            jax.ShapeDtypeStruct((bsz, HEADS, HEAD_DIM, HEAD_DIM), F32),
            jax.ShapeDtypeStruct((bsz, HEADS, HEAD_DIM, HEAD_DIM), F32),
        ],
        scratch_shapes=[
            pltpu.VMEM((bb * HEADS, HEAD_DIM, HEAD_DIM), F32),
            pltpu.VMEM((bb * HEADS, HEAD_DIM, HEAD_DIM), F32),
        ],
        compiler_params=pltpu.CompilerParams(
            dimension_semantics=("arbitrary", "arbitrary"), vmem_limit_bytes=VMEM_LIMIT),
        name="mixer",
    )(qk, qk, *([rest] * 7), zt, intra, rc, hpc, hpr, ret_g, dn_g, sret0, sdn0)


def _post_kernel(h_ref, mix_ref, p_ref, wout_ref, n2g_ref, wup_ref, wdown_ref, proj_ref,
                 pleg_ref, gatew_ref, gateb_ref, fing_ref, out_ref, acc_s, xn_s, *, final):
    j = pl.program_id(1)

    @pl.when(j == 0)
    def _():
        h1 = h_ref[...] + jnp.dot(mix_ref[...].astype(BF16), wout_ref[...],
                                  preferred_element_type=F32)
        acc_s[...] = h1
        xn_s[...] = _rms(h1, n2g_ref[...]).astype(BF16)

    up = jnp.maximum(jnp.dot(xn_s[...], wup_ref[...], preferred_element_type=F32), 0.0)
    acc_s[...] += jnp.dot((up * up).astype(BF16), wdown_ref[...], preferred_element_type=F32)

    @pl.when(j == pl.num_programs(1) - 1)
    def _():
        h2 = acc_s[...]
        gate = _sigmoid(jnp.dot(_rms(h2, pleg_ref[...]).astype(BF16), gatew_ref[...],
                                preferred_element_type=F32) + gateb_ref[...])
        h3 = h2 + jnp.dot(p_ref[...].astype(BF16), proj_ref[...],
                          preferred_element_type=F32) * gate
        if final:
            h3 = _rms(h3, fing_ref[...])
        out_ref[...] = h3


def _post(h, mix, p, l, w_out, n2g, w_up, w_down, proj, pleg, gate_w, gate_b, fin_g, final, tm, tf):
    m = h.shape[0]

    def row(width):
        return pl.BlockSpec((tm, width), lambda i, j: (i, 0))

    def layer(shape):
        return pl.BlockSpec((None,) + shape, lambda i, j: (l,) + (0,) * len(shape))

    return pl.pallas_call(
        functools.partial(_post_kernel, final=final),
        grid=(m // tm, D_FF // tf),
        in_specs=[
            row(D_MODEL), row(2 * HEAD_W),
            pl.BlockSpec((None, tm, PLE_DIM), lambda i, j: (l, i, 0)),
            layer((2 * HEAD_W, D_MODEL)),
            layer((1, D_MODEL)),
            pl.BlockSpec((None, D_MODEL, tf), lambda i, j: (l, 0, j)),
            pl.BlockSpec((None, tf, D_MODEL), lambda i, j: (l, j, 0)),
            layer((PLE_DIM, D_MODEL)),
            layer((1, D_MODEL)),
            layer((D_MODEL, D_MODEL)),
            layer((1, D_MODEL)),
            pl.BlockSpec((1, D_MODEL), lambda i, j: (0, 0)),
        ],
        out_specs=row(D_MODEL),
        out_shape=jax.ShapeDtypeStruct((m, D_MODEL), F32),
        scratch_shapes=[pltpu.VMEM((tm, D_MODEL), F32), pltpu.VMEM((tm, D_MODEL), BF16)],
        compiler_params=pltpu.CompilerParams(
            dimension_semantics=("arbitrary", "arbitrary"), vmem_limit_bytes=VMEM_LIMIT),
        name="post",
    )(h, mix, p, w_out, n2g, w_up, w_down, proj, pleg, gate_w, gate_b, fin_g)


def _rope_tables(pos0, t):
    half = HEAD_DIM // 2
    inv = ROPE_BASE ** (-jnp.arange(half, dtype=F32) / half)
    ang = (pos0 + jnp.arange(t)).astype(F32)[:, None] * inv[None, :]
    cos, sin = jnp.cos(ang), jnp.sin(ang)
    return jnp.concatenate([cos, cos], axis=-1), jnp.concatenate([-sin, sin], axis=-1)


def _retention_tables():
    log_gamma = jnp.log(1.0 - 2.0 ** (-5.0 - jnp.arange(HEADS, dtype=F32)))
    idx = jnp.arange(CHUNK, dtype=F32)
    intra = jnp.exp(jnp.abs(idx[:, None] - idx[None, :])[None] * log_gamma[:, None, None])
    cross = jnp.exp((idx + 1.0)[None, :] * log_gamma[:, None])
    inject = jnp.exp((CHUNK - 1.0 - idx)[None, :] * log_gamma[:, None])
    carry = jnp.exp(CHUNK * log_gamma)
    carry = jnp.broadcast_to(carry[:, None], (HEADS, CHUNK))
    rc = jnp.broadcast_to(jnp.stack([cross, inject, carry])[..., None], (3, HEADS, CHUNK, LANE))
    return intra, rc


def _trunk(x, p, conv0, sret0, sdn0, zero_state_layer, pos0, wts, intra, rc, bb, cb, tm_in, tm, tf):
    bsz, t, _ = x.shape
    m = bsz * t
    depth = wts["w_in"].shape[0]
    cos_t, sin_t = _rope_tables(pos0, t)
    p = p.reshape(depth, m, PLE_DIM)
    h = x
    rets, dns, convs = [], [], []
    for l in range(depth):
        l0 = 0 if zero_state_layer else l
        qk, rest, zt, conv = _in_proj(h.reshape(bsz, t, D_MODEL), l, wts["n1g"], wts["w_in"],
                                      wts["w_ba"], wts["w_ba_t"], cos_t, sin_t, wts["conv_w"],
                                      conv0, l0, tm_in)
        mix, sret, sdn = _mixer(qk, rest, zt, l, intra, rc, wts["hpc"], wts["hpr"], wts["ret_g"],
                                wts["dn_g"], sret0, sdn0, l0, bb, cb)
        h = _post(h.reshape(m, D_MODEL), mix.reshape(m, 2 * HEAD_W), p, l, wts["w_out"], wts["n2g"],
                  wts["w_up"], wts["w_down"], wts["proj"], wts["pleg"], wts["gate_w"], wts["gate_b"],
                  wts["fin_g"], l == depth - 1, tm, tf)
        rets.append(sret)
        dns.append(sdn)
        convs.append(conv)
    return h.reshape(bsz, t, D_MODEL), jnp.stack(rets), jnp.stack(dns), jnp.stack(convs)


def kernel(x_prompt, x_sample, p_prompt, p_sample, state_ret, state_delta, state_conv, norm1_g, w_in, conv_w, dt_bias, a_log, ret_norm_g, dn_norm_g, w_out, norm2_g, w_up, w_down, ple_proj, ple_norm_g, ple_gate_w, ple_gate_b, final_norm_g):
    depth = w_in.shape[0]
    w_in_b = w_in.astype(BF16)
    w_gate_cols = w_in_b[:, :, Z_MAIN:]
    hpc = jnp.zeros((depth, SUBLANE, LANE), F32)
    hpc = hpc.at[:, 0, HEADS:N_GATE].set(a_log).at[:, 1, HEADS:N_GATE].set(dt_bias)
    hpr = jnp.zeros((depth, SUBLANE, LANE), F32)
    hpr = hpr.at[:, HEADS:N_GATE, 0].set(a_log).at[:, HEADS:N_GATE, 1].set(dt_bias)
    wts = dict(
        n1g=norm1_g[:, None, :], w_in=w_in_b,
        w_ba=jnp.pad(w_gate_cols, ((0, 0), (0, 0), (0, LANE - N_GATE))),
        w_ba_t=jnp.swapaxes(w_gate_cols, 1, 2),
        conv_w=conv_w, hpc=hpc, hpr=hpr,
        ret_g=ret_norm_g[:, None, :], dn_g=dn_norm_g[:, None, :],
        w_out=w_out.astype(BF16), n2g=norm2_g[:, None, :],
        w_up=w_up.astype(BF16), w_down=w_down.astype(BF16),
        proj=ple_proj.astype(BF16), pleg=ple_norm_g[:, None, :],
        gate_w=ple_gate_w.astype(BF16), gate_b=ple_gate_b[:, None, :],
        fin_g=final_norm_g[None, :])
    intra, rc = _retention_tables()
    bp = x_prompt.shape[0]
    zero_state = jnp.zeros((1, bp, HEADS, HEAD_DIM, HEAD_DIM), F32)
    zero_conv = jnp.zeros((1, bp, CONV_W - 1, CONV_CH), F32)
    y_p, ret_p, dn_p, conv_p = _trunk(x_prompt, p_prompt, zero_conv, zero_state, zero_state, True, 0,
                                      wts, intra, rc, bb=2, cb=4, tm_in=512, tm=512, tf=1024)
    y_s, ret_s, dn_s, conv_s = _trunk(x_sample, p_sample, state_conv, state_ret, state_delta, False,
                                      PAST_LEN, wts, intra, rc, bb=4, cb=1, tm_in=CHUNK, tm=512,
                                      tf=1024)
    return (y_p, y_s, ret_p, dn_p, conv_p, ret_s, dn_s, conv_s)
```

```python
import functools

import jax
import jax.numpy as jnp
from jax import lax
from jax.experimental import pallas as pl
from jax.experimental.pallas import tpu as pltpu

F32 = jnp.float32
BF16 = jnp.bfloat16

D_MODEL = 1024
CHUNK = 64
HEADS = 4
HEAD_DIM = 128
HEAD_W = HEADS * HEAD_DIM
PLE_DIM = 256
D_FF = 4 * D_MODEL
CONV_W = 4
CONV_CH = 3 * HEAD_W
ROPE_BASE = 10000.0
EPS = 1e-6
PAST_LEN = 4096
Z_MAIN = 8 * HEAD_W
N_GATE = 2 * HEADS
LANE = 128
SUBLANE = 8
REST_W = 6 * HEAD_W + LANE
VMEM_LIMIT = 56 * 1024 * 1024


def _split3(x):
    hi = x.astype(BF16)
    r = x - hi.astype(F32)
    mid = r.astype(BF16)
    lo = (r - mid.astype(F32)).astype(BF16)
    return hi, mid, lo


def _rms(x, g):
    return x * lax.rsqrt(jnp.mean(x * x, axis=-1, keepdims=True) + EPS) * g


def _softplus(x):
    return jnp.maximum(x, 0.0) + jnp.log1p(jnp.exp(-jnp.abs(x)))


def _sigmoid(x):
    return 0.5 + 0.5 * jnp.tanh(0.5 * x)


def _silu(x):
    h = 0.5 * x
    return h + h * jnp.tanh(h)


def _in_proj_kernel(x_ref, g_ref, w_ref, wba_ref, cos_ref, sin_ref, convw_ref, conv0_ref,
                    qk_ref, rest_ref, zt_ref, convnew_ref, hist_s):
    j = pl.program_id(1)
    nb, tm, _ = x_ref.shape
    xn = _rms(x_ref[...].reshape(nb * tm, D_MODEL), g_ref[...]).astype(BF16)
    pair_w = 2 * HEAD_DIM

    def proj(col, pair):
        lo = col * HEAD_W + pair * pair_w
        return jnp.dot(xn, w_ref[:, lo:lo + pair_w], preferred_element_type=F32)

    @pl.when(j == 0)
    def _():
        hist_s[:, SUBLANE - (CONV_W - 1):SUBLANE, :] = conv0_ref[...]

    def rope(col, h, z, r):
        scale = 1.0 if col == 0 else HEAD_DIM ** -0.5
        out = (z * cos_ref[...] + pltpu.roll(z, HEAD_DIM // 2, 1) * sin_ref[...]) * scale
        lo = col * HEAD_W + h * HEAD_DIM
        qk_ref[r, :, lo:lo + HEAD_DIM] = out.astype(BF16)

    def conv(col, h, z, r):
        n = col - 4
        cs = slice(n * HEAD_W + h * HEAD_DIM, n * HEAD_W + (h + 1) * HEAD_DIM)
        xp = jnp.concatenate([hist_s[r, :, cs], z], axis=0)
        acc = z * convw_ref[CONV_W - 1:CONV_W, cs]
        for i in range(CONV_W - 1):
            acc = acc + pltpu.roll(xp, CONV_W - 1 - i, 0)[SUBLANE:] * convw_ref[i:i + 1, cs]
        hist_s[r, :, cs] = z[tm - SUBLANE:, :]
        c = _silu(acc)
        if n == 0:
            c = c * lax.rsqrt(jnp.sum(c * c, axis=-1, keepdims=True) + EPS) * (HEAD_DIM ** -0.5)
        elif n == 1:
            c = c * lax.rsqrt(jnp.sum(c * c, axis=-1, keepdims=True) + EPS)
        lo = (col - 2) * HEAD_W + h * HEAD_DIM
        rest_ref[r, :, lo:lo + HEAD_DIM] = c

    def plain(col, h, z, r):
        rest_ref[r, :, h * HEAD_DIM:(h + 1) * HEAD_DIM] = z

    def gate(col, h, z, r):
        lo = (HEAD_W if col == 3 else 5 * HEAD_W) + h * HEAD_DIM
        rest_ref[r, :, lo:lo + HEAD_DIM] = _silu(z)

    finishers = {0: rope, 1: rope, 2: plain, 3: gate, 4: conv, 5: conv, 6: conv, 7: gate}

    def finish(col, pair, z):
        for hh in range(2):
            for r in range(nb):
                finishers[col](col, 2 * pair + hh,
                               z[r * tm:(r + 1) * tm, hh * HEAD_DIM:(hh + 1) * HEAD_DIM], r)

    order = ((4, 0), (0, 0), (4, 1), (0, 1), (5, 0), (1, 0), (5, 1), (1, 1),
             (6, 0), (3, 0), (6, 1), (3, 1), (7, 0), (7, 1), (2, 0), (2, 1))
    pending = None
    for col, pair in order:
        z = proj(col, pair)
        if pending is not None:
            finish(*pending)
        pending = (col, pair, z)
    finish(*pending)

    ba = jnp.dot(xn, wba_ref[...], preferred_element_type=F32)
    zt = ba.T[:SUBLANE]
    for r in range(nb):
        rest_ref[r, :, 6 * HEAD_W:] = ba[r * tm:(r + 1) * tm]
        zt_ref[r] = zt[:, r * tm:(r + 1) * tm]

    @pl.when(j == pl.num_programs(1) - 1)
    def _():
        convnew_ref[...] = hist_s[:, SUBLANE - (CONV_W - 1):SUBLANE, :]


def _in_proj(x, l, n1g, w, wba, cos_t, sin_t, conv_w, conv0, l0, nb, tm):
    bsz, t, _ = x.shape

    def layer(shape):
        return pl.BlockSpec((None,) + shape, lambda i, j: (l,) + (0,) * len(shape))

    return pl.pallas_call(
        _in_proj_kernel,
        grid=(bsz // nb, t // tm),
        in_specs=[
            pl.BlockSpec((nb, tm, D_MODEL), lambda i, j: (i, j, 0)),
            layer((1, D_MODEL)),
            layer((D_MODEL, Z_MAIN)),
            layer((D_MODEL, LANE)),
            pl.BlockSpec((tm, LANE), lambda i, j: (j, 0)),
            pl.BlockSpec((tm, LANE), lambda i, j: (j, 0)),
            layer((CONV_W, CONV_CH)),
            pl.BlockSpec((None, nb, CONV_W - 1, CONV_CH), lambda i, j: (l0, i, 0, 0)),
        ],
        out_specs=[
            pl.BlockSpec((nb, tm, 2 * HEAD_W), lambda i, j: (i, j, 0)),
            pl.BlockSpec((nb, tm, REST_W), lambda i, j: (i, j, 0)),
            pl.BlockSpec((nb, SUBLANE, tm), lambda i, j: (i, 0, j)),
            pl.BlockSpec((nb, CONV_W - 1, CONV_CH), lambda i, j: (i, 0, 0)),
        ],
        out_shape=[
            jax.ShapeDtypeStruct((bsz, t, 2 * HEAD_W), BF16),
            jax.ShapeDtypeStruct((bsz, t, REST_W), F32),
            jax.ShapeDtypeStruct((bsz, SUBLANE, t), F32),
            jax.ShapeDtypeStruct((bsz, CONV_W - 1, CONV_CH), F32),
        ],
        scratch_shapes=[pltpu.VMEM((nb, SUBLANE, CONV_CH), F32)],
        compiler_params=pltpu.CompilerParams(
            dimension_semantics=("arbitrary", "arbitrary"), vmem_limit_bytes=VMEM_LIMIT),
        name="in_proj",
    )(x, n1g, w, wba, cos_t, sin_t, conv_w, conv0)


def _bmm(a, b):
    return jnp.einsum("gmk,gkn->gmn", a.astype(BF16), b.astype(BF16), preferred_element_type=F32)


def _bmm_nt(a, b):
    return jnp.einsum("gmk,gnk->gmn", a.astype(BF16), b.astype(BF16), preferred_element_type=F32)


def _bmm_tn(a, b):
    return jnp.einsum("gkm,gkn->gmn", a.astype(BF16), b.astype(BF16), preferred_element_type=F32)


def _unit_lower_inverse(a):
    n = a.shape[-1]
    ri = lax.broadcasted_iota(jnp.int32, (n, n), 0)
    ci = lax.broadcasted_iota(jnp.int32, (n, n), 1)
    eye = (ri == ci).astype(F32)[None]
    inner = ((ri // 2) == (ci // 2))[None]
    p = eye - jnp.where(inner, a, 0.0)
    size = 2
    while size < n:
        size *= 2
        outer = ((ri // size) == (ci // size))[None]
        e = jnp.where(outer & jnp.logical_not(inner), a, 0.0)
        p = p - _bmm(p, _bmm(e, p))
        inner = outer
    return p


def _mixer_kernel(qr_ref, kr_ref, va_ref, ga_ref, qb_ref, kb_ref, vb_ref, gb_ref, ba_ref, zt_ref,
                  intra_ref, rc_ref, hpc_ref, hpr_ref, retg_ref, dng_ref, sret0_ref, sdn0_ref,
                  mix_ref, sret_out, sdn_out, sret_s, sdn_s, *, bb, cb):
    step = pl.program_id(1)
    ng = bb * HEADS
    state_shape = (ng, HEAD_DIM, HEAD_DIM)

    @pl.when(step == 0)
    def _():
        sret_s[...] = sret0_ref[...].reshape(state_shape)
        sdn_s[...] = sdn0_ref[...].reshape(state_shape)

    ri = lax.broadcasted_iota(jnp.int32, (CHUNK, CHUNK), 0)
    ci = lax.broadcasted_iota(jnp.int32, (CHUNK, CHUNK), 1)
    causal = ri >= ci
    strict = ri > ci
    ltri = causal.astype(BF16)
    utri = (ri <= ci).astype(BF16)
    dot = functools.partial(jnp.dot, preferred_element_type=F32)

    hpc = hpc_ref[...]
    hpr = hpr_ref[...]
    neg_a_col = -jnp.exp(hpc[0:1, :])
    neg_a_row = -jnp.exp(hpr[:, 0:1])
    dn_g = dng_ref[...]
    ret_g = retg_ref[...]
    heads = [slice(h * HEAD_DIM, (h + 1) * HEAD_DIM) for h in range(HEADS)]
    reps = cb * bb
    intra = jnp.concatenate([intra_ref[...]] * reps, axis=0)
    inject = jnp.concatenate([rc_ref[1]] * reps, axis=0)
    cross = jnp.concatenate([rc_ref[0]] * bb, axis=0)
    carry = jnp.concatenate([rc_ref[2, :, 0:1, :]] * bb, axis=0)
    ret_gain = jnp.stack([ret_g[:, hs] for hs in heads] * reps)

    def lanes(col, j):
        return jnp.broadcast_to(col[:, j:j + 1], (CHUNK, HEAD_DIM))

    def heads_of(ref):
        return jnp.stack([ref[b, pl.ds(c * CHUNK, CHUNK), hs]
                          for c in range(cb) for b in range(bb) for hs in heads])

    beta, gc_c, gc_r = [], [], []
    for c in range(cb):
        r0 = c * CHUNK
        for b in range(bb):
            ba = ba_ref[b, pl.ds(r0, CHUNK), :]
            zt = zt_ref[b, :, r0:r0 + CHUNK]
            beta_col = _sigmoid(ba)
            g_col = neg_a_col * _softplus(ba + hpc[1:2, :])
            g_row = neg_a_row * _softplus(zt + hpr[:, 1:2])
            gh, gm, gl = _split3(g_col)
            gc_col = dot(ltri, gh) + (dot(ltri, gm) + dot(ltri, gl))
            gh, gm, gl = _split3(g_row)
            gc_row = dot(gh, utri) + (dot(gm, utri) + dot(gl, utri))
            for h in range(HEADS):
                beta.append(lanes(beta_col, h))
                gc_c.append(lanes(gc_col, HEADS + h))
                gc_r.append(gc_row[HEADS + h:HEADS + h + 1, :])

    q_r, k_r, v_r = heads_of(qr_ref), heads_of(kr_ref), heads_of(va_ref)
    o_intra = _bmm(_bmm_nt(q_r, k_r) * intra, v_r)
    s_inc = _bmm_tn(k_r, v_r * inject)

    q, k, v = heads_of(qb_ref), heads_of(kb_ref), heads_of(vb_ref)
    beta, gc_c = jnp.stack(beta), jnp.stack(gc_c)
    gc_last = gc_c[:, CHUNK - 1:CHUNK, :]
    e_gc = jnp.exp(gc_c)
    decay = jnp.exp(jnp.where(causal[None], gc_c[:, :, :CHUNK] - jnp.stack(gc_r), -jnp.inf))
    kbeta = k * beta
    kk = _bmm_nt(jnp.concatenate([kbeta, q], axis=1), k)
    a = jnp.where(strict[None], kk[:, :CHUNK] * decay, 0.0)
    attn = kk[:, CHUNK:] * decay
    tinv = _unit_lower_inverse(a)
    sol = _bmm(tinv, jnp.concatenate([v * beta, kbeta * e_gc], axis=2))
    u = sol[:, :, :HEAD_DIM]
    wq = jnp.concatenate([sol[:, :, HEAD_DIM:], q * e_gc], axis=1)
    kg = k * jnp.exp(gc_last - gc_c)
    g_end = jnp.exp(gc_last)

    s_ret = sret_s[...]
    s_dn = sdn_s[...]
    o_a, o_b = [], []
    for c in range(cb):
        sl = slice(c * ng, (c + 1) * ng)
        o_a.append(o_intra[sl] + _bmm(q_r[sl], s_ret) * cross)
        s_ret = s_ret * carry + s_inc[sl]
        ws = _bmm(wq[sl], s_dn)
        v_new = u[sl] - ws[:, :CHUNK]
        o_b.append(ws[:, CHUNK:] + _bmm(attn[sl], v_new))
        s_dn = s_dn * g_end[sl] + _bmm_tn(kg[sl], v_new)
    sret_s[...] = s_ret
    sdn_s[...] = s_dn

    o = jnp.concatenate(o_a, axis=0)
    mu = jnp.mean(o, axis=-1, keepdims=True)
    oc = o - mu
    var = jnp.mean(oc * oc, axis=-1, keepdims=True)
    o_a = oc * lax.rsqrt(var + EPS) * ret_gain * heads_of(ga_ref)
    o_b = _rms(jnp.concatenate(o_b, axis=0), dn_g[None]) * heads_of(gb_ref)
    for c in range(cb):
        rows = pl.ds(c * CHUNK, CHUNK)
        for b in range(bb):
            for h, hs in enumerate(heads):
                g = (c * bb + b) * HEADS + h
                mix_ref[b, rows, hs] = o_a[g]
                mix_ref[b, rows, HEAD_W + h * HEAD_DIM:HEAD_W + (h + 1) * HEAD_DIM] = o_b[g]

    @pl.when(step == pl.num_programs(1) - 1)
    def _():
        sret_out[...] = sret_s[...].reshape(sret_out.shape)
        sdn_out[...] = sdn_s[...].reshape(sdn_out.shape)


def _mixer(qk, rest, zt, l, intra, rc, hpc, hpr, ret_g, dn_g, sret0, sdn0, l0, bb, cb):
    bsz, t, _ = qk.shape
    blk = cb * CHUNK
    grid = (bsz // bb, t // blk)

    def cols(col):
        return pl.BlockSpec((bb, blk, HEAD_W), lambda i, j: (i, j, col))

    def const(shape):
        return pl.BlockSpec(shape, lambda i, j: (0,) * len(shape))

    def layer(shape):
        return pl.BlockSpec((None,) + shape, lambda i, j: (l,) + (0,) * len(shape))

    state_shape = (bb, HEADS, HEAD_DIM, HEAD_DIM)
    state_in = pl.BlockSpec((None,) + state_shape, lambda i, j: (l0, i, 0, 0, 0))
    state_out = pl.BlockSpec(state_shape, lambda i, j: (i, 0, 0, 0))
    kern = functools.partial(_mixer_kernel, bb=bb, cb=cb)
    return pl.pallas_call(
        kern,
        grid=grid,
        in_specs=[cols(0), cols(1)] + [cols(c) for c in range(6)] + [
            pl.BlockSpec((bb, blk, LANE), lambda i, j: (i, j, 6 * HEAD_W // LANE)),
            pl.BlockSpec((bb, SUBLANE, blk), lambda i, j: (i, 0, j)),
            const((HEADS, CHUNK, CHUNK)),
            const((3, HEADS, CHUNK, LANE)),
            layer((SUBLANE, LANE)),
            layer((SUBLANE, LANE)),
            layer((1, HEAD_W)),
            layer((1, HEAD_DIM)),
            state_in, state_in,
        ],
        out_specs=[
            pl.BlockSpec((bb, blk, 2 * HEAD_W), lambda i, j: (i, j, 0)),
            state_out, state_out,
        ],
        out_shape=[
            jax.ShapeDtypeStruct((bsz, t, 2 * HEAD_W), F32),
            jax.ShapeDtypeStruct((bsz, HEADS, HEAD_DIM, HEAD_DIM), F32),
            jax.ShapeDtypeStruct((bsz, HEADS, HEAD_DIM, HEAD_DIM), F32),
        ],
        scratch_shapes=[
            pltpu.VMEM((bb * HEADS, HEAD_DIM, HEAD_DIM), F32),
            pltpu.VMEM((bb * HEADS, HEAD_DIM, HEAD_DIM), F32),
        ],
        compiler_params=pltpu.CompilerParams(
            dimension_semantics=("arbitrary", "arbitrary"), vmem_limit_bytes=VMEM_LIMIT),
        name="mixer",
    )(qk, qk, *([rest] * 7), zt, intra, rc, hpc, hpr, ret_g, dn_g, sret0, sdn0)


def _post_kernel(h_ref, mix_ref, p_ref, wout_ref, n2g_ref, wup_ref, wdown_ref, proj_ref,
                 pleg_ref, gatew_ref, gateb_ref, fing_ref, out_ref, acc_s, xn_s, *, final):
    j = pl.program_id(1)

    @pl.when(j == 0)
    def _():
        h1 = h_ref[...] + jnp.dot(mix_ref[...].astype(BF16), wout_ref[...],
                                  preferred_element_type=F32)
        acc_s[...] = h1
        xn_s[...] = _rms(h1, n2g_ref[...]).astype(BF16)

    up = jnp.maximum(jnp.dot(xn_s[...], wup_ref[...], preferred_element_type=F32), 0.0)
    acc_s[...] += jnp.dot((up * up).astype(BF16), wdown_ref[...], preferred_element_type=F32)

    @pl.when(j == pl.num_programs(1) - 1)
    def _():
        h2 = acc_s[...]
        gate = _sigmoid(jnp.dot(_rms(h2, pleg_ref[...]).astype(BF16), gatew_ref[...],
                                preferred_element_type=F32) + gateb_ref[...])
        h3 = h2 + jnp.dot(p_ref[...].astype(BF16), proj_ref[...],
                          preferred_element_type=F32) * gate
        if final:
            h3 = _rms(h3, fing_ref[...])
        out_ref[...] = h3


def _post(h, mix, p, l, w_out, n2g, w_up, w_down, proj, pleg, gate_w, gate_b, fin_g, final, tm, tf):
    m = h.shape[0]

    def row(width):
        return pl.BlockSpec((tm, width), lambda i, j: (i, 0))

    def layer(shape):
        return pl.BlockSpec((None,) + shape, lambda i, j: (l,) + (0,) * len(shape))

    return pl.pallas_call(
        functools.partial(_post_kernel, final=final),
        grid=(m // tm, D_FF // tf),
        in_specs=[
            row(D_MODEL), row(2 * HEAD_W),
            pl.BlockSpec((None, tm, PLE_DIM), lambda i, j: (l, i, 0)),
            layer((2 * HEAD_W, D_MODEL)),
            layer((1, D_MODEL)),
            pl.BlockSpec((None, D_MODEL, tf), lambda i, j: (l, 0, j)),
            pl.BlockSpec((None, tf, D_MODEL), lambda i, j: (l, j, 0)),
            layer((PLE_DIM, D_MODEL)),
            layer((1, D_MODEL)),
            layer((D_MODEL, D_MODEL)),
            layer((1, D_MODEL)),
            pl.BlockSpec((1, D_MODEL), lambda i, j: (0, 0)),
        ],
        out_specs=row(D_MODEL),
        out_shape=jax.ShapeDtypeStruct((m, D_MODEL), F32),
        scratch_shapes=[pltpu.VMEM((tm, D_MODEL), F32), pltpu.VMEM((tm, D_MODEL), BF16)],
        compiler_params=pltpu.CompilerParams(
            dimension_semantics=("arbitrary", "arbitrary"), vmem_limit_bytes=VMEM_LIMIT),
        name="post",
    )(h, mix, p, w_out, n2g, w_up, w_down, proj, pleg, gate_w, gate_b, fin_g)


def _rope_tables(pos0, t):
    half = HEAD_DIM // 2
    inv = ROPE_BASE ** (-jnp.arange(half, dtype=F32) / half)
    ang = (pos0 + jnp.arange(t)).astype(F32)[:, None] * inv[None, :]
    cos, sin = jnp.cos(ang), jnp.sin(ang)
    return jnp.concatenate([cos, cos], axis=-1), jnp.concatenate([-sin, sin], axis=-1)


def _retention_tables():
    log_gamma = jnp.log(1.0 - 2.0 ** (-5.0 - jnp.arange(HEADS, dtype=F32)))
    idx = jnp.arange(CHUNK, dtype=F32)
    intra = jnp.exp(jnp.abs(idx[:, None] - idx[None, :])[None] * log_gamma[:, None, None])
    cross = jnp.exp((idx + 1.0)[None, :] * log_gamma[:, None])
    inject = jnp.exp((CHUNK - 1.0 - idx)[None, :] * log_gamma[:, None])
    carry = jnp.exp(CHUNK * log_gamma)
    carry = jnp.broadcast_to(carry[:, None], (HEADS, CHUNK))
    rc = jnp.broadcast_to(jnp.stack([cross, inject, carry])[..., None], (3, HEADS, CHUNK, LANE))
    return intra, rc


def _trunk(x, p, conv0, sret0, sdn0, zero_state_layer, pos0, wts, intra, rc, bb, cb, nb_in, tm_in, tm,
           tf):
    bsz, t, _ = x.shape
    m = bsz * t
    depth = wts["w_in"].shape[0]
    cos_t, sin_t = _rope_tables(pos0, t)
    p = p.reshape(depth, m, PLE_DIM)
    h = x
    rets, dns, convs = [], [], []
    for l in range(depth):
        l0 = 0 if zero_state_layer else l
        qk, rest, zt, conv = _in_proj(h.reshape(bsz, t, D_MODEL), l, wts["n1g"], wts["w_in"],
                                      wts["w_ba"], cos_t, sin_t, wts["conv_w"],
                                      conv0, l0, nb_in, tm_in)
        mix, sret, sdn = _mixer(qk, rest, zt, l, intra, rc, wts["hpc"], wts["hpr"], wts["ret_g"],
                                wts["dn_g"], sret0, sdn0, l0, bb, cb)
        h = _post(h.reshape(m, D_MODEL), mix.reshape(m, 2 * HEAD_W), p, l, wts["w_out"], wts["n2g"],
                  wts["w_up"], wts["w_down"], wts["proj"], wts["pleg"], wts["gate_w"], wts["gate_b"],
                  wts["fin_g"], l == depth - 1, tm, tf)
        rets.append(sret)
        dns.append(sdn)
        convs.append(conv)
    return h.reshape(bsz, t, D_MODEL), jnp.stack(rets), jnp.stack(dns), jnp.stack(convs)


def kernel(x_prompt, x_sample, p_prompt, p_sample, state_ret, state_delta, state_conv, norm1_g, w_in, conv_w, dt_bias, a_log, ret_norm_g, dn_norm_g, w_out, norm2_g, w_up, w_down, ple_proj, ple_norm_g, ple_gate_w, ple_gate_b, final_norm_g):
    depth = w_in.shape[0]
    w_in_b = w_in[:, :, :Z_MAIN].astype(BF16)
    w_gate_cols = w_in[:, :, Z_MAIN:].astype(BF16)
    hpc = jnp.zeros((depth, SUBLANE, LANE), F32)
    hpc = hpc.at[:, 0, HEADS:N_GATE].set(a_log).at[:, 1, HEADS:N_GATE].set(dt_bias)
    hpr = jnp.zeros((depth, SUBLANE, LANE), F32)
    hpr = hpr.at[:, HEADS:N_GATE, 0].set(a_log).at[:, HEADS:N_GATE, 1].set(dt_bias)
    wts = dict(
        n1g=norm1_g[:, None, :], w_in=w_in_b,
        w_ba=jnp.pad(w_gate_cols, ((0, 0), (0, 0), (0, LANE - N_GATE))),
        conv_w=conv_w, hpc=hpc, hpr=hpr,
        ret_g=ret_norm_g[:, None, :], dn_g=dn_norm_g[:, None, :],
        w_out=w_out.astype(BF16), n2g=norm2_g[:, None, :],
        w_up=w_up.astype(BF16), w_down=w_down.astype(BF16),
        proj=ple_proj.astype(BF16), pleg=ple_norm_g[:, None, :],
        gate_w=ple_gate_w.astype(BF16), gate_b=ple_gate_b[:, None, :],
        fin_g=final_norm_g[None, :])
    intra, rc = _retention_tables()
    bp = x_prompt.shape[0]
    zero_state = jnp.zeros((1, bp, HEADS, HEAD_DIM, HEAD_DIM), F32)
    zero_conv = jnp.zeros((1, bp, CONV_W - 1, CONV_CH), F32)
    y_p, ret_p, dn_p, conv_p = _trunk(x_prompt, p_prompt, zero_conv, zero_state, zero_state, True, 0,
                                      wts, intra, rc, bb=2, cb=4, nb_in=1, tm_in=512, tm=512, tf=1024)
    y_s, ret_s, dn_s, conv_s = _trunk(x_sample, p_sample, state_conv, state_ret, state_delta, False,
                                      PAST_LEN, wts, intra, rc, bb=8, cb=1, nb_in=8, tm_in=CHUNK,
                                      tm=512, tf=1024)
    return (y_p, y_s, ret_p, dn_p, conv_p, ret_s, dn_s, conv_s)
```

```python
import functools

import jax
import jax.numpy as jnp
from jax import lax
from jax.experimental import pallas as pl
from jax.experimental.pallas import tpu as pltpu

F32 = jnp.float32
BF16 = jnp.bfloat16

D_MODEL = 1024
CHUNK = 64
HEADS = 4
HEAD_DIM = 128
HEAD_W = HEADS * HEAD_DIM
PLE_DIM = 256
D_FF = 4 * D_MODEL
CONV_W = 4
CONV_CH = 3 * HEAD_W
ROPE_BASE = 10000.0
EPS = 1e-6
PAST_LEN = 4096
Z_MAIN = 8 * HEAD_W
N_GATE = 2 * HEADS
LANE = 128
SUBLANE = 8
REST_W = 6 * HEAD_W + LANE
VMEM_LIMIT = 56 * 1024 * 1024


def _split3(x):
    hi = x.astype(BF16)
    r = x - hi.astype(F32)
    mid = r.astype(BF16)
    lo = (r - mid.astype(F32)).astype(BF16)
    return hi, mid, lo


def _rms(x, g):
    return x * lax.rsqrt(jnp.mean(x * x, axis=-1, keepdims=True) + EPS) * g


def _softplus(x):
    return jnp.maximum(x, 0.0) + jnp.log1p(jnp.exp(-jnp.abs(x)))


def _sigmoid(x):
    return 0.5 + 0.5 * jnp.tanh(0.5 * x)


def _silu(x):
    h = 0.5 * x
    return h + h * jnp.tanh(h)


def _cast_kernel(x_ref, o_ref):
    o_ref[...] = x_ref[...].astype(o_ref.dtype)


def _cast_leading_cols(w, ncols, rows_per_step):
    depth, rows, _ = w.shape
    spec = pl.BlockSpec((None, rows_per_step, ncols), lambda l, i: (l, i, 0))
    return pl.pallas_call(
        _cast_kernel,
        grid=(depth, rows // rows_per_step),
        in_specs=[spec],
        out_specs=spec,
        out_shape=jax.ShapeDtypeStruct((depth, rows, ncols), BF16),
        compiler_params=pltpu.CompilerParams(
            dimension_semantics=("arbitrary", "arbitrary"), vmem_limit_bytes=VMEM_LIMIT),
        name="cast_w_in",
    )(w)


def _in_proj_kernel(x_ref, g_ref, w_ref, wba_ref, cos_ref, sin_ref, convw_ref, conv0_ref,
                    qk_ref, rest_ref, zt_ref, convnew_ref, hist_s):
    j = pl.program_id(1)
    nb, tm, _ = x_ref.shape
    xn = _rms(x_ref[...].reshape(nb * tm, D_MODEL), g_ref[...]).astype(BF16)
    pair_w = 2 * HEAD_DIM

    def proj(col, pair):
        lo = col * HEAD_W + pair * pair_w
        return jnp.dot(xn, w_ref[:, lo:lo + pair_w], preferred_element_type=F32)

    @pl.when(j == 0)
    def _():
        hist_s[:, SUBLANE - (CONV_W - 1):SUBLANE, :] = conv0_ref[...]

    def rope(col, h, z, r):
        scale = 1.0 if col == 0 else HEAD_DIM ** -0.5
        out = (z * cos_ref[...] + pltpu.roll(z, HEAD_DIM // 2, 1) * sin_ref[...]) * scale
        lo = col * HEAD_W + h * HEAD_DIM
        qk_ref[r, :, lo:lo + HEAD_DIM] = out.astype(BF16)

    def conv(col, h, z, r):
        n = col - 4
        cs = slice(n * HEAD_W + h * HEAD_DIM, n * HEAD_W + (h + 1) * HEAD_DIM)
        xp = jnp.concatenate([hist_s[r, :, cs], z], axis=0)
        acc = z * convw_ref[CONV_W - 1:CONV_W, cs]
        for i in range(CONV_W - 1):
            acc = acc + pltpu.roll(xp, CONV_W - 1 - i, 0)[SUBLANE:] * convw_ref[i:i + 1, cs]
        hist_s[r, :, cs] = z[tm - SUBLANE:, :]
        c = _silu(acc)
        if n == 0:
            c = c * lax.rsqrt(jnp.sum(c * c, axis=-1, keepdims=True) + EPS) * (HEAD_DIM ** -0.5)
        elif n == 1:
            c = c * lax.rsqrt(jnp.sum(c * c, axis=-1, keepdims=True) + EPS)
        lo = (col - 2) * HEAD_W + h * HEAD_DIM
        rest_ref[r, :, lo:lo + HEAD_DIM] = c

    def plain(col, h, z, r):
        rest_ref[r, :, h * HEAD_DIM:(h + 1) * HEAD_DIM] = z

    def gate(col, h, z, r):
        lo = (HEAD_W if col == 3 else 5 * HEAD_W) + h * HEAD_DIM
        rest_ref[r, :, lo:lo + HEAD_DIM] = _silu(z)

    finishers = {0: rope, 1: rope, 2: plain, 3: gate, 4: conv, 5: conv, 6: conv, 7: gate}

    def finish(col, pair, z):
        for hh in range(2):
            for r in range(nb):
                finishers[col](col, 2 * pair + hh,
                               z[r * tm:(r + 1) * tm, hh * HEAD_DIM:(hh + 1) * HEAD_DIM], r)

    order = ((4, 0), (0, 0), (4, 1), (0, 1), (5, 0), (1, 0), (5, 1), (1, 1),
             (6, 0), (3, 0), (6, 1), (3, 1), (7, 0), (7, 1), (2, 0), (2, 1))
    pending = None
    for col, pair in order:
        z = proj(col, pair)
        if pending is not None:
            finish(*pending)
        pending = (col, pair, z)
    finish(*pending)

    ba = jnp.dot(xn, wba_ref[...], preferred_element_type=F32)
    zt = ba.T[:SUBLANE]
    for r in range(nb):
        rest_ref[r, :, 6 * HEAD_W:] = ba[r * tm:(r + 1) * tm]
        zt_ref[r] = zt[:, r * tm:(r + 1) * tm]

    @pl.when(j == pl.num_programs(1) - 1)
    def _():
        convnew_ref[...] = hist_s[:, SUBLANE - (CONV_W - 1):SUBLANE, :]


def _in_proj(x, l, n1g, w, wba, cos_t, sin_t, conv_w, conv0, l0, nb, tm):
    bsz, t, _ = x.shape

    def layer(shape):
        return pl.BlockSpec((None,) + shape, lambda i, j: (l,) + (0,) * len(shape))

    return pl.pallas_call(
        _in_proj_kernel,
        grid=(bsz // nb, t // tm),
        in_specs=[
            pl.BlockSpec((nb, tm, D_MODEL), lambda i, j: (i, j, 0)),
            layer((1, D_MODEL)),
            layer((D_MODEL, Z_MAIN)),
            layer((D_MODEL, LANE)),
            pl.BlockSpec((tm, LANE), lambda i, j: (j, 0)),
            pl.BlockSpec((tm, LANE), lambda i, j: (j, 0)),
            layer((CONV_W, CONV_CH)),
            pl.BlockSpec((None, nb, CONV_W - 1, CONV_CH), lambda i, j: (l0, i, 0, 0)),
        ],
        out_specs=[
            pl.BlockSpec((nb, tm, 2 * HEAD_W), lambda i, j: (i, j, 0)),
            pl.BlockSpec((nb, tm, REST_W), lambda i, j: (i, j, 0)),
            pl.BlockSpec((nb, SUBLANE, tm), lambda i, j: (i, 0, j)),
            pl.BlockSpec((nb, CONV_W - 1, CONV_CH), lambda i, j: (i, 0, 0)),
        ],
        out_shape=[
            jax.ShapeDtypeStruct((bsz, t, 2 * HEAD_W), BF16),
            jax.ShapeDtypeStruct((bsz, t, REST_W), F32),
            jax.ShapeDtypeStruct((bsz, SUBLANE, t), F32),
            jax.ShapeDtypeStruct((bsz, CONV_W - 1, CONV_CH), F32),
        ],
        scratch_shapes=[pltpu.VMEM((nb, SUBLANE, CONV_CH), F32)],
        compiler_params=pltpu.CompilerParams(
            dimension_semantics=("arbitrary", "arbitrary"), vmem_limit_bytes=VMEM_LIMIT),
        name="in_proj",
    )(x, n1g, w, wba, cos_t, sin_t, conv_w, conv0)


def _bmm(a, b):
    return jnp.einsum("gmk,gkn->gmn", a.astype(BF16), b.astype(BF16), preferred_element_type=F32)


def _bmm_nt(a, b):
    return jnp.einsum("gmk,gnk->gmn", a.astype(BF16), b.astype(BF16), preferred_element_type=F32)


def _bmm_tn(a, b):
    return jnp.einsum("gkm,gkn->gmn", a.astype(BF16), b.astype(BF16), preferred_element_type=F32)


def _unit_lower_inverse(a):
    n = a.shape[-1]
    ri = lax.broadcasted_iota(jnp.int32, (n, n), 0)
    ci = lax.broadcasted_iota(jnp.int32, (n, n), 1)
    eye = (ri == ci).astype(F32)[None]
    inner = ((ri // 2) == (ci // 2))[None]
    p = eye - jnp.where(inner, a, 0.0)
    size = 2
    while size < n:
        size *= 2
        outer = ((ri // size) == (ci // size))[None]
        e = jnp.where(outer & jnp.logical_not(inner), a, 0.0)
        p = p - _bmm(p, _bmm(e, p))
        inner = outer
    return p


def _mixer_kernel(qr_ref, kr_ref, va_ref, ga_ref, qb_ref, kb_ref, vb_ref, gb_ref, ba_ref, zt_ref,
                  intra_ref, rc_ref, hpc_ref, hpr_ref, retg_ref, dng_ref, sret0_ref, sdn0_ref,
                  mix_ref, sret_out, sdn_out, sret_s, sdn_s, *, bb, cb):
    step = pl.program_id(1)
    ng = bb * HEADS
    state_shape = (ng, HEAD_DIM, HEAD_DIM)

    @pl.when(step == 0)
    def _():
        sret_s[...] = sret0_ref[...].reshape(state_shape)
        sdn_s[...] = sdn0_ref[...].reshape(state_shape)

    ri = lax.broadcasted_iota(jnp.int32, (CHUNK, CHUNK), 0)
    ci = lax.broadcasted_iota(jnp.int32, (CHUNK, CHUNK), 1)
    causal = ri >= ci
    strict = ri > ci
    ltri = causal.astype(BF16)
    utri = (ri <= ci).astype(BF16)
    dot = functools.partial(jnp.dot, preferred_element_type=F32)

    hpc = hpc_ref[...]
    hpr = hpr_ref[...]
    neg_a_col = -jnp.exp(hpc[0:1, :])
    neg_a_row = -jnp.exp(hpr[:, 0:1])
    dn_g = dng_ref[...]
    ret_g = retg_ref[...]
    heads = [slice(h * HEAD_DIM, (h + 1) * HEAD_DIM) for h in range(HEADS)]
    reps = cb * bb
    intra = jnp.concatenate([intra_ref[...]] * reps, axis=0)
    inject = jnp.concatenate([rc_ref[1]] * reps, axis=0)
    cross = jnp.concatenate([rc_ref[0]] * bb, axis=0)
    carry = jnp.concatenate([rc_ref[2, :, 0:1, :]] * bb, axis=0)
    ret_gain = jnp.stack([ret_g[:, hs] for hs in heads] * reps)

    def lanes(col, j):
        return jnp.broadcast_to(col[:, j:j + 1], (CHUNK, HEAD_DIM))

    def heads_of(ref):
        return jnp.stack([ref[b, pl.ds(c * CHUNK, CHUNK), hs]
                          for c in range(cb) for b in range(bb) for hs in heads])

    beta, gc_c, gc_r = [], [], []
    for c in range(cb):
        r0 = c * CHUNK
        for b in range(bb):
            ba = ba_ref[b, pl.ds(r0, CHUNK), :]
            zt = zt_ref[b, :, r0:r0 + CHUNK]
            beta_col = _sigmoid(ba)
            g_col = neg_a_col * _softplus(ba + hpc[1:2, :])
            g_row = neg_a_row * _softplus(zt + hpr[:, 1:2])
            gh, gm, gl = _split3(g_col)
            gc_col = dot(ltri, gh) + (dot(ltri, gm) + dot(ltri, gl))
            gh, gm, gl = _split3(g_row)
            gc_row = dot(gh, utri) + (dot(gm, utri) + dot(gl, utri))
            for h in range(HEADS):
                beta.append(lanes(beta_col, h))
                gc_c.append(lanes(gc_col, HEADS + h))
                gc_r.append(gc_row[HEADS + h:HEADS + h + 1, :])

    q_r, k_r, v_r = heads_of(qr_ref), heads_of(kr_ref), heads_of(va_ref)
    o_intra = _bmm(_bmm_nt(q_r, k_r) * intra, v_r)
    s_inc = _bmm_tn(k_r, v_r * inject)

    q, k, v = heads_of(qb_ref), heads_of(kb_ref), heads_of(vb_ref)
    beta, gc_c = jnp.stack(beta), jnp.stack(gc_c)
    gc_last = gc_c[:, CHUNK - 1:CHUNK, :]
    e_gc = jnp.exp(gc_c)
    decay = jnp.exp(jnp.where(causal[None], gc_c[:, :, :CHUNK] - jnp.stack(gc_r), -jnp.inf))
    kbeta = k * beta
    kk = _bmm_nt(jnp.concatenate([kbeta, q], axis=1), k)
    a = jnp.where(strict[None], kk[:, :CHUNK] * decay, 0.0)
    attn = kk[:, CHUNK:] * decay
    tinv = _unit_lower_inverse(a)
    sol = _bmm(tinv, jnp.concatenate([v * beta, kbeta * e_gc], axis=2))
    u = sol[:, :, :HEAD_DIM]
    wq = jnp.concatenate([sol[:, :, HEAD_DIM:], q * e_gc], axis=1)
    kg = k * jnp.exp(gc_last - gc_c)
    g_end = jnp.exp(gc_last)

    s_ret = sret_s[...]
    s_dn = sdn_s[...]
    o_a, o_b = [], []
    for c in range(cb):
        sl = slice(c * ng, (c + 1) * ng)
        o_a.append(o_intra[sl] + _bmm(q_r[sl], s_ret) * cross)
        s_ret = s_ret * carry + s_inc[sl]
        ws = _bmm(wq[sl], s_dn)
        v_new = u[sl] - ws[:, :CHUNK]
        o_b.append(ws[:, CHUNK:] + _bmm(attn[sl], v_new))
        s_dn = s_dn * g_end[sl] + _bmm_tn(kg[sl], v_new)
    sret_s[...] = s_ret
    sdn_s[...] = s_dn

    o = jnp.concatenate(o_a, axis=0)
    mu = jnp.mean(o, axis=-1, keepdims=True)
    oc = o - mu
    var = jnp.mean(oc * oc, axis=-1, keepdims=True)
    o_a = oc * lax.rsqrt(var + EPS) * ret_gain * heads_of(ga_ref)
    o_b = _rms(jnp.concatenate(o_b, axis=0), dn_g[None]) * heads_of(gb_ref)
    for c in range(cb):
        rows = pl.ds(c * CHUNK, CHUNK)
        for b in range(bb):
            for h, hs in enumerate(heads):
                g = (c * bb + b) * HEADS + h
                mix_ref[b, rows, hs] = o_a[g]
                mix_ref[b, rows, HEAD_W + h * HEAD_DIM:HEAD_W + (h + 1) * HEAD_DIM] = o_b[g]

    @pl.when(step == pl.num_programs(1) - 1)
    def _():
        sret_out[...] = sret_s[...].reshape(sret_out.shape)
        sdn_out[...] = sdn_s[...].reshape(sdn_out.shape)


def _mixer(qk, rest, zt, l, intra, rc, hpc, hpr, ret_g, dn_g, sret0, sdn0, l0, bb, cb):
    bsz, t, _ = qk.shape
    blk = cb * CHUNK
    grid = (bsz // bb, t // blk)

    def cols(col):
        return pl.BlockSpec((bb, blk, HEAD_W), lambda i, j: (i, j, col))

    def const(shape):
        return pl.BlockSpec(shape, lambda i, j: (0,) * len(shape))

    def layer(shape):
        return pl.BlockSpec((None,) + shape, lambda i, j: (l,) + (0,) * len(shape))

    state_shape = (bb, HEADS, HEAD_DIM, HEAD_DIM)
    state_in = pl.BlockSpec((None,) + state_shape, lambda i, j: (l0, i, 0, 0, 0))
    state_out = pl.BlockSpec(state_shape, lambda i, j: (i, 0, 0, 0))
    kern = functools.partial(_mixer_kernel, bb=bb, cb=cb)
    return pl.pallas_call(
        kern,
        grid=grid,
        in_specs=[cols(0), cols(1)] + [cols(c) for c in range(6)] + [
            pl.BlockSpec((bb, blk, LANE), lambda i, j: (i, j, 6 * HEAD_W // LANE)),
            pl.BlockSpec((bb, SUBLANE, blk), lambda i, j: (i, 0, j)),
            const((HEADS, CHUNK, CHUNK)),
            const((3, HEADS, CHUNK, LANE)),
            layer((SUBLANE, LANE)),
            layer((SUBLANE, LANE)),
            layer((1, HEAD_W)),
            layer((1, HEAD_DIM)),
            state_in, state_in,
        ],
        out_specs=[
            pl.BlockSpec((bb, blk, 2 * HEAD_W), lambda i, j: (i, j, 0)),
            state_out, state_out,
        ],
        out_shape=[
            jax.ShapeDtypeStruct((bsz, t, 2 * HEAD_W), F32),
            jax.ShapeDtypeStruct((bsz, HEADS, HEAD_DIM, HEAD_DIM), F32),
            jax.ShapeDtypeStruct((bsz, HEADS, HEAD_DIM, HEAD_DIM), F32),
        ],
        scratch_shapes=[
            pltpu.VMEM((bb * HEADS, HEAD_DIM, HEAD_DIM), F32),
            pltpu.VMEM((bb * HEADS, HEAD_DIM, HEAD_DIM), F32),
        ],
        compiler_params=pltpu.CompilerParams(
            dimension_semantics=("arbitrary", "arbitrary"), vmem_limit_bytes=VMEM_LIMIT),
        name="mixer",
    )(qk, qk, *([rest] * 7), zt, intra, rc, hpc, hpr, ret_g, dn_g, sret0, sdn0)


def _post_kernel(h_ref, mix_ref, p_ref, wout_ref, n2g_ref, wup_ref, wdown_ref, proj_ref,
                 pleg_ref, gatew_ref, gateb_ref, fing_ref, out_ref, acc_s, xn_s, *, final):
    j = pl.program_id(1)

    @pl.when(j == 0)
    def _():
        h1 = h_ref[...] + jnp.dot(mix_ref[...].astype(BF16), wout_ref[...],
                                  preferred_element_type=F32)
        acc_s[...] = h1
        xn_s[...] = _rms(h1, n2g_ref[...]).astype(BF16)

    up = jnp.maximum(jnp.dot(xn_s[...], wup_ref[...], preferred_element_type=F32), 0.0)
    acc_s[...] += jnp.dot((up * up).astype(BF16), wdown_ref[...], preferred_element_type=F32)

    @pl.when(j == pl.num_programs(1) - 1)
    def _():
        h2 = acc_s[...]
        gate = _sigmoid(jnp.dot(_rms(h2, pleg_ref[...]).astype(BF16), gatew_ref[...],
                                preferred_element_type=F32) + gateb_ref[...])
        h3 = h2 + jnp.dot(p_ref[...].astype(BF16), proj_ref[...],
                          preferred_element_type=F32) * gate
        if final:
            h3 = _rms(h3, fing_ref[...])
        out_ref[...] = h3


def _post(h, mix, p, l, w_out, n2g, w_up, w_down, proj, pleg, gate_w, gate_b, fin_g, final, tm, tf):
    m = h.shape[0]

    def row(width):
        return pl.BlockSpec((tm, width), lambda i, j: (i, 0))

    def layer(shape):
        return pl.BlockSpec((None,) + shape, lambda i, j: (l,) + (0,) * len(shape),
                            pipeline_mode=pl.Buffered(1))

    return pl.pallas_call(
        functools.partial(_post_kernel, final=final),
        grid=(m // tm, D_FF // tf),
        in_specs=[
            row(D_MODEL), row(2 * HEAD_W),
            pl.BlockSpec((None, tm, PLE_DIM), lambda i, j: (l, i, 0)),
            layer((2 * HEAD_W, D_MODEL)),
            layer((1, D_MODEL)),
            pl.BlockSpec((None, D_MODEL, tf), lambda i, j: (l, 0, j)),
            pl.BlockSpec((None, tf, D_MODEL), lambda i, j: (l, j, 0)),
            layer((PLE_DIM, D_MODEL)),
            layer((1, D_MODEL)),
            layer((D_MODEL, D_MODEL)),
            layer((1, D_MODEL)),
            pl.BlockSpec((1, D_MODEL), lambda i, j: (0, 0)),
        ],
        out_specs=row(D_MODEL),
        out_shape=jax.ShapeDtypeStruct((m, D_MODEL), F32),
        scratch_shapes=[pltpu.VMEM((tm, D_MODEL), F32), pltpu.VMEM((tm, D_MODEL), BF16)],
        compiler_params=pltpu.CompilerParams(
            dimension_semantics=("arbitrary", "arbitrary"), vmem_limit_bytes=VMEM_LIMIT),
        name="post",
    )(h, mix, p, w_out, n2g, w_up, w_down, proj, pleg, gate_w, gate_b, fin_g)


def _rope_tables(pos0, t):
    half = HEAD_DIM // 2
    step = min(t, LANE)
    inv = ROPE_BASE ** (-jnp.arange(half, dtype=F32) / half)
    ang_hi = (pos0 + step * jnp.arange(t // step)).astype(F32)[:, None] * inv[None, :]
    ang_lo = jnp.arange(step).astype(F32)[:, None] * inv[None, :]
    ch, sh = jnp.cos(ang_hi)[:, None, :], jnp.sin(ang_hi)[:, None, :]
    cl, sl = jnp.cos(ang_lo)[None, :, :], jnp.sin(ang_lo)[None, :, :]
    cos = (ch * cl - sh * sl).reshape(t, half)
    sin = (sh * cl + ch * sl).reshape(t, half)
    return jnp.concatenate([cos, cos], axis=-1), jnp.concatenate([-sin, sin], axis=-1)


def _retention_tables():
    log_gamma = jnp.log(1.0 - 2.0 ** (-5.0 - jnp.arange(HEADS, dtype=F32)))
    idx = jnp.arange(CHUNK, dtype=F32)
    intra = jnp.exp(jnp.abs(idx[:, None] - idx[None, :])[None] * log_gamma[:, None, None])
    cross = jnp.exp((idx + 1.0)[None, :] * log_gamma[:, None])
    inject = jnp.exp((CHUNK - 1.0 - idx)[None, :] * log_gamma[:, None])
    carry = jnp.exp(CHUNK * log_gamma)
    carry = jnp.broadcast_to(carry[:, None], (HEADS, CHUNK))
    rc = jnp.broadcast_to(jnp.stack([cross, inject, carry])[..., None], (3, HEADS, CHUNK, LANE))
    return intra, rc


def _trunk(x, p, conv0, sret0, sdn0, zero_state_layer, pos0, wts, intra, rc, bb, cb, nb_in, tm_in, tm,
           tf):
    bsz, t, _ = x.shape
    m = bsz * t
    depth = wts["w_in"].shape[0]
    cos_t, sin_t = _rope_tables(pos0, t)
    p = p.reshape(depth, m, PLE_DIM)
    h = x
    rets, dns, convs = [], [], []
    for l in range(depth):
        l0 = 0 if zero_state_layer else l
        qk, rest, zt, conv = _in_proj(h.reshape(bsz, t, D_MODEL), l, wts["n1g"], wts["w_in"],
                                      wts["w_ba"], cos_t, sin_t, wts["conv_w"],
                                      conv0, l0, nb_in, tm_in)
        mix, sret, sdn = _mixer(qk, rest, zt, l, intra, rc, wts["hpc"], wts["hpr"], wts["ret_g"],
                                wts["dn_g"], sret0, sdn0, l0, bb, cb)
        h = _post(h.reshape(m, D_MODEL), mix.reshape(m, 2 * HEAD_W), p, l, wts["w_out"], wts["n2g"],
                  wts["w_up"], wts["w_down"], wts["proj"], wts["pleg"], wts["gate_w"], wts["gate_b"],
                  wts["fin_g"], l == depth - 1, tm, tf)
        rets.append(sret)
        dns.append(sdn)
        convs.append(conv)
    return h.reshape(bsz, t, D_MODEL), jnp.stack(rets), jnp.stack(dns), jnp.stack(convs)


def kernel(x_prompt, x_sample, p_prompt, p_sample, state_ret, state_delta, state_conv, norm1_g, w_in, conv_w, dt_bias, a_log, ret_norm_g, dn_norm_g, w_out, norm2_g, w_up, w_down, ple_proj, ple_norm_g, ple_gate_w, ple_gate_b, final_norm_g):
    depth = w_in.shape[0]
    w_in_b = _cast_leading_cols(w_in, Z_MAIN, rows_per_step=256)
    w_gate_cols = w_in[:, :, Z_MAIN:].astype(BF16)
    hpc = jnp.zeros((depth, SUBLANE, LANE), F32)
    hpc = hpc.at[:, 0, HEADS:N_GATE].set(a_log).at[:, 1, HEADS:N_GATE].set(dt_bias)
    hpr = jnp.zeros((depth, SUBLANE, LANE), F32)
    hpr = hpr.at[:, HEADS:N_GATE, 0].set(a_log).at[:, HEADS:N_GATE, 1].set(dt_bias)
    wts = dict(
        n1g=norm1_g[:, None, :], w_in=w_in_b,
        w_ba=jnp.pad(w_gate_cols, ((0, 0), (0, 0), (0, LANE - N_GATE))),
        conv_w=conv_w, hpc=hpc, hpr=hpr,
        ret_g=ret_norm_g[:, None, :], dn_g=dn_norm_g[:, None, :],
        w_out=w_out.astype(BF16), n2g=norm2_g[:, None, :],
        w_up=w_up.astype(BF16), w_down=w_down.astype(BF16),
        proj=ple_proj.astype(BF16), pleg=ple_norm_g[:, None, :],
        gate_w=ple_gate_w.astype(BF16), gate_b=ple_gate_b[:, None, :],
        fin_g=final_norm_g[None, :])
    intra, rc = _retention_tables()
    bp = x_prompt.shape[0]
    zero_state = jnp.zeros((1, bp, HEADS, HEAD_DIM, HEAD_DIM), F32)
    zero_conv = jnp.zeros((1, bp, CONV_W - 1, CONV_CH), F32)
    y_p, ret_p, dn_p, conv_p = _trunk(x_prompt, p_prompt, zero_conv, zero_state, zero_state, True, 0,
                                      wts, intra, rc, bb=2, cb=4, nb_in=1, tm_in=512, tm=1024, tf=1024)
    y_s, ret_s, dn_s, conv_s = _trunk(x_sample, p_sample, state_conv, state_ret, state_delta, False,
                                      PAST_LEN, wts, intra, rc, bb=8, cb=1, nb_in=8, tm_in=CHUNK,
                                      tm=512, tf=1024)
    return (y_p, y_s, ret_p, dn_p, conv_p, ret_s, dn_s, conv_s)
```

```python
import functools

import jax
import jax.numpy as jnp
from jax import lax
from jax.experimental import pallas as pl
from jax.experimental.pallas import tpu as pltpu

F32 = jnp.float32
BF16 = jnp.bfloat16

D_MODEL = 1024
CHUNK = 64
HEADS = 4
HEAD_DIM = 128
HEAD_W = HEADS * HEAD_DIM
PLE_DIM = 256
D_FF = 4 * D_MODEL
CONV_W = 4
CONV_CH = 3 * HEAD_W
ROPE_BASE = 10000.0
EPS = 1e-6
PAST_LEN = 4096
Z_MAIN = 8 * HEAD_W
N_GATE = 2 * HEADS
LANE = 128
SUBLANE = 8
REST_W = 6 * HEAD_W + LANE
VMEM_LIMIT = 56 * 1024 * 1024


def _split3(x):
    hi = x.astype(BF16)
    r = x - hi.astype(F32)
    mid = r.astype(BF16)
    lo = (r - mid.astype(F32)).astype(BF16)
    return hi, mid, lo


def _rms(x, g):
    return x * lax.rsqrt(jnp.mean(x * x, axis=-1, keepdims=True) + EPS) * g


def _softplus(x):
    return jnp.maximum(x, 0.0) + jnp.log1p(jnp.exp(-jnp.abs(x)))


def _sigmoid(x):
    return 0.5 + 0.5 * jnp.tanh(0.5 * x)


def _silu(x):
    h = 0.5 * x
    return h + h * jnp.tanh(h)


def _cast_kernel(x_ref, o_ref):
    o_ref[...] = x_ref[...].astype(o_ref.dtype)


def _cast_leading_cols(w, ncols, rows_per_step):
    depth, rows, cols = w.shape
    spec = pl.BlockSpec((rows_per_step, ncols), lambda i: (i, 0))
    out = pl.pallas_call(
        _cast_kernel,
        grid=(depth * rows // rows_per_step,),
        in_specs=[spec],
        out_specs=spec,
        out_shape=jax.ShapeDtypeStruct((depth * rows, ncols), BF16),
        compiler_params=pltpu.CompilerParams(
            dimension_semantics=("arbitrary",), vmem_limit_bytes=VMEM_LIMIT),
        name="cast_w_in",
    )(w.reshape(depth * rows, cols))
    return out.reshape(depth, rows, ncols)


def _in_proj_kernel(x_ref, g_ref, w_ref, wba_ref, cos_ref, sin_ref, convw_ref, conv0_ref,
                    qk_ref, rest_ref, zt_ref, convnew_ref, hist_s):
    j = pl.program_id(1)
    nb, tm, _ = x_ref.shape
    xn = _rms(x_ref[...].reshape(nb * tm, D_MODEL), g_ref[...]).astype(BF16)
    pair_w = 2 * HEAD_DIM

    def proj(col, pair):
        lo = col * HEAD_W + pair * pair_w
        return jnp.dot(xn, w_ref[:, lo:lo + pair_w], preferred_element_type=F32)

    @pl.when(j == 0)
    def _():
        hist_s[:, SUBLANE - (CONV_W - 1):SUBLANE, :] = conv0_ref[...]

    def rope(col, h, z, r):
        scale = 1.0 if col == 0 else HEAD_DIM ** -0.5
        out = (z * cos_ref[...] + pltpu.roll(z, HEAD_DIM // 2, 1) * sin_ref[...]) * scale
        lo = col * HEAD_W + h * HEAD_DIM
        qk_ref[r, :, lo:lo + HEAD_DIM] = out.astype(BF16)

    def conv(col, h, z, r):
        n = col - 4
        cs = slice(n * HEAD_W + h * HEAD_DIM, n * HEAD_W + (h + 1) * HEAD_DIM)
        xp = jnp.concatenate([hist_s[r, :, cs], z], axis=0)
        acc = z * convw_ref[CONV_W - 1:CONV_W, cs]
        for i in range(CONV_W - 1):
            acc = acc + pltpu.roll(xp, CONV_W - 1 - i, 0)[SUBLANE:] * convw_ref[i:i + 1, cs]
        hist_s[r, :, cs] = z[tm - SUBLANE:, :]
        c = _silu(acc)
        if n == 0:
            c = c * lax.rsqrt(jnp.sum(c * c, axis=-1, keepdims=True) + EPS) * (HEAD_DIM ** -0.5)
        elif n == 1:
            c = c * lax.rsqrt(jnp.sum(c * c, axis=-1, keepdims=True) + EPS)
        lo = (col - 2) * HEAD_W + h * HEAD_DIM
        rest_ref[r, :, lo:lo + HEAD_DIM] = c

    def plain(col, h, z, r):
        rest_ref[r, :, h * HEAD_DIM:(h + 1) * HEAD_DIM] = z

    def gate(col, h, z, r):
        lo = (HEAD_W if col == 3 else 5 * HEAD_W) + h * HEAD_DIM
        rest_ref[r, :, lo:lo + HEAD_DIM] = _silu(z)

    finishers = {0: rope, 1: rope, 2: plain, 3: gate, 4: conv, 5: conv, 6: conv, 7: gate}

    def finish(col, pair, z):
        for hh in range(2):
            for r in range(nb):
                finishers[col](col, 2 * pair + hh,
                               z[r * tm:(r + 1) * tm, hh * HEAD_DIM:(hh + 1) * HEAD_DIM], r)

    order = ((4, 0), (0, 0), (4, 1), (0, 1), (5, 0), (1, 0), (5, 1), (1, 1),
             (6, 0), (3, 0), (6, 1), (3, 1), (7, 0), (7, 1), (2, 0), (2, 1))
    pending = None
    for col, pair in order:
        z = proj(col, pair)
        if pending is not None:
            finish(*pending)
        pending = (col, pair, z)
    finish(*pending)

    ba = jnp.dot(xn, wba_ref[...], preferred_element_type=F32)
    zt = ba.T[:SUBLANE]
    for r in range(nb):
        rest_ref[r, :, 6 * HEAD_W:] = ba[r * tm:(r + 1) * tm]
        zt_ref[r] = zt[:, r * tm:(r + 1) * tm]

    @pl.when(j == pl.num_programs(1) - 1)
    def _():
        convnew_ref[...] = hist_s[:, SUBLANE - (CONV_W - 1):SUBLANE, :]


def _in_proj(x, l, n1g, w, wba, cos_t, sin_t, conv_w, conv0, l0, nb, tm):
    bsz, t, _ = x.shape

    def layer(shape):
        return pl.BlockSpec((None,) + shape, lambda i, j: (l,) + (0,) * len(shape))

    return pl.pallas_call(
        _in_proj_kernel,
        grid=(bsz // nb, t // tm),
        in_specs=[
            pl.BlockSpec((nb, tm, D_MODEL), lambda i, j: (i, j, 0)),
            layer((1, D_MODEL)),
            layer((D_MODEL, Z_MAIN)),
            layer((D_MODEL, LANE)),
            pl.BlockSpec((tm, LANE), lambda i, j: (j, 0)),
            pl.BlockSpec((tm, LANE), lambda i, j: (j, 0)),
            layer((CONV_W, CONV_CH)),
            pl.BlockSpec((None, nb, CONV_W - 1, CONV_CH), lambda i, j: (l0, i, 0, 0)),
        ],
        out_specs=[
            pl.BlockSpec((nb, tm, 2 * HEAD_W), lambda i, j: (i, j, 0)),
            pl.BlockSpec((nb, tm, REST_W), lambda i, j: (i, j, 0)),
            pl.BlockSpec((nb, SUBLANE, tm), lambda i, j: (i, 0, j)),
            pl.BlockSpec((nb, CONV_W - 1, CONV_CH), lambda i, j: (i, 0, 0)),
        ],
        out_shape=[
            jax.ShapeDtypeStruct((bsz, t, 2 * HEAD_W), BF16),
            jax.ShapeDtypeStruct((bsz, t, REST_W), F32),
            jax.ShapeDtypeStruct((bsz, SUBLANE, t), F32),
            jax.ShapeDtypeStruct((bsz, CONV_W - 1, CONV_CH), F32),
        ],
        scratch_shapes=[pltpu.VMEM((nb, SUBLANE, CONV_CH), F32)],
        compiler_params=pltpu.CompilerParams(
            dimension_semantics=("arbitrary", "arbitrary"), vmem_limit_bytes=VMEM_LIMIT),
        name="in_proj",
    )(x, n1g, w, wba, cos_t, sin_t, conv_w, conv0)


def _bmm(a, b):
    return jnp.einsum("gmk,gkn->gmn", a.astype(BF16), b.astype(BF16), preferred_element_type=F32)


def _bmm_nt(a, b):
    return jnp.einsum("gmk,gnk->gmn", a.astype(BF16), b.astype(BF16), preferred_element_type=F32)


def _bmm_tn(a, b):
    return jnp.einsum("gkm,gkn->gmn", a.astype(BF16), b.astype(BF16), preferred_element_type=F32)


def _unit_lower_inverse(a):
    n = a.shape[-1]
    ri = lax.broadcasted_iota(jnp.int32, (n, n), 0)
    ci = lax.broadcasted_iota(jnp.int32, (n, n), 1)
    eye = (ri == ci).astype(F32)[None]
    inner = ((ri // 2) == (ci // 2))[None]
    p = eye - jnp.where(inner, a, 0.0)
    size = 2
    while size < n:
        size *= 2
        outer = ((ri // size) == (ci // size))[None]
        e = jnp.where(outer & jnp.logical_not(inner), a, 0.0)
        p = p - _bmm(p, _bmm(e, p))
        inner = outer
    return p


def _mixer_kernel(qr_ref, kr_ref, va_ref, ga_ref, qb_ref, kb_ref, vb_ref, gb_ref, ba_ref, zt_ref,
                  intra_ref, rc_ref, hpc_ref, hpr_ref, retg_ref, dng_ref, sret0_ref, sdn0_ref,
                  mix_ref, sret_out, sdn_out, sret_s, sdn_s, *, bb, cb):
    step = pl.program_id(1)
    ng = bb * HEADS
    state_shape = (ng, HEAD_DIM, HEAD_DIM)

    @pl.when(step == 0)
    def _():
        sret_s[...] = sret0_ref[...].reshape(state_shape)
        sdn_s[...] = sdn0_ref[...].reshape(state_shape)

    ri = lax.broadcasted_iota(jnp.int32, (CHUNK, CHUNK), 0)
    ci = lax.broadcasted_iota(jnp.int32, (CHUNK, CHUNK), 1)
    causal = ri >= ci
    strict = ri > ci
    ltri = causal.astype(BF16)
    utri = (ri <= ci).astype(BF16)
    dot = functools.partial(jnp.dot, preferred_element_type=F32)

    hpc = hpc_ref[...]
    hpr = hpr_ref[...]
    neg_a_col = -jnp.exp(hpc[0:1, :])
    neg_a_row = -jnp.exp(hpr[:, 0:1])
    dn_g = dng_ref[...]
    ret_g = retg_ref[...]
    heads = [slice(h * HEAD_DIM, (h + 1) * HEAD_DIM) for h in range(HEADS)]
    reps = cb * bb
    intra = jnp.concatenate([intra_ref[...]] * reps, axis=0)
    inject = jnp.concatenate([rc_ref[1]] * reps, axis=0)
    cross = jnp.concatenate([rc_ref[0]] * bb, axis=0)
    carry = jnp.concatenate([rc_ref[2, :, 0:1, :]] * bb, axis=0)
    ret_gain = jnp.stack([ret_g[:, hs] for hs in heads] * reps)

    def lanes(col, j):
        return jnp.broadcast_to(col[:, j:j + 1], (CHUNK, HEAD_DIM))

    def prepare(chunks):
        def heads_of(ref):
            return jnp.stack([ref[b, pl.ds(c * CHUNK, CHUNK), hs]
                              for c in chunks for b in range(bb) for hs in heads])

        beta, gc_c, gc_r = [], [], []
        for c in chunks:
            r0 = c * CHUNK
            for b in range(bb):
                ba = ba_ref[b, pl.ds(r0, CHUNK), :]
                zt = zt_ref[b, :, r0:r0 + CHUNK]
                beta_col = _sigmoid(ba)
                g_col = neg_a_col * _softplus(ba + hpc[1:2, :])
                g_row = neg_a_row * _softplus(zt + hpr[:, 1:2])
                gh, gm, gl = _split3(g_col)
                gc_col = dot(ltri, gh) + (dot(ltri, gm) + dot(ltri, gl))
                gh, gm, gl = _split3(g_row)
                gc_row = dot(gh, utri) + (dot(gm, utri) + dot(gl, utri))
                for h in range(HEADS):
                    beta.append(lanes(beta_col, h))
                    gc_c.append(lanes(gc_col, HEADS + h))
                    gc_r.append(gc_row[HEADS + h:HEADS + h + 1, :])

        q_r, k_r, v_r = heads_of(qr_ref), heads_of(kr_ref), heads_of(va_ref)
        o_intra = _bmm(_bmm_nt(q_r, k_r) * intra, v_r)
        s_inc = _bmm_tn(k_r, v_r * inject)

        q, k, v = heads_of(qb_ref), heads_of(kb_ref), heads_of(vb_ref)
        beta, gc_c = jnp.stack(beta), jnp.stack(gc_c)
        gc_last = gc_c[:, CHUNK - 1:CHUNK, :]
        e_gc = jnp.exp(gc_c)
        decay = jnp.exp(jnp.where(causal[None], gc_c[:, :, :CHUNK] - jnp.stack(gc_r), -jnp.inf))
        kbeta = k * beta
        kk = _bmm_nt(jnp.concatenate([kbeta, q], axis=1), k)
        a = jnp.where(strict[None], kk[:, :CHUNK] * decay, 0.0)
        tinv = _unit_lower_inverse(a)
        sol = _bmm(tinv, jnp.concatenate([v * beta, kbeta * e_gc], axis=2))
        return dict(
            q_r=q_r, o_intra=o_intra, s_inc=s_inc, attn=kk[:, CHUNK:] * decay,
            u=sol[:, :, :HEAD_DIM],
            wq=jnp.concatenate([sol[:, :, HEAD_DIM:], q * e_gc], axis=1),
            kg=k * jnp.exp(gc_last - gc_c), g_end=jnp.exp(gc_last),
            gate_a=heads_of(ga_ref), gate_b=heads_of(gb_ref))

    groups = [list(range(cb))]
    prepared = [prepare(chunks) for chunks in groups]

    s_ret = sret_s[...]
    s_dn = sdn_s[...]
    for chunks, pre in zip(groups, prepared):
        o_a, o_b = [], []
        for n in range(len(chunks)):
            sl = slice(n * ng, (n + 1) * ng)
            o_a.append(pre["o_intra"][sl] + _bmm(pre["q_r"][sl], s_ret) * cross)
            s_ret = s_ret * carry + pre["s_inc"][sl]
            ws = _bmm(pre["wq"][sl], s_dn)
            v_new = pre["u"][sl] - ws[:, :CHUNK]
            o_b.append(ws[:, CHUNK:] + _bmm(pre["attn"][sl], v_new))
            s_dn = s_dn * pre["g_end"][sl] + _bmm_tn(pre["kg"][sl], v_new)
        o = jnp.concatenate(o_a, axis=0)
        mu = jnp.mean(o, axis=-1, keepdims=True)
        oc = o - mu
        var = jnp.mean(oc * oc, axis=-1, keepdims=True)
        o_a = oc * lax.rsqrt(var + EPS) * ret_gain * pre["gate_a"]
        o_b = _rms(jnp.concatenate(o_b, axis=0), dn_g[None]) * pre["gate_b"]
        for n, c in enumerate(chunks):
            rows = pl.ds(c * CHUNK, CHUNK)
            for b in range(bb):
                for h, hs in enumerate(heads):
                    g = (n * bb + b) * HEADS + h
                    mix_ref[b, rows, hs] = o_a[g]
                    mix_ref[b, rows, HEAD_W + h * HEAD_DIM:HEAD_W + (h + 1) * HEAD_DIM] = o_b[g]
    sret_s[...] = s_ret
    sdn_s[...] = s_dn

    @pl.when(step == pl.num_programs(1) - 1)
    def _():
        sret_out[...] = sret_s[...].reshape(sret_out.shape)
        sdn_out[...] = sdn_s[...].reshape(sdn_out.shape)


def _mixer(qk, rest, zt, l, intra, rc, hpc, hpr, ret_g, dn_g, sret0, sdn0, l0, bb, cb):
    bsz, t, _ = qk.shape
    blk = cb * CHUNK
    grid = (bsz // bb, t // blk)

    def cols(col):
        return pl.BlockSpec((bb, blk, HEAD_W), lambda i, j: (i, j, col))

    def const(shape):
        return pl.BlockSpec(shape, lambda i, j: (0,) * len(shape))

    def layer(shape):
        return pl.BlockSpec((None,) + shape, lambda i, j: (l,) + (0,) * len(shape))

    state_shape = (bb, HEADS, HEAD_DIM, HEAD_DIM)
    state_in = pl.BlockSpec((None,) + state_shape, lambda i, j: (l0, i, 0, 0, 0))
    state_out = pl.BlockSpec(state_shape, lambda i, j: (i, 0, 0, 0))
    kern = functools.partial(_mixer_kernel, bb=bb, cb=cb)
    return pl.pallas_call(
        kern,
        grid=grid,
        in_specs=[cols(0), cols(1)] + [cols(c) for c in range(6)] + [
            pl.BlockSpec((bb, blk, LANE), lambda i, j: (i, j, 6 * HEAD_W // LANE)),
            pl.BlockSpec((bb, SUBLANE, blk), lambda i, j: (i, 0, j)),
            const((HEADS, CHUNK, CHUNK)),
            const((3, HEADS, CHUNK, LANE)),
            layer((SUBLANE, LANE)),
            layer((SUBLANE, LANE)),
            layer((1, HEAD_W)),
            layer((1, HEAD_DIM)),
            state_in, state_in,
        ],
        out_specs=[
            pl.BlockSpec((bb, blk, 2 * HEAD_W), lambda i, j: (i, j, 0)),
            state_out, state_out,
        ],
        out_shape=[
            jax.ShapeDtypeStruct((bsz, t, 2 * HEAD_W), F32),
            jax.ShapeDtypeStruct((bsz, HEADS, HEAD_DIM, HEAD_DIM), F32),
            jax.ShapeDtypeStruct((bsz, HEADS, HEAD_DIM, HEAD_DIM), F32),
        ],
        scratch_shapes=[
            pltpu.VMEM((bb * HEADS, HEAD_DIM, HEAD_DIM), F32),
            pltpu.VMEM((bb * HEADS, HEAD_DIM, HEAD_DIM), F32),
        ],
        compiler_params=pltpu.CompilerParams(
            dimension_semantics=("arbitrary", "arbitrary"), vmem_limit_bytes=VMEM_LIMIT),
        name="mixer",
    )(qk, qk, *([rest] * 7), zt, intra, rc, hpc, hpr, ret_g, dn_g, sret0, sdn0)


def _post_kernel(h_ref, mix_ref, p_ref, wout_ref, n2g_ref, wup_ref, wdown_ref, proj_ref,
                 pleg_ref, gatew_ref, gateb_ref, fing_ref, out_ref, acc_s, xn_s, *, final):
    j = pl.program_id(1)

    @pl.when(j == 0)
    def _():
        h1 = h_ref[...] + jnp.dot(mix_ref[...].astype(BF16), wout_ref[...],
                                  preferred_element_type=F32)
        acc_s[...] = h1
        xn_s[...] = _rms(h1, n2g_ref[...]).astype(BF16)

    up = jnp.maximum(jnp.dot(xn_s[...], wup_ref[...], preferred_element_type=F32), 0.0)
    acc_s[...] += jnp.dot((up * up).astype(BF16), wdown_ref[...], preferred_element_type=F32)

    @pl.when(j == pl.num_programs(1) - 1)
    def _():
        h2 = acc_s[...]
        gate = _sigmoid(jnp.dot(_rms(h2, pleg_ref[...]).astype(BF16), gatew_ref[...],
                                preferred_element_type=F32) + gateb_ref[...])
        h3 = h2 + jnp.dot(p_ref[...].astype(BF16), proj_ref[...],
                          preferred_element_type=F32) * gate
        if final:
            h3 = _rms(h3, fing_ref[...])
        out_ref[...] = h3


def _post(h, mix, p, l, w_out, n2g, w_up, w_down, proj, pleg, gate_w, gate_b, fin_g, final, tm, tf):
    m = h.shape[0]

    def row(width):
        return pl.BlockSpec((tm, width), lambda i, j: (i, 0))

    def layer(shape):
        return pl.BlockSpec((None,) + shape, lambda i, j: (l,) + (0,) * len(shape),
                            pipeline_mode=pl.Buffered(1))

    return pl.pallas_call(
        functools.partial(_post_kernel, final=final),
        grid=(m // tm, D_FF // tf),
        in_specs=[
            row(D_MODEL), row(2 * HEAD_W),
            pl.BlockSpec((None, tm, PLE_DIM), lambda i, j: (l, i, 0)),
            layer((2 * HEAD_W, D_MODEL)),
            layer((1, D_MODEL)),
            pl.BlockSpec((None, D_MODEL, tf), lambda i, j: (l, 0, j)),
            pl.BlockSpec((None, tf, D_MODEL), lambda i, j: (l, j, 0)),
            layer((PLE_DIM, D_MODEL)),
            layer((1, D_MODEL)),
            layer((D_MODEL, D_MODEL)),
            layer((1, D_MODEL)),
            pl.BlockSpec((1, D_MODEL), lambda i, j: (0, 0)),
        ],
        out_specs=row(D_MODEL),
        out_shape=jax.ShapeDtypeStruct((m, D_MODEL), F32),
        scratch_shapes=[pltpu.VMEM((tm, D_MODEL), F32), pltpu.VMEM((tm, D_MODEL), BF16)],
        compiler_params=pltpu.CompilerParams(
            dimension_semantics=("arbitrary", "arbitrary"), vmem_limit_bytes=VMEM_LIMIT),
        name="post",
    )(h, mix, p, w_out, n2g, w_up, w_down, proj, pleg, gate_w, gate_b, fin_g)


def _rope_tables(pos0, t):
    half = HEAD_DIM // 2
    step = min(t, LANE)
    inv = ROPE_BASE ** (-jnp.arange(half, dtype=F32) / half)
    ang_hi = (pos0 + step * jnp.arange(t // step)).astype(F32)[:, None] * inv[None, :]
    ang_lo = jnp.arange(step).astype(F32)[:, None] * inv[None, :]
    ch, sh = jnp.cos(ang_hi)[:, None, :], jnp.sin(ang_hi)[:, None, :]
    cl, sl = jnp.cos(ang_lo)[None, :, :], jnp.sin(ang_lo)[None, :, :]
    cos = (ch * cl - sh * sl).reshape(t, half)
    sin = (sh * cl + ch * sl).reshape(t, half)
    return jnp.concatenate([cos, cos], axis=-1), jnp.concatenate([-sin, sin], axis=-1)


def _retention_tables():
    log_gamma = jnp.log(1.0 - 2.0 ** (-5.0 - jnp.arange(HEADS, dtype=F32)))
    idx = jnp.arange(CHUNK, dtype=F32)
    intra = jnp.exp(jnp.abs(idx[:, None] - idx[None, :])[None] * log_gamma[:, None, None])
    cross = jnp.exp((idx + 1.0)[None, :] * log_gamma[:, None])
    inject = jnp.exp((CHUNK - 1.0 - idx)[None, :] * log_gamma[:, None])
    carry = jnp.exp(CHUNK * log_gamma)
    carry = jnp.broadcast_to(carry[:, None], (HEADS, CHUNK))
    rc = jnp.broadcast_to(jnp.stack([cross, inject, carry])[..., None], (3, HEADS, CHUNK, LANE))
    return intra, rc


def _trunk(x, p, conv0, sret0, sdn0, zero_state_layer, pos0, wts, intra, rc, bb, cb, nb_in, tm_in,
           tm, tf):
    bsz, t, _ = x.shape
    m = bsz * t
    depth = wts["w_in"].shape[0]
    cos_t, sin_t = _rope_tables(pos0, t)
    p = p.reshape(depth, m, PLE_DIM)
    h = x
    rets, dns, convs = [], [], []
    for l in range(depth):
        l0 = 0 if zero_state_layer else l
        qk, rest, zt, conv = _in_proj(h.reshape(bsz, t, D_MODEL), l, wts["n1g"], wts["w_in"],
                                      wts["w_ba"], cos_t, sin_t, wts["conv_w"],
                                      conv0, l0, nb_in, tm_in)
        mix, sret, sdn = _mixer(qk, rest, zt, l, intra, rc, wts["hpc"], wts["hpr"], wts["ret_g"],
                                wts["dn_g"], sret0, sdn0, l0, bb, cb)
        h = _post(h.reshape(m, D_MODEL), mix.reshape(m, 2 * HEAD_W), p, l, wts["w_out"], wts["n2g"],
                  wts["w_up"], wts["w_down"], wts["proj"], wts["pleg"], wts["gate_w"], wts["gate_b"],
                  wts["fin_g"], l == depth - 1, tm, tf)
        rets.append(sret)
        dns.append(sdn)
        convs.append(conv)
    return h.reshape(bsz, t, D_MODEL), jnp.stack(rets), jnp.stack(dns), jnp.stack(convs)


def kernel(x_prompt, x_sample, p_prompt, p_sample, state_ret, state_delta, state_conv, norm1_g, w_in, conv_w, dt_bias, a_log, ret_norm_g, dn_norm_g, w_out, norm2_g, w_up, w_down, ple_proj, ple_norm_g, ple_gate_w, ple_gate_b, final_norm_g):
    depth = w_in.shape[0]
    w_in_b = _cast_leading_cols(w_in, Z_MAIN, rows_per_step=256)
    w_gate_cols = w_in[:, :, Z_MAIN:].astype(BF16)
    hpc = jnp.zeros((depth, SUBLANE, LANE), F32)
    hpc = hpc.at[:, 0, HEADS:N_GATE].set(a_log).at[:, 1, HEADS:N_GATE].set(dt_bias)
    hpr = jnp.zeros((depth, SUBLANE, LANE), F32)
    hpr = hpr.at[:, HEADS:N_GATE, 0].set(a_log).at[:, HEADS:N_GATE, 1].set(dt_bias)
    wts = dict(
        n1g=norm1_g[:, None, :], w_in=w_in_b,
        w_ba=jnp.pad(w_gate_cols, ((0, 0), (0, 0), (0, LANE - N_GATE))),
        conv_w=conv_w, hpc=hpc, hpr=hpr,
        ret_g=ret_norm_g[:, None, :], dn_g=dn_norm_g[:, None, :],
        w_out=w_out.astype(BF16), n2g=norm2_g[:, None, :],
        w_up=w_up.astype(BF16), w_down=w_down.astype(BF16),
        proj=ple_proj.astype(BF16), pleg=ple_norm_g[:, None, :],
        gate_w=ple_gate_w.astype(BF16), gate_b=ple_gate_b[:, None, :],
        fin_g=final_norm_g[None, :])
    intra, rc = _retention_tables()
    bp = x_prompt.shape[0]
    zero_state = jnp.zeros((1, bp, HEADS, HEAD_DIM, HEAD_DIM), F32)
    zero_conv = jnp.zeros((1, bp, CONV_W - 1, CONV_CH), F32)
    y_p, ret_p, dn_p, conv_p = _trunk(x_prompt, p_prompt, zero_conv, zero_state, zero_state, True, 0,
                                      wts, intra, rc, bb=2, cb=4, nb_in=1, tm_in=512, tm=1024,
                                      tf=1024)
    y_s, ret_s, dn_s, conv_s = _trunk(x_sample, p_sample, state_conv, state_ret, state_delta, False,
                                      PAST_LEN, wts, intra, rc, bb=8, cb=1, nb_in=8, tm_in=CHUNK,
                                      tm=1024, tf=1024)
    return (y_p, y_s, ret_p, dn_p, conv_p, ret_s, dn_s, conv_s)
```

```python
import functools

import jax
import jax.numpy as jnp
from jax import lax
from jax.experimental import pallas as pl
from jax.experimental.pallas import tpu as pltpu

F32 = jnp.float32
BF16 = jnp.bfloat16

D_MODEL = 1024
CHUNK = 64
HEADS = 4
HEAD_DIM = 128
HEAD_W = HEADS * HEAD_DIM
PLE_DIM = 256
D_FF = 4 * D_MODEL
CONV_W = 4
CONV_CH = 3 * HEAD_W
ROPE_BASE = 10000.0
EPS = 1e-6
PAST_LEN = 4096
Z_MAIN = 8 * HEAD_W
N_GATE = 2 * HEADS
LANE = 128
SUBLANE = 8
REST_W = 6 * HEAD_W + LANE
VMEM_LIMIT = 56 * 1024 * 1024


def _split3(x):
    hi = x.astype(BF16)
    r = x - hi.astype(F32)
    mid = r.astype(BF16)
    lo = (r - mid.astype(F32)).astype(BF16)
    return hi, mid, lo


def _rms(x, g):
    return x * lax.rsqrt(jnp.mean(x * x, axis=-1, keepdims=True) + EPS) * g


def _softplus(x):
    return jnp.maximum(x, 0.0) + jnp.log1p(jnp.exp(-jnp.abs(x)))


def _sigmoid(x):
    return 0.5 + 0.5 * jnp.tanh(0.5 * x)


def _silu(x):
    h = 0.5 * x
    return h + h * jnp.tanh(h)


def _cast_kernel(x_ref, o_ref):
    o_ref[...] = x_ref[...].astype(o_ref.dtype)


def _cast_leading_cols(w, ncols, rows_per_step):
    depth, rows, cols = w.shape
    spec = pl.BlockSpec((rows_per_step, ncols), lambda i: (i, 0))
    out = pl.pallas_call(
        _cast_kernel,
        grid=(depth * rows // rows_per_step,),
        in_specs=[spec],
        out_specs=spec,
        out_shape=jax.ShapeDtypeStruct((depth * rows, ncols), BF16),
        compiler_params=pltpu.CompilerParams(
            dimension_semantics=("arbitrary",), vmem_limit_bytes=VMEM_LIMIT),
        name="cast_w_in",
    )(w.reshape(depth * rows, cols))
    return out.reshape(depth, rows, ncols)


def _in_proj_kernel(x_ref, g_ref, w_ref, wba_ref, cos_ref, sin_ref, convw_ref, conv0_ref,
                    qk_ref, rest_ref, zt_ref, convnew_ref, hist_s):
    j = pl.program_id(1)
    nb, tm, _ = x_ref.shape
    xn = _rms(x_ref[...].reshape(nb * tm, D_MODEL), g_ref[...]).astype(BF16)
    pair_w = 2 * HEAD_DIM

    def proj(col, pair):
        lo = col * HEAD_W + pair * pair_w
        return jnp.dot(xn, w_ref[:, lo:lo + pair_w], preferred_element_type=F32)

    @pl.when(j == 0)
    def _():
        hist_s[:, SUBLANE - (CONV_W - 1):SUBLANE, :] = conv0_ref[...]

    def rope(col, h, z, r):
        scale = 1.0 if col == 0 else HEAD_DIM ** -0.5
        out = (z * cos_ref[...] + pltpu.roll(z, HEAD_DIM // 2, 1) * sin_ref[...]) * scale
        lo = col * HEAD_W + h * HEAD_DIM
        qk_ref[r, :, lo:lo + HEAD_DIM] = out.astype(BF16)

    def conv(col, h, z, r):
        n = col - 4
        cs = slice(n * HEAD_W + h * HEAD_DIM, n * HEAD_W + (h + 1) * HEAD_DIM)
        xp = jnp.concatenate([hist_s[r, :, cs], z], axis=0)
        acc = z * convw_ref[CONV_W - 1:CONV_W, cs]
        for i in range(CONV_W - 1):
            acc = acc + pltpu.roll(xp, CONV_W - 1 - i, 0)[SUBLANE:] * convw_ref[i:i + 1, cs]
        hist_s[r, :, cs] = z[tm - SUBLANE:, :]
        c = _silu(acc)
        if n == 0:
            c = c * lax.rsqrt(jnp.sum(c * c, axis=-1, keepdims=True) + EPS) * (HEAD_DIM ** -0.5)
        elif n == 1:
            c = c * lax.rsqrt(jnp.sum(c * c, axis=-1, keepdims=True) + EPS)
        lo = (col - 2) * HEAD_W + h * HEAD_DIM
        rest_ref[r, :, lo:lo + HEAD_DIM] = c

    def plain(col, h, z, r):
        rest_ref[r, :, h * HEAD_DIM:(h + 1) * HEAD_DIM] = z

    def gate(col, h, z, r):
        lo = (HEAD_W if col == 3 else 5 * HEAD_W) + h * HEAD_DIM
        rest_ref[r, :, lo:lo + HEAD_DIM] = _silu(z)

    finishers = {0: rope, 1: rope, 2: plain, 3: gate, 4: conv, 5: conv, 6: conv, 7: gate}

    def finish(col, pair, z):
        for hh in range(2):
            for r in range(nb):
                finishers[col](col, 2 * pair + hh,
                               z[r * tm:(r + 1) * tm, hh * HEAD_DIM:(hh + 1) * HEAD_DIM], r)

    order = ((4, 0), (0, 0), (4, 1), (0, 1), (5, 0), (1, 0), (5, 1), (1, 1),
             (6, 0), (3, 0), (6, 1), (3, 1), (7, 0), (7, 1), (2, 0), (2, 1))
    pending = None
    for col, pair in order:
        z = proj(col, pair)
        if pending is not None:
            finish(*pending)
        pending = (col, pair, z)
    finish(*pending)

    ba = jnp.dot(xn, wba_ref[...], preferred_element_type=F32)
    zt = ba.T[:SUBLANE]
    for r in range(nb):
        rest_ref[r, :, 6 * HEAD_W:] = ba[r * tm:(r + 1) * tm]
        zt_ref[r] = zt[:, r * tm:(r + 1) * tm]

    @pl.when(j == pl.num_programs(1) - 1)
    def _():
        convnew_ref[...] = hist_s[:, SUBLANE - (CONV_W - 1):SUBLANE, :]


def _in_proj(x, l, n1g, w, wba, cos_t, sin_t, conv_w, conv0, l0, nb, tm):
    bsz, t, _ = x.shape

    def layer(shape):
        return pl.BlockSpec((None,) + shape, lambda i, j: (l,) + (0,) * len(shape))

    return pl.pallas_call(
        _in_proj_kernel,
        grid=(bsz // nb, t // tm),
        in_specs=[
            pl.BlockSpec((nb, tm, D_MODEL), lambda i, j: (i, j, 0)),
            layer((1, D_MODEL)),
            layer((D_MODEL, Z_MAIN)),
            layer((D_MODEL, LANE)),
            pl.BlockSpec((tm, LANE), lambda i, j: (j, 0)),
            pl.BlockSpec((tm, LANE), lambda i, j: (j, 0)),
            layer((CONV_W, CONV_CH)),
            pl.BlockSpec((None, nb, CONV_W - 1, CONV_CH), lambda i, j: (l0, i, 0, 0)),
        ],
        out_specs=[
            pl.BlockSpec((nb, tm, 2 * HEAD_W), lambda i, j: (i, j, 0)),
            pl.BlockSpec((nb, tm, REST_W), lambda i, j: (i, j, 0)),
            pl.BlockSpec((nb, SUBLANE, tm), lambda i, j: (i, 0, j)),
            pl.BlockSpec((nb, CONV_W - 1, CONV_CH), lambda i, j: (i, 0, 0)),
        ],
        out_shape=[
            jax.ShapeDtypeStruct((bsz, t, 2 * HEAD_W), BF16),
            jax.ShapeDtypeStruct((bsz, t, REST_W), F32),
            jax.ShapeDtypeStruct((bsz, SUBLANE, t), F32),
            jax.ShapeDtypeStruct((bsz, CONV_W - 1, CONV_CH), F32),
        ],
        scratch_shapes=[pltpu.VMEM((nb, SUBLANE, CONV_CH), F32)],
        compiler_params=pltpu.CompilerParams(
            dimension_semantics=("arbitrary", "arbitrary"), vmem_limit_bytes=VMEM_LIMIT),
        name="in_proj",
    )(x, n1g, w, wba, cos_t, sin_t, conv_w, conv0)


def _bmm(a, b):
    return jnp.einsum("gmk,gkn->gmn", a.astype(BF16), b.astype(BF16), preferred_element_type=F32)


def _bmm_nt(a, b):
    return jnp.einsum("gmk,gnk->gmn", a.astype(BF16), b.astype(BF16), preferred_element_type=F32)


def _bmm_tn(a, b):
    return jnp.einsum("gkm,gkn->gmn", a.astype(BF16), b.astype(BF16), preferred_element_type=F32)


def _unit_lower_inverse(a):
    n = a.shape[-1]
    ri = lax.broadcasted_iota(jnp.int32, (n, n), 0)
    ci = lax.broadcasted_iota(jnp.int32, (n, n), 1)
    eye = (ri == ci).astype(F32)[None]
    inner = ((ri // 2) == (ci // 2))[None]
    p = eye - jnp.where(inner, a, 0.0)
    size = 2
    while size < n:
        size *= 2
        outer = ((ri // size) == (ci // size))[None]
        e = jnp.where(outer & jnp.logical_not(inner), a, 0.0)
        p = p - _bmm(p, _bmm(e, p))
        inner = outer
    return p


def _mixer_kernel(qr_ref, kr_ref, va_ref, ga_ref, qb_ref, kb_ref, vb_ref, gb_ref, ba_ref, zt_ref,
                  intra_ref, rc_ref, hpc_ref, hpr_ref, retg_ref, dng_ref, sret0_ref, sdn0_ref,
                  mix_ref, sret_out, sdn_out, sret_s, sdn_s, *, bb, cb):
    step = pl.program_id(1)
    ng = bb * HEADS
    state_shape = (ng, HEAD_DIM, HEAD_DIM)

    @pl.when(step == 0)
    def _():
        sret_s[...] = sret0_ref[...].reshape(state_shape)
        sdn_s[...] = sdn0_ref[...].reshape(state_shape)

    ri = lax.broadcasted_iota(jnp.int32, (CHUNK, CHUNK), 0)
    ci = lax.broadcasted_iota(jnp.int32, (CHUNK, CHUNK), 1)
    causal = ri >= ci
    strict = ri > ci
    ltri = causal.astype(BF16)
    utri = (ri <= ci).astype(BF16)
    dot = functools.partial(jnp.dot, preferred_element_type=F32)

    hpc = hpc_ref[...]
    hpr = hpr_ref[...]
    neg_a_col = -jnp.exp(hpc[0:1, :])
    neg_a_row = -jnp.exp(hpr[:, 0:1])
    dn_g = dng_ref[...]
    ret_g = retg_ref[...]
    heads = [slice(h * HEAD_DIM, (h + 1) * HEAD_DIM) for h in range(HEADS)]
    reps = cb * bb
    intra = jnp.concatenate([intra_ref[...]] * reps, axis=0)
    inject = jnp.concatenate([rc_ref[1]] * reps, axis=0)
    cross = jnp.concatenate([rc_ref[0]] * bb, axis=0)
    carry = jnp.concatenate([rc_ref[2, :, 0:1, :]] * bb, axis=0)
    ret_gain = jnp.stack([ret_g[:, hs] for hs in heads] * reps)

    def lanes(col, j):
        return jnp.broadcast_to(col[:, j:j + 1], (CHUNK, HEAD_DIM))

    def prepare(chunks):
        def heads_of(ref):
            return jnp.stack([ref[b, pl.ds(c * CHUNK, CHUNK), hs]
                              for c in chunks for b in range(bb) for hs in heads])

        beta, gc_c, gc_r = [], [], []
        for c in chunks:
            r0 = c * CHUNK
            for b in range(bb):
                ba = ba_ref[b, pl.ds(r0, CHUNK), :]
                zt = zt_ref[b, :, r0:r0 + CHUNK]
                beta_col = _sigmoid(ba)
                g_col = neg_a_col * _softplus(ba + hpc[1:2, :])
                g_row = neg_a_row * _softplus(zt + hpr[:, 1:2])
                gh, gm, gl = _split3(g_col)
                gc_col = dot(ltri, gh) + (dot(ltri, gm) + dot(ltri, gl))
                gh, gm, gl = _split3(g_row)
                gc_row = dot(gh, utri) + (dot(gm, utri) + dot(gl, utri))
                for h in range(HEADS):
                    beta.append(lanes(beta_col, h))
                    gc_c.append(lanes(gc_col, HEADS + h))
                    gc_r.append(gc_row[HEADS + h:HEADS + h + 1, :])

        q_r, k_r, v_r = heads_of(qr_ref), heads_of(kr_ref), heads_of(va_ref)
        o_intra = _bmm(_bmm_nt(q_r, k_r) * intra, v_r)
        s_inc = _bmm_tn(k_r, v_r * inject)

        q, k, v = heads_of(qb_ref), heads_of(kb_ref), heads_of(vb_ref)
        beta, gc_c = jnp.stack(beta), jnp.stack(gc_c)
        gc_last = gc_c[:, CHUNK - 1:CHUNK, :]
        e_gc = jnp.exp(gc_c)
        decay = jnp.exp(jnp.where(causal[None], gc_c[:, :, :CHUNK] - jnp.stack(gc_r), -jnp.inf))
        kbeta = k * beta
        kk = _bmm_nt(jnp.concatenate([kbeta, q], axis=1), k)
        a = jnp.where(strict[None], kk[:, :CHUNK] * decay, 0.0)
        tinv = _unit_lower_inverse(a)
        sol = _bmm(tinv, jnp.concatenate([v * beta, kbeta * e_gc], axis=2))
        return dict(
            q_r=q_r, o_intra=o_intra, s_inc=s_inc, attn=kk[:, CHUNK:] * decay,
            u=sol[:, :, :HEAD_DIM],
            wq=jnp.concatenate([sol[:, :, HEAD_DIM:], q * e_gc], axis=1),
            kg=k * jnp.exp(gc_last - gc_c), g_end=jnp.exp(gc_last),
            gate_a=heads_of(ga_ref), gate_b=heads_of(gb_ref))

    groups = [list(range(cb))]
    prepared = [prepare(chunks) for chunks in groups]

    s_ret = sret_s[...]
    s_dn = sdn_s[...]
    for chunks, pre in zip(groups, prepared):
        o_a, o_b = [], []
        for n in range(len(chunks)):
            sl = slice(n * ng, (n + 1) * ng)
            o_a.append(pre["o_intra"][sl] + _bmm(pre["q_r"][sl], s_ret) * cross)
            s_ret = s_ret * carry + pre["s_inc"][sl]
            ws = _bmm(pre["wq"][sl], s_dn)
            v_new = pre["u"][sl] - ws[:, :CHUNK]
            o_b.append(ws[:, CHUNK:] + _bmm(pre["attn"][sl], v_new))
            s_dn = s_dn * pre["g_end"][sl] + _bmm_tn(pre["kg"][sl], v_new)
        o = jnp.concatenate(o_a, axis=0)
        mu = jnp.mean(o, axis=-1, keepdims=True)
        oc = o - mu
        var = jnp.mean(oc * oc, axis=-1, keepdims=True)
        o_a = oc * lax.rsqrt(var + EPS) * ret_gain * pre["gate_a"]
        o_b = _rms(jnp.concatenate(o_b, axis=0), dn_g[None]) * pre["gate_b"]
        for n, c in enumerate(chunks):
            rows = pl.ds(c * CHUNK, CHUNK)
            for b in range(bb):
                for h, hs in enumerate(heads):
                    g = (n * bb + b) * HEADS + h
                    mix_ref[b, rows, hs] = o_a[g].astype(BF16)
                    mix_ref[b, rows, HEAD_W + h * HEAD_DIM:HEAD_W + (h + 1) * HEAD_DIM] = (
                        o_b[g].astype(BF16))
    sret_s[...] = s_ret
    sdn_s[...] = s_dn

    @pl.when(step == pl.num_programs(1) - 1)
    def _():
        sret_out[...] = sret_s[...].reshape(sret_out.shape)
        sdn_out[...] = sdn_s[...].reshape(sdn_out.shape)


def _mixer(qk, rest, zt, l, intra, rc, hpc, hpr, ret_g, dn_g, sret0, sdn0, l0, bb, cb):
    bsz, t, _ = qk.shape
    blk = cb * CHUNK
    grid = (bsz // bb, t // blk)

    def cols(col):
        return pl.BlockSpec((bb, blk, HEAD_W), lambda i, j: (i, j, col))

    def const(shape):
        return pl.BlockSpec(shape, lambda i, j: (0,) * len(shape))

    def layer(shape):
        return pl.BlockSpec((None,) + shape, lambda i, j: (l,) + (0,) * len(shape))

    state_shape = (bb, HEADS, HEAD_DIM, HEAD_DIM)
    state_in = pl.BlockSpec((None,) + state_shape, lambda i, j: (l0, i, 0, 0, 0))
    state_out = pl.BlockSpec(state_shape, lambda i, j: (i, 0, 0, 0))
    kern = functools.partial(_mixer_kernel, bb=bb, cb=cb)
    return pl.pallas_call(
        kern,
        grid=grid,
        in_specs=[cols(0), cols(1)] + [cols(c) for c in range(6)] + [
            pl.BlockSpec((bb, blk, LANE), lambda i, j: (i, j, 6 * HEAD_W // LANE)),
            pl.BlockSpec((bb, SUBLANE, blk), lambda i, j: (i, 0, j)),
            const((HEADS, CHUNK, CHUNK)),
            const((3, HEADS, CHUNK, LANE)),
            layer((SUBLANE, LANE)),
            layer((SUBLANE, LANE)),
            layer((1, HEAD_W)),
            layer((1, HEAD_DIM)),
            state_in, state_in,
        ],
        out_specs=[
            pl.BlockSpec((bb, blk, 2 * HEAD_W), lambda i, j: (i, j, 0)),
            state_out, state_out,
        ],
        out_shape=[
            jax.ShapeDtypeStruct((bsz, t, 2 * HEAD_W), BF16),
            jax.ShapeDtypeStruct((bsz, HEADS, HEAD_DIM, HEAD_DIM), F32),
            jax.ShapeDtypeStruct((bsz, HEADS, HEAD_DIM, HEAD_DIM), F32),
        ],
        scratch_shapes=[
            pltpu.VMEM((bb * HEADS, HEAD_DIM, HEAD_DIM), F32),
            pltpu.VMEM((bb * HEADS, HEAD_DIM, HEAD_DIM), F32),
        ],
        compiler_params=pltpu.CompilerParams(
            dimension_semantics=("arbitrary", "arbitrary"), vmem_limit_bytes=VMEM_LIMIT),
        name="mixer",
    )(qk, qk, *([rest] * 7), zt, intra, rc, hpc, hpr, ret_g, dn_g, sret0, sdn0)


def _post_kernel(h_ref, mix_ref, p_ref, wout_ref, n2g_ref, wup_ref, wdown_ref, proj_ref,
                 pleg_ref, gatew_ref, gateb_ref, fing_ref, out_ref, acc_s, xn_s, *, final):
    j = pl.program_id(1)

    @pl.when(j == 0)
    def _():
        h1 = h_ref[...] + jnp.dot(mix_ref[...], wout_ref[...],
                                  preferred_element_type=F32)
        acc_s[...] = h1
        xn_s[...] = _rms(h1, n2g_ref[...]).astype(BF16)

    up = jnp.maximum(jnp.dot(xn_s[...], wup_ref[...], preferred_element_type=F32), 0.0)
    acc_s[...] += jnp.dot((up * up).astype(BF16), wdown_ref[...], preferred_element_type=F32)

    @pl.when(j == pl.num_programs(1) - 1)
    def _():
        h2 = acc_s[...]
        gate = _sigmoid(jnp.dot(_rms(h2, pleg_ref[...]).astype(BF16), gatew_ref[...],
                                preferred_element_type=F32) + gateb_ref[...])
        h3 = h2 + jnp.dot(p_ref[...].astype(BF16), proj_ref[...],
                          preferred_element_type=F32) * gate
        if final:
            h3 = _rms(h3, fing_ref[...])
        out_ref[...] = h3


def _post(h, mix, p, l, w_out, n2g, w_up, w_down, proj, pleg, gate_w, gate_b, fin_g, final, tm, tf):
    m = h.shape[0]

    def row(width):
        return pl.BlockSpec((tm, width), lambda i, j: (i, 0))

    def layer(shape):
        return pl.BlockSpec((None,) + shape, lambda i, j: (l,) + (0,) * len(shape),
                            pipeline_mode=pl.Buffered(1))

    return pl.pallas_call(
        functools.partial(_post_kernel, final=final),
        grid=(m // tm, D_FF // tf),
        in_specs=[
            row(D_MODEL), row(2 * HEAD_W),
            pl.BlockSpec((None, tm, PLE_DIM), lambda i, j: (l, i, 0)),
            layer((2 * HEAD_W, D_MODEL)),
            layer((1, D_MODEL)),
            pl.BlockSpec((None, D_MODEL, tf), lambda i, j: (l, 0, j)),
            pl.BlockSpec((None, tf, D_MODEL), lambda i, j: (l, j, 0)),
            layer((PLE_DIM, D_MODEL)),
            layer((1, D_MODEL)),
            layer((D_MODEL, D_MODEL)),
            layer((1, D_MODEL)),
            pl.BlockSpec((1, D_MODEL), lambda i, j: (0, 0)),
        ],
        out_specs=row(D_MODEL),
        out_shape=jax.ShapeDtypeStruct((m, D_MODEL), F32),
        scratch_shapes=[pltpu.VMEM((tm, D_MODEL), F32), pltpu.VMEM((tm, D_MODEL), BF16)],
        compiler_params=pltpu.CompilerParams(
            dimension_semantics=("arbitrary", "arbitrary"), vmem_limit_bytes=VMEM_LIMIT),
        name="post",
    )(h, mix, p, w_out, n2g, w_up, w_down, proj, pleg, gate_w, gate_b, fin_g)


def _rope_tables(pos0, t):
    half = HEAD_DIM // 2
    step = min(t, LANE)
    inv = ROPE_BASE ** (-jnp.arange(half, dtype=F32) / half)
    ang_hi = (pos0 + step * jnp.arange(t // step)).astype(F32)[:, None] * inv[None, :]
    ang_lo = jnp.arange(step).astype(F32)[:, None] * inv[None, :]
    ch, sh = jnp.cos(ang_hi)[:, None, :], jnp.sin(ang_hi)[:, None, :]
    cl, sl = jnp.cos(ang_lo)[None, :, :], jnp.sin(ang_lo)[None, :, :]
    cos = (ch * cl - sh * sl).reshape(t, half)
    sin = (sh * cl + ch * sl).reshape(t, half)
    return jnp.concatenate([cos, cos], axis=-1), jnp.concatenate([-sin, sin], axis=-1)


def _retention_tables():
    log_gamma = jnp.log(1.0 - 2.0 ** (-5.0 - jnp.arange(HEADS, dtype=F32)))
    idx = jnp.arange(CHUNK, dtype=F32)
    intra = jnp.exp(jnp.abs(idx[:, None] - idx[None, :])[None] * log_gamma[:, None, None])
    cross = jnp.exp((idx + 1.0)[None, :] * log_gamma[:, None])
    inject = jnp.exp((CHUNK - 1.0 - idx)[None, :] * log_gamma[:, None])
    carry = jnp.exp(CHUNK * log_gamma)
    carry = jnp.broadcast_to(carry[:, None], (HEADS, CHUNK))
    rc = jnp.broadcast_to(jnp.stack([cross, inject, carry])[..., None], (3, HEADS, CHUNK, LANE))
    return intra, rc


def _trunk(x, p, conv0, sret0, sdn0, zero_state_layer, pos0, wts, intra, rc, bb, cb, nb_in, tm_in,
           tm, tf):
    bsz, t, _ = x.shape
    m = bsz * t
    depth = wts["w_in"].shape[0]
    cos_t, sin_t = _rope_tables(pos0, t)
    p = p.reshape(depth, m, PLE_DIM)
    h = x
    rets, dns, convs = [], [], []
    for l in range(depth):
        l0 = 0 if zero_state_layer else l
        qk, rest, zt, conv = _in_proj(h.reshape(bsz, t, D_MODEL), l, wts["n1g"], wts["w_in"],
                                      wts["w_ba"], cos_t, sin_t, wts["conv_w"],
                                      conv0, l0, nb_in, tm_in)
        mix, sret, sdn = _mixer(qk, rest, zt, l, intra, rc, wts["hpc"], wts["hpr"], wts["ret_g"],
                                wts["dn_g"], sret0, sdn0, l0, bb, cb)
        h = _post(h.reshape(m, D_MODEL), mix.reshape(m, 2 * HEAD_W), p, l, wts["w_out"], wts["n2g"],
                  wts["w_up"], wts["w_down"], wts["proj"], wts["pleg"], wts["gate_w"], wts["gate_b"],
                  wts["fin_g"], l == depth - 1, tm, tf)
        rets.append(sret)
        dns.append(sdn)
        convs.append(conv)
    return h.reshape(bsz, t, D_MODEL), jnp.stack(rets), jnp.stack(dns), jnp.stack(convs)


def kernel(x_prompt, x_sample, p_prompt, p_sample, state_ret, state_delta, state_conv, norm1_g, w_in, conv_w, dt_bias, a_log, ret_norm_g, dn_norm_g, w_out, norm2_g, w_up, w_down, ple_proj, ple_norm_g, ple_gate_w, ple_gate_b, final_norm_g):
    depth = w_in.shape[0]
    w_in_b = _cast_leading_cols(w_in, Z_MAIN, rows_per_step=256)
    w_gate_cols = w_in[:, :, Z_MAIN:].astype(BF16)
    hpc = jnp.zeros((depth, SUBLANE, LANE), F32)
    hpc = hpc.at[:, 0, HEADS:N_GATE].set(a_log).at[:, 1, HEADS:N_GATE].set(dt_bias)
    hpr = jnp.zeros((depth, SUBLANE, LANE), F32)
    hpr = hpr.at[:, HEADS:N_GATE, 0].set(a_log).at[:, HEADS:N_GATE, 1].set(dt_bias)
    wts = dict(
        n1g=norm1_g[:, None, :], w_in=w_in_b,
        w_ba=jnp.pad(w_gate_cols, ((0, 0), (0, 0), (0, LANE - N_GATE))),
        conv_w=conv_w, hpc=hpc, hpr=hpr,
        ret_g=ret_norm_g[:, None, :], dn_g=dn_norm_g[:, None, :],
        w_out=w_out.astype(BF16), n2g=norm2_g[:, None, :],
        w_up=w_up.astype(BF16), w_down=w_down.astype(BF16),
        proj=ple_proj.astype(BF16), pleg=ple_norm_g[:, None, :],
        gate_w=ple_gate_w.astype(BF16), gate_b=ple_gate_b[:, None, :],
        fin_g=final_norm_g[None, :])
    intra, rc = _retention_tables()
    bp = x_prompt.shape[0]
    zero_state = jnp.zeros((1, bp, HEADS, HEAD_DIM, HEAD_DIM), F32)
    zero_conv = jnp.zeros((1, bp, CONV_W - 1, CONV_CH), F32)
    y_p, ret_p, dn_p, conv_p = _trunk(x_prompt, p_prompt, zero_conv, zero_state, zero_state, True, 0,
                                      wts, intra, rc, bb=2, cb=4, nb_in=1, tm_in=1024, tm=1024,
                                      tf=1024)
    y_s, ret_s, dn_s, conv_s = _trunk(x_sample, p_sample, state_conv, state_ret, state_delta, False,
                                      PAST_LEN, wts, intra, rc, bb=8, cb=1, nb_in=8, tm_in=CHUNK,
                                      tm=1024, tf=1024)
    return (y_p, y_s, ret_p, dn_p, conv_p, ret_s, dn_s, conv_s)
```

```python
import functools

import jax
import jax.numpy as jnp
from jax import lax
from jax.experimental import pallas as pl
from jax.experimental.pallas import tpu as pltpu

F32 = jnp.float32
BF16 = jnp.bfloat16

D_MODEL = 1024
CHUNK = 64
HEADS = 4
HEAD_DIM = 128
HEAD_W = HEADS * HEAD_DIM
PLE_DIM = 256
D_FF = 4 * D_MODEL
CONV_W = 4
CONV_CH = 3 * HEAD_W
ROPE_BASE = 10000.0
EPS = 1e-6
PAST_LEN = 4096
Z_MAIN = 8 * HEAD_W
N_GATE = 2 * HEADS
LANE = 128
SUBLANE = 8
REST_W = 6 * HEAD_W + LANE
VMEM_LIMIT = 56 * 1024 * 1024
POST_ROWS = 1024
POST_FF_COLS = 1024


def _split3(x):
    hi = x.astype(BF16)
    r = x - hi.astype(F32)
    mid = r.astype(BF16)
    lo = (r - mid.astype(F32)).astype(BF16)
    return hi, mid, lo


def _rms(x, g):
    return x * lax.rsqrt(jnp.mean(x * x, axis=-1, keepdims=True) + EPS) * g


def _softplus(x):
    return jnp.maximum(x, 0.0) + jnp.log1p(jnp.exp(-jnp.abs(x)))


def _sigmoid(x):
    return 0.5 + 0.5 * jnp.tanh(0.5 * x)


def _silu(x):
    h = 0.5 * x
    return h + h * jnp.tanh(h)


def _in_proj_kernel(x_ref, g_ref, w_ref, wba_ref, cos_ref, sin_ref, convw_ref, conv0_ref,
                    qk_ref, rest_ref, zt_ref, convnew_ref, hist_s):
    j = pl.program_id(1)
    nb, tm, _ = x_ref.shape
    xn = _rms(x_ref[...].reshape(nb * tm, D_MODEL), g_ref[...]).astype(BF16)
    pair_w = 2 * HEAD_DIM

    def proj(col, pair):
        lo = col * HEAD_W + pair * pair_w
        return jnp.dot(xn, w_ref[:, lo:lo + pair_w], preferred_element_type=F32)

    @pl.when(j == 0)
    def _():
        hist_s[...] = jnp.zeros_like(hist_s)
        hist_s[:, SUBLANE - (CONV_W - 1):SUBLANE, :] = conv0_ref[...]

    def rope(col, h, z, r):
        scale = 1.0 if col == 0 else HEAD_DIM ** -0.5
        out = (z * cos_ref[...] + pltpu.roll(z, HEAD_DIM // 2, 1) * sin_ref[...]) * scale
        lo = col * HEAD_W + h * HEAD_DIM
        qk_ref[r, :, lo:lo + HEAD_DIM] = out.astype(BF16)

    def conv(col, h, z, r):
        n = col - 4
        cs = slice(n * HEAD_W + h * HEAD_DIM, n * HEAD_W + (h + 1) * HEAD_DIM)
        xp = jnp.concatenate([hist_s[r, :, cs], z], axis=0)
        acc = z * convw_ref[CONV_W - 1:CONV_W, cs]
        for i in range(CONV_W - 1):
            acc = acc + pltpu.roll(xp, CONV_W - 1 - i, 0)[SUBLANE:] * convw_ref[i:i + 1, cs]
        hist_s[r, :, cs] = z[tm - SUBLANE:, :]
        c = _silu(acc)
        if n == 0:
            c = c * lax.rsqrt(jnp.sum(c * c, axis=-1, keepdims=True) + EPS) * (HEAD_DIM ** -0.5)
        elif n == 1:
            c = c * lax.rsqrt(jnp.sum(c * c, axis=-1, keepdims=True) + EPS)
        lo = (col - 2) * HEAD_W + h * HEAD_DIM
        rest_ref[r, :, lo:lo + HEAD_DIM] = c

    def plain(col, h, z, r):
        rest_ref[r, :, h * HEAD_DIM:(h + 1) * HEAD_DIM] = z

    def gate(col, h, z, r):
        lo = (HEAD_W if col == 3 else 5 * HEAD_W) + h * HEAD_DIM
        rest_ref[r, :, lo:lo + HEAD_DIM] = _silu(z)

    finishers = {0: rope, 1: rope, 2: plain, 3: gate, 4: conv, 5: conv, 6: conv, 7: gate}

    def finish(col, pair, z):
        for hh in range(2):
            for r in range(nb):
                finishers[col](col, 2 * pair + hh,
                               z[r * tm:(r + 1) * tm, hh * HEAD_DIM:(hh + 1) * HEAD_DIM], r)

    order = ((4, 0), (0, 0), (4, 1), (0, 1), (5, 0), (1, 0), (5, 1), (1, 1),
             (6, 0), (3, 0), (6, 1), (3, 1), (7, 0), (7, 1), (2, 0), (2, 1))
    pending = None
    for col, pair in order:
        z = proj(col, pair)
        if pending is not None:
            finish(*pending)
        pending = (col, pair, z)
    finish(*pending)

    ba = jnp.dot(xn, wba_ref[...], preferred_element_type=F32)
    zt = ba.T[:SUBLANE]
    for r in range(nb):
        rest_ref[r, :, 6 * HEAD_W:] = ba[r * tm:(r + 1) * tm]
        zt_ref[r] = zt[:, r * tm:(r + 1) * tm]

    @pl.when(j == pl.num_programs(1) - 1)
    def _():
        convnew_ref[...] = hist_s[:, SUBLANE - (CONV_W - 1):SUBLANE, :]


def _in_proj(x, l, n1g, w, wba, cos_t, sin_t, conv_w, conv0, l0, nb, tm):
    bsz, t, _ = x.shape

    def layer(shape):
        return pl.BlockSpec((None,) + shape, lambda i, j: (l,) + (0,) * len(shape))

    return pl.pallas_call(
        _in_proj_kernel,
        grid=(bsz // nb, t // tm),
        in_specs=[
            pl.BlockSpec((nb, tm, D_MODEL), lambda i, j: (i, j, 0)),
            layer((1, D_MODEL)),
            layer((D_MODEL, Z_MAIN)),
            layer((D_MODEL, LANE)),
            pl.BlockSpec((tm, LANE), lambda i, j: (j, 0)),
            pl.BlockSpec((tm, LANE), lambda i, j: (j, 0)),
            layer((CONV_W, CONV_CH)),
            pl.BlockSpec((None, nb, CONV_W - 1, CONV_CH), lambda i, j: (l0, i, 0, 0)),
        ],
        out_specs=[
            pl.BlockSpec((nb, tm, 2 * HEAD_W), lambda i, j: (i, j, 0)),
            pl.BlockSpec((nb, tm, REST_W), lambda i, j: (i, j, 0)),
            pl.BlockSpec((nb, SUBLANE, tm), lambda i, j: (i, 0, j)),
            pl.BlockSpec((nb, CONV_W - 1, CONV_CH), lambda i, j: (i, 0, 0)),
        ],
        out_shape=[
            jax.ShapeDtypeStruct((bsz, t, 2 * HEAD_W), BF16),
            jax.ShapeDtypeStruct((bsz, t, REST_W), F32),
            jax.ShapeDtypeStruct((bsz, SUBLANE, t), F32),
            jax.ShapeDtypeStruct((bsz, CONV_W - 1, CONV_CH), F32),
        ],
        scratch_shapes=[pltpu.VMEM((nb, SUBLANE, CONV_CH), F32)],
        compiler_params=pltpu.CompilerParams(
            dimension_semantics=("arbitrary", "arbitrary"), vmem_limit_bytes=VMEM_LIMIT),
        name="in_proj",
    )(x, n1g, w, wba, cos_t, sin_t, conv_w, conv0)


def _bmm(a, b):
    return jnp.einsum("gmk,gkn->gmn", a.astype(BF16), b.astype(BF16), preferred_element_type=F32)


def _bmm_nt(a, b):
    return jnp.einsum("gmk,gnk->gmn", a.astype(BF16), b.astype(BF16), preferred_element_type=F32)


def _bmm_tn(a, b):
    return jnp.einsum("gkm,gkn->gmn", a.astype(BF16), b.astype(BF16), preferred_element_type=F32)


def _unit_lower_inverse(a):
    n = a.shape[-1]
    ri = lax.broadcasted_iota(jnp.int32, (n, n), 0)
    ci = lax.broadcasted_iota(jnp.int32, (n, n), 1)
    eye = (ri == ci).astype(F32)[None]
    inner = ((ri // 2) == (ci // 2))[None]
    p = eye - jnp.where(inner, a, 0.0)
    size = 2
    while size < n:
        size *= 2
        outer = ((ri // size) == (ci // size))[None]
        e = jnp.where(outer & jnp.logical_not(inner), a, 0.0)
        p = p - _bmm(p, _bmm(e, p))
        inner = outer
    return p


def _mixer_kernel(qr_ref, kr_ref, va_ref, ga_ref, qb_ref, kb_ref, vb_ref, gb_ref, ba_ref, zt_ref,
                  intra_ref, rc_ref, hpc_ref, hpr_ref, retg_ref, dng_ref, sret0_ref, sdn0_ref,
                  sret_all_ref, sdn_all_ref, mix_ref, sret_out, sdn_out, sret_s, sdn_s, *, bb, cb):
    del sret_all_ref, sdn_all_ref
    step = pl.program_id(1)
    ng = bb * HEADS
    state_shape = (ng, HEAD_DIM, HEAD_DIM)

    @pl.when(step == 0)
    def _():
        sret_s[...] = sret0_ref[...].reshape(state_shape)
        sdn_s[...] = sdn0_ref[...].reshape(state_shape)

    ri = lax.broadcasted_iota(jnp.int32, (CHUNK, CHUNK), 0)
    ci = lax.broadcasted_iota(jnp.int32, (CHUNK, CHUNK), 1)
    causal = ri >= ci
    strict = ri > ci
    ltri = causal.astype(BF16)
    utri = (ri <= ci).astype(BF16)
    dot = functools.partial(jnp.dot, preferred_element_type=F32)

    hpc = hpc_ref[...]
    hpr = hpr_ref[...]
    neg_a_col = -jnp.exp(hpc[0:1, :])
    neg_a_row = -jnp.exp(hpr[:, 0:1])
    dn_g = dng_ref[...]
    ret_g = retg_ref[...]
    heads = [slice(h * HEAD_DIM, (h + 1) * HEAD_DIM) for h in range(HEADS)]
    reps = cb * bb
    intra = jnp.concatenate([intra_ref[...]] * reps, axis=0)
    inject = jnp.concatenate([rc_ref[1]] * reps, axis=0)
    cross = jnp.concatenate([rc_ref[0]] * bb, axis=0)
    carry = jnp.concatenate([rc_ref[2, :, 0:1, :]] * bb, axis=0)
    ret_gain = jnp.stack([ret_g[:, hs] for hs in heads] * reps)

    def lanes(col, j):
        return jnp.broadcast_to(col[:, j:j + 1], (CHUNK, HEAD_DIM))

    def prepare(chunks):
        def heads_of(ref):
            return jnp.stack([ref[b, pl.ds(c * CHUNK, CHUNK), hs]
                              for c in chunks for b in range(bb) for hs in heads])

        beta, gc_c, gc_r = [], [], []
        for c in chunks:
            r0 = c * CHUNK
            for b in range(bb):
                ba = ba_ref[b, pl.ds(r0, CHUNK), :]
                zt = zt_ref[b, :, r0:r0 + CHUNK]
                beta_col = _sigmoid(ba)
                g_col = neg_a_col * _softplus(ba + hpc[1:2, :])
                g_row = neg_a_row * _softplus(zt + hpr[:, 1:2])
                gh, gm, gl = _split3(g_col)
                gc_col = dot(ltri, gh) + (dot(ltri, gm) + dot(ltri, gl))
                gh, gm, gl = _split3(g_row)
                gc_row = dot(gh, utri) + (dot(gm, utri) + dot(gl, utri))
                for h in range(HEADS):
                    beta.append(lanes(beta_col, h))
                    gc_c.append(lanes(gc_col, HEADS + h))
                    gc_r.append(gc_row[HEADS + h:HEADS + h + 1, :])

        q_r, k_r, v_r = heads_of(qr_ref), heads_of(kr_ref), heads_of(va_ref)
        o_intra = _bmm(_bmm_nt(q_r, k_r) * intra, v_r)
        s_inc = _bmm_tn(k_r, v_r * inject)

        q, k, v = heads_of(qb_ref), heads_of(kb_ref), heads_of(vb_ref)
        beta, gc_c = jnp.stack(beta), jnp.stack(gc_c)
        gc_last = gc_c[:, CHUNK - 1:CHUNK, :]
        e_gc = jnp.exp(gc_c)
        decay = jnp.exp(jnp.where(causal[None], gc_c[:, :, :CHUNK] - jnp.stack(gc_r), -jnp.inf))
        kbeta = k * beta
        kk = _bmm_nt(jnp.concatenate([kbeta, q], axis=1), k)
        a = jnp.where(strict[None], kk[:, :CHUNK] * decay, 0.0)
        tinv = _unit_lower_inverse(a)
        sol = _bmm(tinv, jnp.concatenate([v * beta, kbeta * e_gc], axis=2))
        return dict(
            q_r=q_r, o_intra=o_intra, s_inc=s_inc, attn=kk[:, CHUNK:] * decay,
            u=sol[:, :, :HEAD_DIM],
            wq=jnp.concatenate([sol[:, :, HEAD_DIM:], q * e_gc], axis=1),
            kg=k * jnp.exp(gc_last - gc_c), g_end=jnp.exp(gc_last),
            gate_a=heads_of(ga_ref), gate_b=heads_of(gb_ref))

    groups = [list(range(cb))]
    prepared = [prepare(chunks) for chunks in groups]

    s_ret = sret_s[...]
    s_dn = sdn_s[...]
    for chunks, pre in zip(groups, prepared):
        o_a, o_b = [], []
        for n in range(len(chunks)):
            sl = slice(n * ng, (n + 1) * ng)
            o_a.append(pre["o_intra"][sl] + _bmm(pre["q_r"][sl], s_ret) * cross)
            s_ret = s_ret * carry + pre["s_inc"][sl]
            ws = _bmm(pre["wq"][sl], s_dn)
            v_new = pre["u"][sl] - ws[:, :CHUNK]
            o_b.append(ws[:, CHUNK:] + _bmm(pre["attn"][sl], v_new))
            s_dn = s_dn * pre["g_end"][sl] + _bmm_tn(pre["kg"][sl], v_new)
        o = jnp.concatenate(o_a, axis=0)
        mu = jnp.mean(o, axis=-1, keepdims=True)
        oc = o - mu
        var = jnp.mean(oc * oc, axis=-1, keepdims=True)
        o_a = oc * lax.rsqrt(var + EPS) * ret_gain * pre["gate_a"]
        o_b = _rms(jnp.concatenate(o_b, axis=0), dn_g[None]) * pre["gate_b"]
        for n, c in enumerate(chunks):
            rows = pl.ds(c * CHUNK, CHUNK)
            for b in range(bb):
                for h, hs in enumerate(heads):
                    g = (n * bb + b) * HEADS + h
                    mix_ref[b, rows, hs] = o_a[g].astype(BF16)
                    mix_ref[b, rows, HEAD_W + h * HEAD_DIM:HEAD_W + (h + 1) * HEAD_DIM] = (
                        o_b[g].astype(BF16))
    sret_s[...] = s_ret
    sdn_s[...] = s_dn

    @pl.when(step == pl.num_programs(1) - 1)
    def _():
        sret_out[...] = sret_s[...].reshape(sret_out.shape)
        sdn_out[...] = sdn_s[...].reshape(sdn_out.shape)


def _mixer(qk, rest, zt, l, intra, rc, hpc, hpr, ret_g, dn_g, sret0, sdn0, l0, sret_all, sdn_all, bb,
           cb):
    bsz, t, _ = qk.shape
    blk = cb * CHUNK
    grid = (bsz // bb, t // blk)

    def cols(col):
        return pl.BlockSpec((bb, blk, HEAD_W), lambda i, j: (i, j, col))

    def const(shape):
        return pl.BlockSpec(shape, lambda i, j: (0,) * len(shape))

    def layer(shape):
        return pl.BlockSpec((None,) + shape, lambda i, j: (l,) + (0,) * len(shape))

    state_shape = (bb, HEADS, HEAD_DIM, HEAD_DIM)
    state_in = pl.BlockSpec((None,) + state_shape, lambda i, j: (l0, i, 0, 0, 0))
    state_out = pl.BlockSpec((None,) + state_shape, lambda i, j: (l, i, 0, 0, 0))
    in_place = pl.BlockSpec(memory_space=pl.ANY)
    in_specs = [cols(0), cols(1)] + [cols(c) for c in range(6)] + [
        pl.BlockSpec((bb, blk, LANE), lambda i, j: (i, j, 6 * HEAD_W // LANE)),
        pl.BlockSpec((bb, SUBLANE, blk), lambda i, j: (i, 0, j)),
        const((HEADS, CHUNK, CHUNK)),
        const((3, HEADS, CHUNK, LANE)),
        layer((SUBLANE, LANE)),
        layer((SUBLANE, LANE)),
        layer((1, HEAD_W)),
        layer((1, HEAD_DIM)),
        state_in, state_in, in_place, in_place,
    ]
    kern = functools.partial(_mixer_kernel, bb=bb, cb=cb)
    return pl.pallas_call(
        kern,
        grid=grid,
        in_specs=in_specs,
        out_specs=[
            pl.BlockSpec((bb, blk, 2 * HEAD_W), lambda i, j: (i, j, 0)),
            state_out, state_out,
        ],
        out_shape=[
            jax.ShapeDtypeStruct((bsz, t, 2 * HEAD_W), BF16),
            jax.ShapeDtypeStruct(sret_all.shape, F32),
            jax.ShapeDtypeStruct(sdn_all.shape, F32),
        ],
        input_output_aliases={len(in_specs) - 2: 1, len(in_specs) - 1: 2},
        scratch_shapes=[
            pltpu.VMEM((bb * HEADS, HEAD_DIM, HEAD_DIM), F32),
            pltpu.VMEM((bb * HEADS, HEAD_DIM, HEAD_DIM), F32),
        ],
        compiler_params=pltpu.CompilerParams(
            dimension_semantics=("arbitrary", "arbitrary"), vmem_limit_bytes=VMEM_LIMIT),
        name="mixer",
    )(qk, qk, *([rest] * 7), zt, intra, rc, hpc, hpr, ret_g, dn_g, sret0, sdn0, sret_all, sdn_all)


def _post_kernel(h_ref, mix_ref, p_ref, wout_ref, n2g_ref, wup_ref, wdown_ref, proj_ref,
                 pleg_ref, gatew_ref, gateb_ref, fing_ref, out_ref, acc_s, xn_s, *, final):
    j = pl.program_id(1)

    @pl.when(j == 0)
    def _():
        h1 = h_ref[...] + jnp.dot(mix_ref[...], wout_ref[...],
                                  preferred_element_type=F32)
        acc_s[...] = h1
        xn_s[...] = _rms(h1, n2g_ref[...]).astype(BF16)

    up = jnp.maximum(jnp.dot(xn_s[...], wup_ref[...], preferred_element_type=F32), 0.0)
    acc_s[...] += jnp.dot((up * up).astype(BF16), wdown_ref[...], preferred_element_type=F32)

    @pl.when(j == pl.num_programs(1) - 1)
    def _():
        h2 = acc_s[...]
        gate = _sigmoid(jnp.dot(_rms(h2, pleg_ref[...]).astype(BF16), gatew_ref[...],
                                preferred_element_type=F32) + gateb_ref[...])
        h3 = h2 + jnp.dot(p_ref[...].astype(BF16), proj_ref[...],
                          preferred_element_type=F32) * gate
        if final:
            h3 = _rms(h3, fing_ref[...])
        out_ref[...] = h3


def _post(h, mix, p, l, w_out, n2g, w_up, w_down, proj, pleg, gate_w, gate_b, fin_g, final):
    m = h.shape[0]
    tm, tf = POST_ROWS, POST_FF_COLS

    def row(width):
        return pl.BlockSpec((tm, width), lambda i, j: (i, 0))

    def layer(shape):
        return pl.BlockSpec((None,) + shape, lambda i, j: (l,) + (0,) * len(shape),
                            pipeline_mode=pl.Buffered(1))

    return pl.pallas_call(
        functools.partial(_post_kernel, final=final),
        grid=(m // tm, D_FF // tf),
        in_specs=[
            row(D_MODEL), row(2 * HEAD_W),
            pl.BlockSpec((None, tm, PLE_DIM), lambda i, j: (l, i, 0)),
            layer((2 * HEAD_W, D_MODEL)),
            layer((1, D_MODEL)),
            pl.BlockSpec((None, D_MODEL, tf), lambda i, j: (l, 0, j)),
            pl.BlockSpec((None, tf, D_MODEL), lambda i, j: (l, j, 0)),
            layer((PLE_DIM, D_MODEL)),
            layer((1, D_MODEL)),
            layer((D_MODEL, D_MODEL)),
            layer((1, D_MODEL)),
            pl.BlockSpec((1, D_MODEL), lambda i, j: (0, 0)),
        ],
        out_specs=row(D_MODEL),
        out_shape=jax.ShapeDtypeStruct((m, D_MODEL), F32),
        scratch_shapes=[pltpu.VMEM((tm, D_MODEL), F32), pltpu.VMEM((tm, D_MODEL), BF16)],
        compiler_params=pltpu.CompilerParams(
            dimension_semantics=("arbitrary", "arbitrary"), vmem_limit_bytes=VMEM_LIMIT),
        name="post",
    )(h, mix, p, w_out, n2g, w_up, w_down, proj, pleg, gate_w, gate_b, fin_g)


def _rope_tables(pos0, t):
    half = HEAD_DIM // 2
    step = min(t, LANE)
    inv = ROPE_BASE ** (-jnp.arange(half, dtype=F32) / half)
    ang_hi = (pos0 + step * jnp.arange(t // step)).astype(F32)[:, None] * inv[None, :]
    ang_lo = jnp.arange(step).astype(F32)[:, None] * inv[None, :]
    ch, sh = jnp.cos(ang_hi)[:, None, :], jnp.sin(ang_hi)[:, None, :]
    cl, sl = jnp.cos(ang_lo)[None, :, :], jnp.sin(ang_lo)[None, :, :]
    cos = (ch * cl - sh * sl).reshape(t, half)
    sin = (sh * cl + ch * sl).reshape(t, half)
    return jnp.concatenate([cos, cos], axis=-1), jnp.concatenate([-sin, sin], axis=-1)


def _retention_tables():
    log_gamma = jnp.log(1.0 - 2.0 ** (-5.0 - jnp.arange(HEADS, dtype=F32)))
    idx = jnp.arange(CHUNK, dtype=F32)
    intra = jnp.exp(jnp.abs(idx[:, None] - idx[None, :])[None] * log_gamma[:, None, None])
    cross = jnp.exp((idx + 1.0)[None, :] * log_gamma[:, None])
    inject = jnp.exp((CHUNK - 1.0 - idx)[None, :] * log_gamma[:, None])
    carry = jnp.exp(CHUNK * log_gamma)
    carry = jnp.broadcast_to(carry[:, None], (HEADS, CHUNK))
    rc = jnp.broadcast_to(jnp.stack([cross, inject, carry])[..., None], (3, HEADS, CHUNK, LANE))
    return intra, rc


def _trunk(x, p, conv0, sret0, sdn0, zero_state_layer, pos0, wts, intra, rc, bb, cb, nb_in, tm_in):
    bsz, t, _ = x.shape
    m = bsz * t
    depth = wts["w_in"].shape[0]
    cos_t, sin_t = _rope_tables(pos0, t)
    p = p.reshape(depth, m, PLE_DIM)
    h = x
    rets = jnp.zeros((depth, bsz, HEADS, HEAD_DIM, HEAD_DIM), F32)
    dns = jnp.zeros((depth, bsz, HEADS, HEAD_DIM, HEAD_DIM), F32)
    convs = []
    for l in range(depth):
        l0 = 0 if zero_state_layer else l
        qk, rest, zt, conv = _in_proj(h.reshape(bsz, t, D_MODEL), l, wts["n1g"], wts["w_in"],
                                      wts["w_ba"], cos_t, sin_t, wts["conv_w"],
                                      conv0, l0, nb_in, tm_in)
        mix, rets, dns = _mixer(qk, rest, zt, l, intra, rc, wts["hpc"], wts["hpr"], wts["ret_g"],
                                wts["dn_g"], sret0, sdn0, l0, rets, dns, bb, cb)
        h = _post(h.reshape(m, D_MODEL), mix.reshape(m, 2 * HEAD_W), p, l, wts["w_out"], wts["n2g"],
                  wts["w_up"], wts["w_down"], wts["proj"], wts["pleg"], wts["gate_w"], wts["gate_b"],
                  wts["fin_g"], l == depth - 1)
        convs.append(conv)
    return h.reshape(bsz, t, D_MODEL), rets, dns, jnp.stack(convs)


def kernel(x_prompt, x_sample, p_prompt, p_sample, state_ret, state_delta, state_conv, norm1_g, w_in, conv_w, dt_bias, a_log, ret_norm_g, dn_norm_g, w_out, norm2_g, w_up, w_down, ple_proj, ple_norm_g, ple_gate_w, ple_gate_b, final_norm_g):
    depth = w_in.shape[0]
    w_in_b = w_in[:, :, :Z_MAIN].astype(BF16)
    w_gate_cols = w_in[:, :, Z_MAIN:].astype(BF16)
    hpc = jnp.zeros((depth, SUBLANE, LANE), F32)
    hpc = hpc.at[:, 0, HEADS:N_GATE].set(a_log).at[:, 1, HEADS:N_GATE].set(dt_bias)
    hpr = jnp.zeros((depth, SUBLANE, LANE), F32)
    hpr = hpr.at[:, HEADS:N_GATE, 0].set(a_log).at[:, HEADS:N_GATE, 1].set(dt_bias)
    wts = dict(
        n1g=norm1_g[:, None, :], w_in=w_in_b,
        w_ba=jnp.pad(w_gate_cols, ((0, 0), (0, 0), (0, LANE - N_GATE))),
        conv_w=conv_w, hpc=hpc, hpr=hpr,
        ret_g=ret_norm_g[:, None, :], dn_g=dn_norm_g[:, None, :],
        w_out=w_out.astype(BF16), n2g=norm2_g[:, None, :],
        w_up=w_up.astype(BF16), w_down=w_down.astype(BF16),
        proj=ple_proj.astype(BF16), pleg=ple_norm_g[:, None, :],
        gate_w=ple_gate_w.astype(BF16), gate_b=ple_gate_b[:, None, :],
        fin_g=final_norm_g[None, :])
    intra, rc = _retention_tables()
    bp = x_prompt.shape[0]
    zero_state = jnp.zeros((1, bp, HEADS, HEAD_DIM, HEAD_DIM), F32)
    zero_conv = jnp.zeros((1, bp, CONV_W - 1, CONV_CH), F32)
    y_p, ret_p, dn_p, conv_p = _trunk(x_prompt, p_prompt, zero_conv, zero_state, zero_state, True, 0,
                                      wts, intra, rc, bb=2, cb=4, nb_in=1, tm_in=1024)
    y_s, ret_s, dn_s, conv_s = _trunk(x_sample, p_sample, state_conv, state_ret, state_delta, False,
                                      PAST_LEN, wts, intra, rc, bb=8, cb=1, nb_in=8, tm_in=CHUNK)
    return (y_p, y_s, ret_p, dn_p, conv_p, ret_s, dn_s, conv_s)
```

```python
import functools

import jax
import jax.numpy as jnp
from jax import lax
from jax.experimental import pallas as pl
from jax.experimental.pallas import tpu as pltpu

F32 = jnp.float32
BF16 = jnp.bfloat16

D_MODEL = 1024
CHUNK = 64
HEADS = 4
HEAD_DIM = 128
HEAD_W = HEADS * HEAD_DIM
PLE_DIM = 256
D_FF = 4 * D_MODEL
CONV_W = 4
CONV_CH = 3 * HEAD_W
ROPE_BASE = 10000.0
EPS = 1e-6
PAST_LEN = 4096
Z_MAIN = 8 * HEAD_W
N_GATE = 2 * HEADS
LANE = 128
SUBLANE = 8
REST_W = 6 * HEAD_W + LANE
VMEM_LIMIT = 56 * 1024 * 1024
POST_ROWS = 1024
POST_FF_COLS = 1024


def _split3(x):
    hi = x.astype(BF16)
    r = x - hi.astype(F32)
    mid = r.astype(BF16)
    lo = (r - mid.astype(F32)).astype(BF16)
    return hi, mid, lo


def _rms(x, g):
    return x * lax.rsqrt(jnp.mean(x * x, axis=-1, keepdims=True) + EPS) * g


def _softplus(x):
    return jnp.maximum(x, 0.0) + jnp.log1p(jnp.exp(-jnp.abs(x)))


def _sigmoid(x):
    return 0.5 + 0.5 * jnp.tanh(0.5 * x)


def _silu(x):
    h = 0.5 * x
    return h + h * jnp.tanh(h)


def _in_proj_kernel(x_ref, g_ref, w_ref, wba_ref, cos_ref, sin_ref, convw_ref, conv0_ref,
                    conv_all_ref, qk_ref, rest_ref, zt_ref, convnew_ref, hist_s, *, normed):
    del conv_all_ref
    j = pl.program_id(1)
    nb, tm, _ = x_ref.shape
    x = x_ref[...].reshape(nb * tm, D_MODEL)
    xn = x if normed else _rms(x, g_ref[...]).astype(BF16)
    pair_w = 2 * HEAD_DIM

    def proj(col, pair):
        lo = col * HEAD_W + pair * pair_w
        return jnp.dot(xn, w_ref[:, lo:lo + pair_w], preferred_element_type=F32)

    @pl.when(j == 0)
    def _():
        hist_s[...] = jnp.zeros_like(hist_s)
        hist_s[:, SUBLANE - (CONV_W - 1):SUBLANE, :] = conv0_ref[...]

    def rope(col, h, z, r):
        scale = 1.0 if col == 0 else HEAD_DIM ** -0.5
        out = (z * cos_ref[...] + pltpu.roll(z, HEAD_DIM // 2, 1) * sin_ref[...]) * scale
        lo = col * HEAD_W + h * HEAD_DIM
        qk_ref[r, :, lo:lo + HEAD_DIM] = out.astype(BF16)

    def conv(col, h, z, r):
        n = col - 4
        cs = slice(n * HEAD_W + h * HEAD_DIM, n * HEAD_W + (h + 1) * HEAD_DIM)
        xp = jnp.concatenate([hist_s[r, :, cs], z], axis=0)
        acc = z * convw_ref[CONV_W - 1:CONV_W, cs]
        for i in range(CONV_W - 1):
            acc = acc + pltpu.roll(xp, CONV_W - 1 - i, 0)[SUBLANE:] * convw_ref[i:i + 1, cs]
        hist_s[r, :, cs] = z[tm - SUBLANE:, :]
        c = _silu(acc)
        if n == 0:
            c = c * lax.rsqrt(jnp.sum(c * c, axis=-1, keepdims=True) + EPS) * (HEAD_DIM ** -0.5)
        elif n == 1:
            c = c * lax.rsqrt(jnp.sum(c * c, axis=-1, keepdims=True) + EPS)
        lo = (col - 2) * HEAD_W + h * HEAD_DIM
        rest_ref[r, :, lo:lo + HEAD_DIM] = c

    def plain(col, h, z, r):
        rest_ref[r, :, h * HEAD_DIM:(h + 1) * HEAD_DIM] = z

    def gate(col, h, z, r):
        lo = (HEAD_W if col == 3 else 5 * HEAD_W) + h * HEAD_DIM
        rest_ref[r, :, lo:lo + HEAD_DIM] = _silu(z)

    finishers = {0: rope, 1: rope, 2: plain, 3: gate, 4: conv, 5: conv, 6: conv, 7: gate}

    def finish(col, pair, z):
        for hh in range(2):
            for r in range(nb):
                finishers[col](col, 2 * pair + hh,
                               z[r * tm:(r + 1) * tm, hh * HEAD_DIM:(hh + 1) * HEAD_DIM], r)

    order = ((4, 0), (0, 0), (4, 1), (0, 1), (5, 0), (1, 0), (5, 1), (1, 1),
             (6, 0), (3, 0), (6, 1), (3, 1), (7, 0), (7, 1), (2, 0), (2, 1))
    pending = None
    for col, pair in order:
        z = proj(col, pair)
        if pending is not None:
            finish(*pending)
        pending = (col, pair, z)
    finish(*pending)

    ba = jnp.dot(xn, wba_ref[...], preferred_element_type=F32)
    zt = ba.T[:SUBLANE]
    for r in range(nb):
        rest_ref[r, :, 6 * HEAD_W:] = ba[r * tm:(r + 1) * tm]
        zt_ref[r] = zt[:, r * tm:(r + 1) * tm]

    @pl.when(j == pl.num_programs(1) - 1)
    def _():
        convnew_ref[...] = hist_s[:, SUBLANE - (CONV_W - 1):SUBLANE, :]


def _in_proj(x, l, n1g, w, wba, cos_t, sin_t, conv_w, conv0, l0, conv_all, nb, tm):
    bsz, t, _ = x.shape

    def layer(shape):
        return pl.BlockSpec((None,) + shape, lambda i, j: (l,) + (0,) * len(shape))

    in_specs = [
        pl.BlockSpec((nb, tm, D_MODEL), lambda i, j: (i, j, 0)),
        layer((1, D_MODEL)),
        layer((D_MODEL, Z_MAIN)),
        layer((D_MODEL, LANE)),
        pl.BlockSpec((tm, LANE), lambda i, j: (j, 0)),
        pl.BlockSpec((tm, LANE), lambda i, j: (j, 0)),
        layer((CONV_W, CONV_CH)),
        pl.BlockSpec((None, nb, CONV_W - 1, CONV_CH), lambda i, j: (l0, i, 0, 0)),
        pl.BlockSpec(memory_space=pl.ANY),
    ]
    return pl.pallas_call(
        functools.partial(_in_proj_kernel, normed=x.dtype == BF16),
        grid=(bsz // nb, t // tm),
        in_specs=in_specs,
        out_specs=[
            pl.BlockSpec((nb, tm, 2 * HEAD_W), lambda i, j: (i, j, 0)),
            pl.BlockSpec((nb, tm, REST_W), lambda i, j: (i, j, 0)),
            pl.BlockSpec((nb, SUBLANE, tm), lambda i, j: (i, 0, j)),
            pl.BlockSpec((None, nb, CONV_W - 1, CONV_CH), lambda i, j: (l, i, 0, 0)),
        ],
        out_shape=[
            jax.ShapeDtypeStruct((bsz, t, 2 * HEAD_W), BF16),
            jax.ShapeDtypeStruct((bsz, t, REST_W), F32),
            jax.ShapeDtypeStruct((bsz, SUBLANE, t), F32),
            jax.ShapeDtypeStruct(conv_all.shape, F32),
        ],
        input_output_aliases={len(in_specs) - 1: 3},
        scratch_shapes=[pltpu.VMEM((nb, SUBLANE, CONV_CH), F32)],
        compiler_params=pltpu.CompilerParams(
            dimension_semantics=("arbitrary", "arbitrary"), vmem_limit_bytes=VMEM_LIMIT),
        name="in_proj",
    )(x, n1g, w, wba, cos_t, sin_t, conv_w, conv0, conv_all)


def _bmm(a, b):
    return jnp.einsum("gmk,gkn->gmn", a.astype(BF16), b.astype(BF16), preferred_element_type=F32)


def _bmm_nt(a, b):
    return jnp.einsum("gmk,gnk->gmn", a.astype(BF16), b.astype(BF16), preferred_element_type=F32)


def _bmm_tn(a, b):
    return jnp.einsum("gkm,gkn->gmn", a.astype(BF16), b.astype(BF16), preferred_element_type=F32)


def _unit_lower_inverse(a):
    n = a.shape[-1]
    ri = lax.broadcasted_iota(jnp.int32, (n, n), 0)
    ci = lax.broadcasted_iota(jnp.int32, (n, n), 1)
    eye = (ri == ci).astype(F32)[None]
    inner = ((ri // 2) == (ci // 2))[None]
    p = eye - jnp.where(inner, a, 0.0)
    size = 2
    while size < n:
        size *= 2
        outer = ((ri // size) == (ci // size))[None]
        e = jnp.where(outer & jnp.logical_not(inner), a, 0.0)
        p = p - _bmm(p, _bmm(e, p))
        inner = outer
    return p


def _mixer_kernel(qr_ref, kr_ref, va_ref, ga_ref, qb_ref, kb_ref, vb_ref, gb_ref, ba_ref, zt_ref,
                  intra_ref, rc_ref, hpc_ref, hpr_ref, retg_ref, dng_ref, sret0_ref, sdn0_ref,
                  sret_all_ref, sdn_all_ref, mix_ref, sret_out, sdn_out, sret_s, sdn_s, *, bb, cb):
    del sret_all_ref, sdn_all_ref
    step = pl.program_id(1)
    ng = bb * HEADS
    state_shape = (ng, HEAD_DIM, HEAD_DIM)

    @pl.when(step == 0)
    def _():
        sret_s[...] = sret0_ref[...].reshape(state_shape)
        sdn_s[...] = sdn0_ref[...].reshape(state_shape)

    ri = lax.broadcasted_iota(jnp.int32, (CHUNK, CHUNK), 0)
    ci = lax.broadcasted_iota(jnp.int32, (CHUNK, CHUNK), 1)
    causal = ri >= ci
    strict = ri > ci
    ltri = causal.astype(BF16)
    utri = (ri <= ci).astype(BF16)
    dot = functools.partial(jnp.dot, preferred_element_type=F32)

    hpc = hpc_ref[...]
    hpr = hpr_ref[...]
    neg_a_col = -jnp.exp(hpc[0:1, :])
    neg_a_row = -jnp.exp(hpr[:, 0:1])
    dn_g = dng_ref[...]
    ret_g = retg_ref[...]
    heads = [slice(h * HEAD_DIM, (h + 1) * HEAD_DIM) for h in range(HEADS)]
    reps = cb * bb
    intra = jnp.concatenate([intra_ref[...]] * reps, axis=0)
    inject = jnp.concatenate([rc_ref[1]] * reps, axis=0)
    cross = jnp.concatenate([rc_ref[0]] * bb, axis=0)
    carry = jnp.concatenate([rc_ref[2, :, 0:1, :]] * bb, axis=0)
    ret_gain = jnp.stack([ret_g[:, hs] for hs in heads] * reps)

    def lanes(col, j):
        return jnp.broadcast_to(col[:, j:j + 1], (CHUNK, HEAD_DIM))

    def prepare(chunks):
        def heads_of(ref):
            return jnp.stack([ref[b, pl.ds(c * CHUNK, CHUNK), hs]
                              for c in chunks for b in range(bb) for hs in heads])

        beta, gc_c, gc_r = [], [], []
        for c in chunks:
            r0 = c * CHUNK
            for b in range(bb):
                ba = ba_ref[b, pl.ds(r0, CHUNK), :]
                zt = zt_ref[b, :, r0:r0 + CHUNK]
                beta_col = _sigmoid(ba)
                g_col = neg_a_col * _softplus(ba + hpc[1:2, :])
                g_row = neg_a_row * _softplus(zt + hpr[:, 1:2])
                gh, gm, gl = _split3(g_col)
                gc_col = dot(ltri, gh) + (dot(ltri, gm) + dot(ltri, gl))
                gh, gm, gl = _split3(g_row)
                gc_row = dot(gh, utri) + (dot(gm, utri) + dot(gl, utri))
                for h in range(HEADS):
                    beta.append(lanes(beta_col, h))
                    gc_c.append(lanes(gc_col, HEADS + h))
                    gc_r.append(gc_row[HEADS + h:HEADS + h + 1, :])

        q_r, k_r, v_r = heads_of(qr_ref), heads_of(kr_ref), heads_of(va_ref)
        o_intra = _bmm(_bmm_nt(q_r, k_r) * intra, v_r)
        s_inc = _bmm_tn(k_r, v_r * inject)

        q, k, v = heads_of(qb_ref), heads_of(kb_ref), heads_of(vb_ref)
        beta, gc_c = jnp.stack(beta), jnp.stack(gc_c)
        gc_last = gc_c[:, CHUNK - 1:CHUNK, :]
        e_gc = jnp.exp(gc_c)
        decay = jnp.exp(jnp.where(causal[None], gc_c[:, :, :CHUNK] - jnp.stack(gc_r), -jnp.inf))
        kbeta = k * beta
        kk = _bmm_nt(jnp.concatenate([kbeta, q], axis=1), k)
        a = jnp.where(strict[None], kk[:, :CHUNK] * decay, 0.0)
        tinv = _unit_lower_inverse(a)
        sol = _bmm(tinv, jnp.concatenate([v * beta, kbeta * e_gc], axis=2))
        return dict(
            q_r=q_r, o_intra=o_intra, s_inc=s_inc, attn=kk[:, CHUNK:] * decay,
            u=sol[:, :, :HEAD_DIM],
            wq=jnp.concatenate([sol[:, :, HEAD_DIM:], q * e_gc], axis=1),
            kg=k * jnp.exp(gc_last - gc_c), g_end=jnp.exp(gc_last),
            gate_a=heads_of(ga_ref), gate_b=heads_of(gb_ref))

    groups = [list(range(cb))]
    prepared = [prepare(chunks) for chunks in groups]

    s_ret = sret_s[...]
    s_dn = sdn_s[...]
    for chunks, pre in zip(groups, prepared):
        o_a, o_b = [], []
        for n in range(len(chunks)):
            sl = slice(n * ng, (n + 1) * ng)
            o_a.append(pre["o_intra"][sl] + _bmm(pre["q_r"][sl], s_ret) * cross)
            s_ret = s_ret * carry + pre["s_inc"][sl]
            ws = _bmm(pre["wq"][sl], s_dn)
            v_new = pre["u"][sl] - ws[:, :CHUNK]
            o_b.append(ws[:, CHUNK:] + _bmm(pre["attn"][sl], v_new))
            s_dn = s_dn * pre["g_end"][sl] + _bmm_tn(pre["kg"][sl], v_new)
        o = jnp.concatenate(o_a, axis=0)
        mu = jnp.mean(o, axis=-1, keepdims=True)
        oc = o - mu
        var = jnp.mean(oc * oc, axis=-1, keepdims=True)
        o_a = oc * lax.rsqrt(var + EPS) * ret_gain * pre["gate_a"]
        o_b = _rms(jnp.concatenate(o_b, axis=0), dn_g[None]) * pre["gate_b"]
        for n, c in enumerate(chunks):
            rows = pl.ds(c * CHUNK, CHUNK)
            for b in range(bb):
                for h, hs in enumerate(heads):
                    g = (n * bb + b) * HEADS + h
                    mix_ref[b, rows, hs] = o_a[g].astype(BF16)
                    mix_ref[b, rows, HEAD_W + h * HEAD_DIM:HEAD_W + (h + 1) * HEAD_DIM] = (
                        o_b[g].astype(BF16))
    sret_s[...] = s_ret
    sdn_s[...] = s_dn

    @pl.when(step == pl.num_programs(1) - 1)
    def _():
        sret_out[...] = sret_s[...].reshape(sret_out.shape)
        sdn_out[...] = sdn_s[...].reshape(sdn_out.shape)


def _mixer(qk, rest, zt, l, intra, rc, hpc, hpr, ret_g, dn_g, sret0, sdn0, l0, sret_all, sdn_all, bb,
           cb):
    bsz, t, _ = qk.shape
    blk = cb * CHUNK
    grid = (bsz // bb, t // blk)

    def cols(col):
        return pl.BlockSpec((bb, blk, HEAD_W), lambda i, j: (i, j, col))

    def const(shape):
        return pl.BlockSpec(shape, lambda i, j: (0,) * len(shape))

    def layer(shape):
        return pl.BlockSpec((None,) + shape, lambda i, j: (l,) + (0,) * len(shape))

    state_shape = (bb, HEADS, HEAD_DIM, HEAD_DIM)
    state_in = pl.BlockSpec((None,) + state_shape, lambda i, j: (l0, i, 0, 0, 0))
    state_out = pl.BlockSpec((None,) + state_shape, lambda i, j: (l, i, 0, 0, 0))
    in_place = pl.BlockSpec(memory_space=pl.ANY)
    in_specs = [cols(0), cols(1)] + [cols(c) for c in range(6)] + [
        pl.BlockSpec((bb, blk, LANE), lambda i, j: (i, j, 6 * HEAD_W // LANE)),
        pl.BlockSpec((bb, SUBLANE, blk), lambda i, j: (i, 0, j)),
        const((HEADS, CHUNK, CHUNK)),
        const((3, HEADS, CHUNK, LANE)),
        layer((SUBLANE, LANE)),
        layer((SUBLANE, LANE)),
        layer((1, HEAD_W)),
        layer((1, HEAD_DIM)),
        state_in, state_in, in_place, in_place,
    ]
    kern = functools.partial(_mixer_kernel, bb=bb, cb=cb)
    return pl.pallas_call(
        kern,
        grid=grid,
        in_specs=in_specs,
        out_specs=[
            pl.BlockSpec((bb, blk, 2 * HEAD_W), lambda i, j: (i, j, 0)),
            state_out, state_out,
        ],
        out_shape=[
            jax.ShapeDtypeStruct((bsz, t, 2 * HEAD_W), BF16),
            jax.ShapeDtypeStruct(sret_all.shape, F32),
            jax.ShapeDtypeStruct(sdn_all.shape, F32),
        ],
        input_output_aliases={len(in_specs) - 2: 1, len(in_specs) - 1: 2},
        scratch_shapes=[
            pltpu.VMEM((bb * HEADS, HEAD_DIM, HEAD_DIM), F32),
            pltpu.VMEM((bb * HEADS, HEAD_DIM, HEAD_DIM), F32),
        ],
        compiler_params=pltpu.CompilerParams(
            dimension_semantics=("arbitrary", "arbitrary"), vmem_limit_bytes=VMEM_LIMIT),
        name="mixer",
    )(qk, qk, *([rest] * 7), zt, intra, rc, hpc, hpr, ret_g, dn_g, sret0, sdn0, sret_all, sdn_all)


def _post_kernel(h_ref, mix_ref, p_ref, wout_ref, n2g_ref, wup_ref, wdown_ref, proj_ref,
                 pleg_ref, gatew_ref, gateb_ref, nextg_ref, *out_and_scratch, final):
    out_ref, *next_ref = out_and_scratch[:-2]
    acc_s, xn_s = out_and_scratch[-2:]
    j = pl.program_id(1)

    @pl.when(j == 0)
    def _():
        h1 = h_ref[...] + jnp.dot(mix_ref[...], wout_ref[...],
                                  preferred_element_type=F32)
        acc_s[...] = h1
        xn_s[...] = _rms(h1, n2g_ref[...]).astype(BF16)

    up = jnp.maximum(jnp.dot(xn_s[...], wup_ref[...], preferred_element_type=F32), 0.0)
    acc_s[...] += jnp.dot((up * up).astype(BF16), wdown_ref[...], preferred_element_type=F32)

    @pl.when(j == pl.num_programs(1) - 1)
    def _():
        h2 = acc_s[...]
        gate = _sigmoid(jnp.dot(_rms(h2, pleg_ref[...]).astype(BF16), gatew_ref[...],
                                preferred_element_type=F32) + gateb_ref[...])
        h3 = h2 + jnp.dot(p_ref[...].astype(BF16), proj_ref[...],
                          preferred_element_type=F32) * gate
        normed = _rms(h3, nextg_ref[...])
        if final:
            out_ref[...] = normed
        else:
            out_ref[...] = h3
            next_ref[0][...] = normed.astype(BF16)


def _post(h, mix, p, l, w_out, n2g, w_up, w_down, proj, pleg, gate_w, gate_b, next_g, final):
    m = h.shape[0]
    tm, tf = POST_ROWS, POST_FF_COLS
    out_dtypes = (F32,) if final else (F32, BF16)

    def row(width):
        return pl.BlockSpec((tm, width), lambda i, j: (i, 0))

    def layer(shape):
        return pl.BlockSpec((None,) + shape, lambda i, j: (l,) + (0,) * len(shape),
                            pipeline_mode=pl.Buffered(1))

    return pl.pallas_call(
        functools.partial(_post_kernel, final=final),
        grid=(m // tm, D_FF // tf),
        in_specs=[
            row(D_MODEL), row(2 * HEAD_W),
            pl.BlockSpec((None, tm, PLE_DIM), lambda i, j: (l, i, 0)),
            layer((2 * HEAD_W, D_MODEL)),
            layer((1, D_MODEL)),
            pl.BlockSpec((None, D_MODEL, tf), lambda i, j: (l, 0, j)),
            pl.BlockSpec((None, tf, D_MODEL), lambda i, j: (l, j, 0)),
            layer((PLE_DIM, D_MODEL)),
            layer((1, D_MODEL)),
            layer((D_MODEL, D_MODEL)),
            layer((1, D_MODEL)),
            pl.BlockSpec((1, D_MODEL), lambda i, j: (0, 0)),
        ],
        out_specs=[row(D_MODEL)] * len(out_dtypes),
        out_shape=[jax.ShapeDtypeStruct((m, D_MODEL), dt) for dt in out_dtypes],
        scratch_shapes=[pltpu.VMEM((tm, D_MODEL), F32), pltpu.VMEM((tm, D_MODEL), BF16)],
        compiler_params=pltpu.CompilerParams(
            dimension_semantics=("arbitrary", "arbitrary"), vmem_limit_bytes=VMEM_LIMIT),
        name="post",
    )(h, mix, p, w_out, n2g, w_up, w_down, proj, pleg, gate_w, gate_b, next_g)


def _rope_tables(pos0, t):
    half = HEAD_DIM // 2
    step = min(t, LANE)
    inv = ROPE_BASE ** (-jnp.arange(half, dtype=F32) / half)
    ang_hi = (pos0 + step * jnp.arange(t // step)).astype(F32)[:, None] * inv[None, :]
    ang_lo = jnp.arange(step).astype(F32)[:, None] * inv[None, :]
    ch, sh = jnp.cos(ang_hi)[:, None, :], jnp.sin(ang_hi)[:, None, :]
    cl, sl = jnp.cos(ang_lo)[None, :, :], jnp.sin(ang_lo)[None, :, :]
    cos = (ch * cl - sh * sl).reshape(t, half)
    sin = (sh * cl + ch * sl).reshape(t, half)
    return jnp.concatenate([cos, cos], axis=-1), jnp.concatenate([-sin, sin], axis=-1)


def _retention_tables():
    log_gamma = jnp.log(1.0 - 2.0 ** (-5.0 - jnp.arange(HEADS, dtype=F32)))
    idx = jnp.arange(CHUNK, dtype=F32)
    intra = jnp.exp(jnp.abs(idx[:, None] - idx[None, :])[None] * log_gamma[:, None, None])
    cross = jnp.exp((idx + 1.0)[None, :] * log_gamma[:, None])
    inject = jnp.exp((CHUNK - 1.0 - idx)[None, :] * log_gamma[:, None])
    carry = jnp.exp(CHUNK * log_gamma)
    carry = jnp.broadcast_to(carry[:, None], (HEADS, CHUNK))
    rc = jnp.broadcast_to(jnp.stack([cross, inject, carry])[..., None], (3, HEADS, CHUNK, LANE))
    return intra, rc


def _trunk(x, p, conv0, sret0, sdn0, zero_state_layer, pos0, wts, intra, rc, bb, cb, nb_in, tm_in):
    bsz, t, _ = x.shape
    m = bsz * t
    depth = wts["w_in"].shape[0]
    cos_t, sin_t = _rope_tables(pos0, t)
    p = p.reshape(depth, m, PLE_DIM)
    h = x.reshape(m, D_MODEL)
    src = x
    rets = jnp.zeros((depth, bsz, HEADS, HEAD_DIM, HEAD_DIM), F32)
    dns = jnp.zeros((depth, bsz, HEADS, HEAD_DIM, HEAD_DIM), F32)
    convs = jnp.zeros((depth, bsz, CONV_W - 1, CONV_CH), F32)
    for l in range(depth):
        l0 = 0 if zero_state_layer else l
        final = l == depth - 1
        qk, rest, zt, convs = _in_proj(src.reshape(bsz, t, D_MODEL), l, wts["n1g"], wts["w_in"],
                                       wts["w_ba"], cos_t, sin_t, wts["conv_w"],
                                       conv0, l0, convs, nb_in, tm_in)
        mix, rets, dns = _mixer(qk, rest, zt, l, intra, rc, wts["hpc"], wts["hpr"], wts["ret_g"],
                                wts["dn_g"], sret0, sdn0, l0, rets, dns, bb, cb)
        next_g = wts["fin_g"] if final else wts["n1g"][l + 1]
        out = _post(h, mix.reshape(m, 2 * HEAD_W), p, l, wts["w_out"], wts["n2g"], wts["w_up"],
                    wts["w_down"], wts["proj"], wts["pleg"], wts["gate_w"], wts["gate_b"], next_g,
                    final)
        if not final:
            h, src = out
    return out[0].reshape(bsz, t, D_MODEL), rets, dns, convs


def kernel(x_prompt, x_sample, p_prompt, p_sample, state_ret, state_delta, state_conv, norm1_g, w_in, conv_w, dt_bias, a_log, ret_norm_g, dn_norm_g, w_out, norm2_g, w_up, w_down, ple_proj, ple_norm_g, ple_gate_w, ple_gate_b, final_norm_g):
    depth = w_in.shape[0]
    w_in_b = w_in[:, :, :Z_MAIN].astype(BF16)
    w_gate_cols = w_in[:, :, Z_MAIN:].astype(BF16)
    hpc = jnp.zeros((depth, SUBLANE, LANE), F32)
    hpc = hpc.at[:, 0, HEADS:N_GATE].set(a_log).at[:, 1, HEADS:N_GATE].set(dt_bias)
    hpr = jnp.zeros((depth, SUBLANE, LANE), F32)
    hpr = hpr.at[:, HEADS:N_GATE, 0].set(a_log).at[:, HEADS:N_GATE, 1].set(dt_bias)
    wts = dict(
        n1g=norm1_g[:, None, :], w_in=w_in_b,
        w_ba=jnp.pad(w_gate_cols, ((0, 0), (0, 0), (0, LANE - N_GATE))),
        conv_w=conv_w, hpc=hpc, hpr=hpr,
        ret_g=ret_norm_g[:, None, :], dn_g=dn_norm_g[:, None, :],
        w_out=w_out.astype(BF16), n2g=norm2_g[:, None, :],
        w_up=w_up.astype(BF16), w_down=w_down.astype(BF16),
        proj=ple_proj.astype(BF16), pleg=ple_norm_g[:, None, :],
        gate_w=ple_gate_w.astype(BF16), gate_b=ple_gate_b[:, None, :],
        fin_g=final_norm_g[None, :])
    intra, rc = _retention_tables()
    bp = x_prompt.shape[0]
    zero_state = jnp.zeros((1, bp, HEADS, HEAD_DIM, HEAD_DIM), F32)
    zero_conv = jnp.zeros((1, bp, CONV_W - 1, CONV_CH), F32)
    y_p, ret_p, dn_p, conv_p = _trunk(x_prompt, p_prompt, zero_conv, zero_state, zero_state, True, 0,
                                      wts, intra, rc, bb=2, cb=4, nb_in=1, tm_in=1024)
    y_s, ret_s, dn_s, conv_s = _trunk(x_sample, p_sample, state_conv, state_ret, state_delta, False,
                                      PAST_LEN, wts, intra, rc, bb=8, cb=1, nb_in=8, tm_in=CHUNK)
    return (y_p, y_s, ret_p, dn_p, conv_p, ret_s, dn_s, conv_s)
```

```python
import functools

import jax
import jax.numpy as jnp
from jax import lax
from jax.experimental import pallas as pl
from jax.experimental.pallas import tpu as pltpu

F32 = jnp.float32
BF16 = jnp.bfloat16

D_MODEL = 1024
CHUNK = 64
HEADS = 4
HEAD_DIM = 128
HEAD_W = HEADS * HEAD_DIM
PLE_DIM = 256
D_FF = 4 * D_MODEL
CONV_W = 4
CONV_CH = 3 * HEAD_W
ROPE_BASE = 10000.0
EPS = 1e-6
PAST_LEN = 4096
Z_MAIN = 8 * HEAD_W
N_GATE = 2 * HEADS
LANE = 128
SUBLANE = 8
REST_W = 6 * HEAD_W + LANE
VMEM_LIMIT = 56 * 1024 * 1024
W_BLOCK = 1024
POST_ROWS = 1024
POST_FF_COLS = 1024


def _split3(x):
    hi = x.astype(BF16)
    r = x - hi.astype(F32)
    mid = r.astype(BF16)
    lo = (r - mid.astype(F32)).astype(BF16)
    return hi, mid, lo


def _rms(x, g):
    return x * lax.rsqrt(jnp.mean(x * x, axis=-1, keepdims=True) + EPS) * g


def _softplus(x):
    return jnp.maximum(x, 0.0) + jnp.log1p(jnp.exp(-jnp.abs(x)))


def _sigmoid(x):
    return 0.5 + 0.5 * jnp.tanh(0.5 * x)


def _silu(x):
    h = 0.5 * x
    return h + h * jnp.tanh(h)


def _in_proj_kernel(x_ref, g_ref, w0_ref, w1_ref, w2_ref, w3_ref, wba_ref, cos_ref, sin_ref,
                    convw_ref, conv0_ref, conv_all_ref, qk_ref, rest_ref, zt_ref, convnew_ref,
                    hist_s):
    w_refs = (w0_ref, w1_ref, w2_ref, w3_ref)
    del conv_all_ref
    j = pl.program_id(1)
    nb, tm, _ = x_ref.shape
    xn = _rms(x_ref[...].reshape(nb * tm, D_MODEL), g_ref[...]).astype(BF16)
    pair_w = 2 * HEAD_DIM

    def proj(col, pair):
        blk, lo = divmod(col * HEAD_W + pair * pair_w, W_BLOCK)
        return jnp.dot(xn, w_refs[blk][:, lo:lo + pair_w], preferred_element_type=F32)

    @pl.when(j == 0)
    def _():
        hist_s[...] = jnp.zeros_like(hist_s)
        hist_s[:, SUBLANE - (CONV_W - 1):SUBLANE, :] = conv0_ref[...]

    def rope(col, h, z, r):
        scale = 1.0 if col == 0 else HEAD_DIM ** -0.5
        out = (z * cos_ref[...] + pltpu.roll(z, HEAD_DIM // 2, 1) * sin_ref[...]) * scale
        lo = col * HEAD_W + h * HEAD_DIM
        qk_ref[r, :, lo:lo + HEAD_DIM] = out.astype(BF16)

    def conv(col, h, z, r):
        n = col - 4
        cs = slice(n * HEAD_W + h * HEAD_DIM, n * HEAD_W + (h + 1) * HEAD_DIM)
        xp = jnp.concatenate([hist_s[r, :, cs], z], axis=0)
        acc = z * convw_ref[CONV_W - 1:CONV_W, cs]
        for i in range(CONV_W - 1):
            acc = acc + pltpu.roll(xp, CONV_W - 1 - i, 0)[SUBLANE:] * convw_ref[i:i + 1, cs]
        hist_s[r, :, cs] = z[tm - SUBLANE:, :]
        c = _silu(acc)
        if n == 0:
            c = c * lax.rsqrt(jnp.sum(c * c, axis=-1, keepdims=True) + EPS) * (HEAD_DIM ** -0.5)
        elif n == 1:
            c = c * lax.rsqrt(jnp.sum(c * c, axis=-1, keepdims=True) + EPS)
        lo = (col - 2) * HEAD_W + h * HEAD_DIM
        rest_ref[r, :, lo:lo + HEAD_DIM] = c

    def plain(col, h, z, r):
        rest_ref[r, :, h * HEAD_DIM:(h + 1) * HEAD_DIM] = z

    def gate(col, h, z, r):
        lo = (HEAD_W if col == 3 else 5 * HEAD_W) + h * HEAD_DIM
        rest_ref[r, :, lo:lo + HEAD_DIM] = _silu(z)

    finishers = {0: rope, 1: rope, 2: plain, 3: gate, 4: conv, 5: conv, 6: conv, 7: gate}

    def finish(col, pair, z):
        for hh in range(2):
            for r in range(nb):
                finishers[col](col, 2 * pair + hh,
                               z[r * tm:(r + 1) * tm, hh * HEAD_DIM:(hh + 1) * HEAD_DIM], r)

    order = ((4, 0), (0, 0), (4, 1), (0, 1), (5, 0), (1, 0), (5, 1), (1, 1),
             (6, 0), (3, 0), (6, 1), (3, 1), (7, 0), (7, 1), (2, 0), (2, 1))
    pending = None
    for col, pair in order:
        z = proj(col, pair)
        if pending is not None:
            finish(*pending)
        pending = (col, pair, z)
    finish(*pending)

    ba = jnp.dot(xn, wba_ref[...], preferred_element_type=F32)
    zt = ba.T[:SUBLANE]
    for r in range(nb):
        rest_ref[r, :, 6 * HEAD_W:] = ba[r * tm:(r + 1) * tm]
        zt_ref[r] = zt[:, r * tm:(r + 1) * tm]

    @pl.when(j == pl.num_programs(1) - 1)
    def _():
        convnew_ref[...] = hist_s[:, SUBLANE - (CONV_W - 1):SUBLANE, :]


def _in_proj(x, l, n1g, w, wba, cos_t, sin_t, conv_w, conv0, l0, conv_all, nb, tm):
    bsz, t, _ = x.shape

    def layer(shape):
        return pl.BlockSpec((None,) + shape, lambda i, j: (l,) + (0,) * len(shape))

    in_specs = [
        pl.BlockSpec((nb, tm, D_MODEL), lambda i, j: (i, j, 0)),
        layer((1, D_MODEL)),
        *[pl.BlockSpec((None, D_MODEL, W_BLOCK), lambda i, j, c=c: (l, 0, c))
          for c in range(Z_MAIN // W_BLOCK)],
        layer((D_MODEL, LANE)),
        pl.BlockSpec((tm, LANE), lambda i, j: (j, 0)),
        pl.BlockSpec((tm, LANE), lambda i, j: (j, 0)),
        layer((CONV_W, CONV_CH)),
        pl.BlockSpec((None, nb, CONV_W - 1, CONV_CH), lambda i, j: (l0, i, 0, 0)),
        pl.BlockSpec(memory_space=pl.ANY),
    ]
    return pl.pallas_call(
        _in_proj_kernel,
        grid=(bsz // nb, t // tm),
        in_specs=in_specs,
        out_specs=[
            pl.BlockSpec((nb, tm, 2 * HEAD_W), lambda i, j: (i, j, 0)),
            pl.BlockSpec((nb, tm, REST_W), lambda i, j: (i, j, 0)),
            pl.BlockSpec((nb, SUBLANE, tm), lambda i, j: (i, 0, j)),
            pl.BlockSpec((None, nb, CONV_W - 1, CONV_CH), lambda i, j: (l, i, 0, 0)),
        ],
        out_shape=[
            jax.ShapeDtypeStruct((bsz, t, 2 * HEAD_W), BF16),
            jax.ShapeDtypeStruct((bsz, t, REST_W), F32),
            jax.ShapeDtypeStruct((bsz, SUBLANE, t), F32),
            jax.ShapeDtypeStruct(conv_all.shape, F32),
        ],
        input_output_aliases={len(in_specs) - 1: 3},
        scratch_shapes=[pltpu.VMEM((nb, SUBLANE, CONV_CH), F32)],
        compiler_params=pltpu.CompilerParams(
            dimension_semantics=("arbitrary", "arbitrary"), vmem_limit_bytes=VMEM_LIMIT),
        name="in_proj",
    )(x, n1g, *([w] * (Z_MAIN // W_BLOCK)), wba, cos_t, sin_t, conv_w, conv0, conv_all)


def _bmm(a, b):
    return jnp.einsum("gmk,gkn->gmn", a.astype(BF16), b.astype(BF16), preferred_element_type=F32)


def _bmm_nt(a, b):
    return jnp.einsum("gmk,gnk->gmn", a.astype(BF16), b.astype(BF16), preferred_element_type=F32)


def _bmm_tn(a, b):
    return jnp.einsum("gkm,gkn->gmn", a.astype(BF16), b.astype(BF16), preferred_element_type=F32)


def _unit_lower_inverse(a):
    n = a.shape[-1]
    ri = lax.broadcasted_iota(jnp.int32, (n, n), 0)
    ci = lax.broadcasted_iota(jnp.int32, (n, n), 1)
    eye = (ri == ci).astype(F32)[None]
    inner = ((ri // 2) == (ci // 2))[None]
    p = eye - jnp.where(inner, a, 0.0)
    size = 2
    while size < n:
        size *= 2
        outer = ((ri // size) == (ci // size))[None]
        e = jnp.where(outer & jnp.logical_not(inner), a, 0.0)
        p = p - _bmm(p, _bmm(e, p))
        inner = outer
    return p


def _mixer_kernel(qr_ref, kr_ref, va_ref, ga_ref, qb_ref, kb_ref, vb_ref, gb_ref, ba_ref, zt_ref,
                  intra_ref, rc_ref, hpc_ref, hpr_ref, retg_ref, dng_ref, sret0_ref, sdn0_ref,
                  sret_all_ref, sdn_all_ref, mix_ref, sret_out, sdn_out, sret_s, sdn_s, *, bb, cb):
    del sret_all_ref, sdn_all_ref
    step = pl.program_id(1)
    ng = bb * HEADS
    state_shape = (ng, HEAD_DIM, HEAD_DIM)

    @pl.when(step == 0)
    def _():
        sret_s[...] = sret0_ref[...].reshape(state_shape)
        sdn_s[...] = sdn0_ref[...].reshape(state_shape)

    ri = lax.broadcasted_iota(jnp.int32, (CHUNK, CHUNK), 0)
    ci = lax.broadcasted_iota(jnp.int32, (CHUNK, CHUNK), 1)
    causal = ri >= ci
    strict = ri > ci
    ltri = causal.astype(BF16)
    utri = (ri <= ci).astype(BF16)
    dot = functools.partial(jnp.dot, preferred_element_type=F32)

    hpc = hpc_ref[...]
    hpr = hpr_ref[...]
    neg_a_col = -jnp.exp(hpc[0:1, :])
    neg_a_row = -jnp.exp(hpr[:, 0:1])
    dn_g = dng_ref[...]
    ret_g = retg_ref[...]
    heads = [slice(h * HEAD_DIM, (h + 1) * HEAD_DIM) for h in range(HEADS)]
    reps = cb * bb
    intra = jnp.concatenate([intra_ref[...]] * reps, axis=0)
    inject = jnp.concatenate([rc_ref[1]] * reps, axis=0)
    cross = jnp.concatenate([rc_ref[0]] * bb, axis=0)
    carry = jnp.concatenate([rc_ref[2, :, 0:1, :]] * bb, axis=0)
    ret_gain = jnp.stack([ret_g[:, hs] for hs in heads] * reps)

    def lanes(col, j):
        return jnp.broadcast_to(col[:, j:j + 1], (CHUNK, HEAD_DIM))

    def prepare(chunks):
        def heads_of(ref):
            return jnp.stack([ref[b, pl.ds(c * CHUNK, CHUNK), hs]
                              for c in chunks for b in range(bb) for hs in heads])

        beta, gc_c, gc_r = [], [], []
        for c in chunks:
            r0 = c * CHUNK
            for b in range(bb):
                ba = ba_ref[b, pl.ds(r0, CHUNK), :]
                zt = zt_ref[b, :, r0:r0 + CHUNK]
                beta_col = _sigmoid(ba)
                g_col = neg_a_col * _softplus(ba + hpc[1:2, :])
                g_row = neg_a_row * _softplus(zt + hpr[:, 1:2])
                gh, gm, gl = _split3(g_col)
                gc_col = dot(ltri, gh) + (dot(ltri, gm) + dot(ltri, gl))
                gh, gm, gl = _split3(g_row)
                gc_row = dot(gh, utri) + (dot(gm, utri) + dot(gl, utri))
                for h in range(HEADS):
                    beta.append(lanes(beta_col, h))
                    gc_c.append(lanes(gc_col, HEADS + h))
                    gc_r.append(gc_row[HEADS + h:HEADS + h + 1, :])

        q_r, k_r, v_r = heads_of(qr_ref), heads_of(kr_ref), heads_of(va_ref)
        o_intra = _bmm(_bmm_nt(q_r, k_r) * intra, v_r)
        s_inc = _bmm_tn(k_r, v_r * inject)

        q, k, v = heads_of(qb_ref), heads_of(kb_ref), heads_of(vb_ref)
        beta, gc_c = jnp.stack(beta), jnp.stack(gc_c)
        gc_last = gc_c[:, CHUNK - 1:CHUNK, :]
        e_gc = jnp.exp(gc_c)
        decay = jnp.exp(jnp.where(causal[None], gc_c[:, :, :CHUNK] - jnp.stack(gc_r), -jnp.inf))
        kbeta = k * beta
        kk = _bmm_nt(jnp.concatenate([kbeta, q], axis=1), k)
        a = jnp.where(strict[None], kk[:, :CHUNK] * decay, 0.0)
        tinv = _unit_lower_inverse(a)
        sol = _bmm(tinv, jnp.concatenate([v * beta, kbeta * e_gc], axis=2))
        return dict(
            q_r=q_r, o_intra=o_intra, s_inc=s_inc, attn=kk[:, CHUNK:] * decay,
            u=sol[:, :, :HEAD_DIM],
            wq=jnp.concatenate([sol[:, :, HEAD_DIM:], q * e_gc], axis=1),
            kg=k * jnp.exp(gc_last - gc_c), g_end=jnp.exp(gc_last),
            gate_a=heads_of(ga_ref), gate_b=heads_of(gb_ref))

    groups = [list(range(cb))]
    prepared = [prepare(chunks) for chunks in groups]

    s_ret = sret_s[...]
    s_dn = sdn_s[...]
    for chunks, pre in zip(groups, prepared):
        o_a, o_b = [], []
        for n in range(len(chunks)):
            sl = slice(n * ng, (n + 1) * ng)
            o_a.append(pre["o_intra"][sl] + _bmm(pre["q_r"][sl], s_ret) * cross)
            s_ret = s_ret * carry + pre["s_inc"][sl]
            ws = _bmm(pre["wq"][sl], s_dn)
            v_new = pre["u"][sl] - ws[:, :CHUNK]
            o_b.append(ws[:, CHUNK:] + _bmm(pre["attn"][sl], v_new))
            s_dn = s_dn * pre["g_end"][sl] + _bmm_tn(pre["kg"][sl], v_new)
        o = jnp.concatenate(o_a, axis=0)
        mu = jnp.mean(o, axis=-1, keepdims=True)
        oc = o - mu
        var = jnp.mean(oc * oc, axis=-1, keepdims=True)
        o_a = oc * lax.rsqrt(var + EPS) * ret_gain * pre["gate_a"]
        o_b = _rms(jnp.concatenate(o_b, axis=0), dn_g[None]) * pre["gate_b"]
        for n, c in enumerate(chunks):
            rows = pl.ds(c * CHUNK, CHUNK)
            for b in range(bb):
                for h, hs in enumerate(heads):
                    g = (n * bb + b) * HEADS + h
                    mix_ref[b, rows, hs] = o_a[g].astype(BF16)
                    mix_ref[b, rows, HEAD_W + h * HEAD_DIM:HEAD_W + (h + 1) * HEAD_DIM] = (
                        o_b[g].astype(BF16))
    sret_s[...] = s_ret
    sdn_s[...] = s_dn

    @pl.when(step == pl.num_programs(1) - 1)
    def _():
        sret_out[...] = sret_s[...].reshape(sret_out.shape)
        sdn_out[...] = sdn_s[...].reshape(sdn_out.shape)


def _mixer(qk, rest, zt, l, intra, rc, hpc, hpr, ret_g, dn_g, sret0, sdn0, l0, sret_all, sdn_all, bb,
           cb):
    bsz, t, _ = qk.shape
    blk = cb * CHUNK
    grid = (bsz // bb, t // blk)

    def cols(col):
        return pl.BlockSpec((bb, blk, HEAD_W), lambda i, j: (i, j, col))

    def const(shape):
        return pl.BlockSpec(shape, lambda i, j: (0,) * len(shape))

    def layer(shape):
        return pl.BlockSpec((None,) + shape, lambda i, j: (l,) + (0,) * len(shape))

    state_shape = (bb, HEADS, HEAD_DIM, HEAD_DIM)
    state_in = pl.BlockSpec((None,) + state_shape, lambda i, j: (l0, i, 0, 0, 0))
    state_out = pl.BlockSpec((None,) + state_shape, lambda i, j: (l, i, 0, 0, 0))
    in_place = pl.BlockSpec(memory_space=pl.ANY)
    in_specs = [cols(0), cols(1)] + [cols(c) for c in range(6)] + [
        pl.BlockSpec((bb, blk, LANE), lambda i, j: (i, j, 6 * HEAD_W // LANE)),
        pl.BlockSpec((bb, SUBLANE, blk), lambda i, j: (i, 0, j)),
        const((HEADS, CHUNK, CHUNK)),
        const((3, HEADS, CHUNK, LANE)),
        layer((SUBLANE, LANE)),
        layer((SUBLANE, LANE)),
        layer((1, HEAD_W)),
        layer((1, HEAD_DIM)),
        state_in, state_in, in_place, in_place,
    ]
    kern = functools.partial(_mixer_kernel, bb=bb, cb=cb)
    return pl.pallas_call(
        kern,
        grid=grid,
        in_specs=in_specs,
        out_specs=[
            pl.BlockSpec((bb, blk, 2 * HEAD_W), lambda i, j: (i, j, 0)),
            state_out, state_out,
        ],
        out_shape=[
            jax.ShapeDtypeStruct((bsz, t, 2 * HEAD_W), BF16),
            jax.ShapeDtypeStruct(sret_all.shape, F32),
            jax.ShapeDtypeStruct(sdn_all.shape, F32),
        ],
        input_output_aliases={len(in_specs) - 2: 1, len(in_specs) - 1: 2},
        scratch_shapes=[
            pltpu.VMEM((bb * HEADS, HEAD_DIM, HEAD_DIM), F32),
            pltpu.VMEM((bb * HEADS, HEAD_DIM, HEAD_DIM), F32),
        ],
        compiler_params=pltpu.CompilerParams(
            dimension_semantics=("arbitrary", "arbitrary"), vmem_limit_bytes=VMEM_LIMIT),
        name="mixer",
    )(qk, qk, *([rest] * 7), zt, intra, rc, hpc, hpr, ret_g, dn_g, sret0, sdn0, sret_all, sdn_all)


def _post_kernel(h_ref, mix_ref, p_ref, wout_ref, n2g_ref, wup_ref, wdown_ref, proj_ref,
                 pleg_ref, gatew_ref, gateb_ref, fing_ref, out_ref, acc_s, xn_s, *, final):
    j = pl.program_id(1)

    @pl.when(j == 0)
    def _():
        h1 = h_ref[...] + jnp.dot(mix_ref[...], wout_ref[...],
                                  preferred_element_type=F32)
        acc_s[...] = h1
        xn_s[...] = _rms(h1, n2g_ref[...]).astype(BF16)

    up = jnp.maximum(jnp.dot(xn_s[...], wup_ref[...], preferred_element_type=F32), 0.0)
    acc_s[...] += jnp.dot((up * up).astype(BF16), wdown_ref[...], preferred_element_type=F32)

    @pl.when(j == pl.num_programs(1) - 1)
    def _():
        h2 = acc_s[...]
        gate = _sigmoid(jnp.dot(_rms(h2, pleg_ref[...]).astype(BF16), gatew_ref[...],
                                preferred_element_type=F32) + gateb_ref[...])
        h3 = h2 + jnp.dot(p_ref[...].astype(BF16), proj_ref[...],
                          preferred_element_type=F32) * gate
        if final:
            h3 = _rms(h3, fing_ref[...])
        out_ref[...] = h3


def _post(h, mix, p, l, w_out, n2g, w_up, w_down, proj, pleg, gate_w, gate_b, fin_g, final):
    m = h.shape[0]
    tm, tf = POST_ROWS, POST_FF_COLS

    def row(width):
        return pl.BlockSpec((tm, width), lambda i, j: (i, 0))

    def layer(shape):
        return pl.BlockSpec((None,) + shape, lambda i, j: (l,) + (0,) * len(shape),
                            pipeline_mode=pl.Buffered(1))

    return pl.pallas_call(
        functools.partial(_post_kernel, final=final),
        grid=(m // tm, D_FF // tf),
        in_specs=[
            row(D_MODEL), row(2 * HEAD_W),
            pl.BlockSpec((None, tm, PLE_DIM), lambda i, j: (l, i, 0)),
            layer((2 * HEAD_W, D_MODEL)),
            layer((1, D_MODEL)),
            pl.BlockSpec((None, D_MODEL, tf), lambda i, j: (l, 0, j)),
            pl.BlockSpec((None, tf, D_MODEL), lambda i, j: (l, j, 0)),
            layer((PLE_DIM, D_MODEL)),
            layer((1, D_MODEL)),
            layer((D_MODEL, D_MODEL)),
            layer((1, D_MODEL)),
            pl.BlockSpec((1, D_MODEL), lambda i, j: (0, 0)),
        ],
        out_specs=row(D_MODEL),
        out_shape=jax.ShapeDtypeStruct((m, D_MODEL), F32),
        scratch_shapes=[pltpu.VMEM((tm, D_MODEL), F32), pltpu.VMEM((tm, D_MODEL), BF16)],
        compiler_params=pltpu.CompilerParams(
            dimension_semantics=("arbitrary", "arbitrary"), vmem_limit_bytes=VMEM_LIMIT),
        name="post",
    )(h, mix, p, w_out, n2g, w_up, w_down, proj, pleg, gate_w, gate_b, fin_g)


def _rope_tables(pos0, t):
    half = HEAD_DIM // 2
    step = min(t, LANE)
    inv = ROPE_BASE ** (-jnp.arange(half, dtype=F32) / half)
    inv = jnp.concatenate([inv, inv])
    sign = jnp.concatenate([-jnp.ones((half,), F32), jnp.ones((half,), F32)])
    ang_hi = (pos0 + step * jnp.arange(t // step)).astype(F32)[:, None] * inv[None, :]
    ang_lo = jnp.arange(step).astype(F32)[:, None] * inv[None, :]
    ch, sh = jnp.cos(ang_hi)[:, None, :], jnp.sin(ang_hi)[:, None, :]
    cl, sl = jnp.cos(ang_lo)[None, :, :], jnp.sin(ang_lo)[None, :, :]
    cos = (ch * cl - sh * sl).reshape(t, HEAD_DIM)
    sin = ((sh * cl + ch * sl) * sign).reshape(t, HEAD_DIM)
    return cos, sin


def _retention_tables():
    log_gamma = jnp.log(1.0 - 2.0 ** (-5.0 - jnp.arange(HEADS, dtype=F32)))
    idx = jnp.arange(CHUNK, dtype=F32)
    intra = jnp.exp(jnp.abs(idx[:, None] - idx[None, :])[None] * log_gamma[:, None, None])
    cross = jnp.exp((idx + 1.0)[None, :] * log_gamma[:, None])
    inject = jnp.exp((CHUNK - 1.0 - idx)[None, :] * log_gamma[:, None])
    carry = jnp.exp(CHUNK * log_gamma)
    carry = jnp.broadcast_to(carry[:, None], (HEADS, CHUNK))
    rc = jnp.broadcast_to(jnp.stack([cross, inject, carry])[..., None], (3, HEADS, CHUNK, LANE))
    return intra, rc


def _trunk(x, p, conv0, sret0, sdn0, zero_state_layer, pos0, wts, intra, rc, bb, cb, nb_in, tm_in):
    bsz, t, _ = x.shape
    m = bsz * t
    depth = wts["w_in"].shape[0]
    cos_t, sin_t = _rope_tables(pos0, t)
    p = p.reshape(depth, m, PLE_DIM)
    h = x.reshape(m, D_MODEL)
    rets = jnp.zeros((depth, bsz, HEADS, HEAD_DIM, HEAD_DIM), F32)
    dns = jnp.zeros((depth, bsz, HEADS, HEAD_DIM, HEAD_DIM), F32)
    convs = jnp.zeros((depth, bsz, CONV_W - 1, CONV_CH), F32)
    for l in range(depth):
        l0 = 0 if zero_state_layer else l
        qk, rest, zt, convs = _in_proj(h.reshape(bsz, t, D_MODEL), l, wts["n1g"], wts["w_in"],
                                       wts["w_ba"], cos_t, sin_t, wts["conv_w"],
                                       conv0, l0, convs, nb_in, tm_in)
        mix, rets, dns = _mixer(qk, rest, zt, l, intra, rc, wts["hpc"], wts["hpr"], wts["ret_g"],
                                wts["dn_g"], sret0, sdn0, l0, rets, dns, bb, cb)
        h = _post(h, mix.reshape(m, 2 * HEAD_W), p, l, wts["w_out"], wts["n2g"], wts["w_up"],
                  wts["w_down"], wts["proj"], wts["pleg"], wts["gate_w"], wts["gate_b"],
                  wts["fin_g"], l == depth - 1)
    return h.reshape(bsz, t, D_MODEL), rets, dns, convs


def kernel(x_prompt, x_sample, p_prompt, p_sample, state_ret, state_delta, state_conv, norm1_g, w_in, conv_w, dt_bias, a_log, ret_norm_g, dn_norm_g, w_out, norm2_g, w_up, w_down, ple_proj, ple_norm_g, ple_gate_w, ple_gate_b, final_norm_g):
    depth = w_in.shape[0]
    w_in_b = w_in[:, :, :Z_MAIN].astype(BF16)
    w_gate_cols = w_in[:, :, Z_MAIN:].astype(BF16)
    hpc = jnp.zeros((depth, SUBLANE, LANE), F32)
    hpc = hpc.at[:, 0, HEADS:N_GATE].set(a_log).at[:, 1, HEADS:N_GATE].set(dt_bias)
    hpr = jnp.zeros((depth, SUBLANE, LANE), F32)
    hpr = hpr.at[:, HEADS:N_GATE, 0].set(a_log).at[:, HEADS:N_GATE, 1].set(dt_bias)
    wts = dict(
        n1g=norm1_g[:, None, :], w_in=w_in_b,
        w_ba=jnp.pad(w_gate_cols, ((0, 0), (0, 0), (0, LANE - N_GATE))),
        conv_w=conv_w, hpc=hpc, hpr=hpr,
        ret_g=ret_norm_g[:, None, :], dn_g=dn_norm_g[:, None, :],
        w_out=w_out.astype(BF16), n2g=norm2_g[:, None, :],
        w_up=w_up.astype(BF16), w_down=w_down.astype(BF16),
        proj=ple_proj.astype(BF16), pleg=ple_norm_g[:, None, :],
        gate_w=ple_gate_w.astype(BF16), gate_b=ple_gate_b[:, None, :],
        fin_g=final_norm_g[None, :])
    intra, rc = _retention_tables()
    bp = x_prompt.shape[0]
    zero_state = jnp.zeros((1, bp, HEADS, HEAD_DIM, HEAD_DIM), F32)
    zero_conv = jnp.zeros((1, bp, CONV_W - 1, CONV_CH), F32)
    y_p, ret_p, dn_p, conv_p = _trunk(x_prompt, p_prompt, zero_conv, zero_state, zero_state, True, 0,
                                      wts, intra, rc, bb=2, cb=4, nb_in=1, tm_in=1024)
    y_s, ret_s, dn_s, conv_s = _trunk(x_sample, p_sample, state_conv, state_ret, state_delta, False,
                                      PAST_LEN, wts, intra, rc, bb=8, cb=1, nb_in=8, tm_in=CHUNK)
    return (y_p, y_s, ret_p, dn_p, conv_p, ret_s, dn_s, conv_s)
```

```python
import functools

import jax
import jax.numpy as jnp
from jax import lax
from jax.experimental import pallas as pl
from jax.experimental.pallas import tpu as pltpu

F32 = jnp.float32
BF16 = jnp.bfloat16

D_MODEL = 1024
CHUNK = 64
HEADS = 4
HEAD_DIM = 128
HEAD_W = HEADS * HEAD_DIM
PLE_DIM = 256
D_FF = 4 * D_MODEL
CONV_W = 4
CONV_CH = 3 * HEAD_W
ROPE_BASE = 10000.0
EPS = 1e-6
PAST_LEN = 4096
Z_MAIN = 8 * HEAD_W
N_GATE = 2 * HEADS
LANE = 128
SUBLANE = 8
REST_W = 6 * HEAD_W + LANE
VMEM_LIMIT = 56 * 1024 * 1024
POST_ROWS = 1024
POST_FF_COLS = 1024


def _split3(x):
    hi = x.astype(BF16)
    r = x - hi.astype(F32)
    mid = r.astype(BF16)
    lo = (r - mid.astype(F32)).astype(BF16)
    return hi, mid, lo


def _rms(x, g):
    return x * lax.rsqrt(jnp.mean(x * x, axis=-1, keepdims=True) + EPS) * g


def _softplus(x):
    return jnp.maximum(x, 0.0) + jnp.log1p(jnp.exp(-jnp.abs(x)))


def _sigmoid(x):
    return 0.5 + 0.5 * jnp.tanh(0.5 * x)


def _silu(x):
    h = 0.5 * x
    return h + h * jnp.tanh(h)


def _in_proj_kernel(x_ref, g_ref, w_ref, wba_ref, cos_ref, sin_ref, convw_ref, conv0_ref,
                    conv_all_ref, qk_ref, rest_ref, zt_ref, convnew_ref, hist_s):
    del conv_all_ref
    j = pl.program_id(1)
    nb, tm, _ = x_ref.shape
    xn = _rms(x_ref[...].reshape(nb * tm, D_MODEL), g_ref[...]).astype(BF16)
    pair_w = 2 * HEAD_DIM

    def proj(col, pair):
        lo = col * HEAD_W + pair * pair_w
        return jnp.dot(xn, w_ref[:, lo:lo + pair_w], preferred_element_type=F32)

    @pl.when(j == 0)
    def _():
        hist_s[...] = jnp.zeros_like(hist_s)
        hist_s[:, SUBLANE - (CONV_W - 1):SUBLANE, :] = conv0_ref[...]

    def rope(col, h, z, r):
        scale = 1.0 if col == 0 else HEAD_DIM ** -0.5
        out = (z * cos_ref[...] + pltpu.roll(z, HEAD_DIM // 2, 1) * sin_ref[...]) * scale
        lo = col * HEAD_W + h * HEAD_DIM
        qk_ref[r, :, lo:lo + HEAD_DIM] = out.astype(BF16)

    def conv(col, h, z, r):
        n = col - 4
        cs = slice(n * HEAD_W + h * HEAD_DIM, n * HEAD_W + (h + 1) * HEAD_DIM)
        xp = jnp.concatenate([hist_s[r, :, cs], z], axis=0)
        acc = z * convw_ref[CONV_W - 1:CONV_W, cs]
        for i in range(CONV_W - 1):
            acc = acc + pltpu.roll(xp, CONV_W - 1 - i, 0)[SUBLANE:] * convw_ref[i:i + 1, cs]
        hist_s[r, :, cs] = z[tm - SUBLANE:, :]
        c = _silu(acc)
        if n == 0:
            c = c * lax.rsqrt(jnp.sum(c * c, axis=-1, keepdims=True) + EPS) * (HEAD_DIM ** -0.5)
        elif n == 1:
            c = c * lax.rsqrt(jnp.sum(c * c, axis=-1, keepdims=True) + EPS)
        lo = (col - 2) * HEAD_W + h * HEAD_DIM
        rest_ref[r, :, lo:lo + HEAD_DIM] = c

    def plain(col, h, z, r):
        rest_ref[r, :, h * HEAD_DIM:(h + 1) * HEAD_DIM] = z

    def gate(col, h, z, r):
        lo = (HEAD_W if col == 3 else 5 * HEAD_W) + h * HEAD_DIM
        rest_ref[r, :, lo:lo + HEAD_DIM] = _silu(z)

    finishers = {0: rope, 1: rope, 2: plain, 3: gate, 4: conv, 5: conv, 6: conv, 7: gate}

    def finish(col, pair, z):
        for hh in range(2):
            for r in range(nb):
                finishers[col](col, 2 * pair + hh,
                               z[r * tm:(r + 1) * tm, hh * HEAD_DIM:(hh + 1) * HEAD_DIM], r)

    order = ((4, 0), (0, 0), (4, 1), (0, 1), (5, 0), (1, 0), (5, 1), (1, 1),
             (6, 0), (3, 0), (6, 1), (3, 1), (7, 0), (7, 1), (2, 0), (2, 1))
    pending = None
    for col, pair in order:
        z = proj(col, pair)
        if pending is not None:
            finish(*pending)
        pending = (col, pair, z)
    finish(*pending)

    ba = jnp.dot(xn, wba_ref[...], preferred_element_type=F32)
    zt = ba.T[:SUBLANE]
    for r in range(nb):
        rest_ref[r, :, 6 * HEAD_W:] = ba[r * tm:(r + 1) * tm]
        zt_ref[r] = zt[:, r * tm:(r + 1) * tm]

    @pl.when(j == pl.num_programs(1) - 1)
    def _():
        convnew_ref[...] = hist_s[:, SUBLANE - (CONV_W - 1):SUBLANE, :]


def _in_proj(x, l, n1g, w, wba, cos_t, sin_t, conv_w, conv0, l0, conv_all, nb, tm):
    bsz, t, _ = x.shape

    def layer(shape):
        return pl.BlockSpec((None,) + shape, lambda i, j: (l,) + (0,) * len(shape))

    in_specs = [
        pl.BlockSpec((nb, tm, D_MODEL), lambda i, j: (i, j, 0)),
        layer((1, D_MODEL)),
        pl.BlockSpec((D_MODEL, Z_MAIN), lambda i, j: (l, 0)),
        layer((D_MODEL, LANE)),
        pl.BlockSpec((tm, LANE), lambda i, j: (j, 0)),
        pl.BlockSpec((tm, LANE), lambda i, j: (j, 0)),
        layer((CONV_W, CONV_CH)),
        pl.BlockSpec((None, nb, CONV_W - 1, CONV_CH), lambda i, j: (l0, i, 0, 0)),
        pl.BlockSpec(memory_space=pl.ANY),
    ]
    return pl.pallas_call(
        _in_proj_kernel,
        grid=(bsz // nb, t // tm),
        in_specs=in_specs,
        out_specs=[
            pl.BlockSpec((nb, tm, 2 * HEAD_W), lambda i, j: (i, j, 0)),
            pl.BlockSpec((nb, tm, REST_W), lambda i, j: (i, j, 0)),
            pl.BlockSpec((nb, SUBLANE, tm), lambda i, j: (i, 0, j)),
            pl.BlockSpec((None, nb, CONV_W - 1, CONV_CH), lambda i, j: (l, i, 0, 0)),
        ],
        out_shape=[
            jax.ShapeDtypeStruct((bsz, t, 2 * HEAD_W), BF16),
            jax.ShapeDtypeStruct((bsz, t, REST_W), F32),
            jax.ShapeDtypeStruct((bsz, SUBLANE, t), F32),
            jax.ShapeDtypeStruct(conv_all.shape, F32),
        ],
        input_output_aliases={len(in_specs) - 1: 3},
        scratch_shapes=[pltpu.VMEM((nb, SUBLANE, CONV_CH), F32)],
        compiler_params=pltpu.CompilerParams(
            dimension_semantics=("arbitrary", "arbitrary"), vmem_limit_bytes=VMEM_LIMIT),
        name="in_proj",
    )(x, n1g, w, wba, cos_t, sin_t, conv_w, conv0, conv_all)


def _bmm(a, b):
    return jnp.einsum("gmk,gkn->gmn", a.astype(BF16), b.astype(BF16), preferred_element_type=F32)


def _bmm_nt(a, b):
    return jnp.einsum("gmk,gnk->gmn", a.astype(BF16), b.astype(BF16), preferred_element_type=F32)


def _bmm_tn(a, b):
    return jnp.einsum("gkm,gkn->gmn", a.astype(BF16), b.astype(BF16), preferred_element_type=F32)


def _unit_lower_inverse(a):
    n = a.shape[-1]
    ri = lax.broadcasted_iota(jnp.int32, (n, n), 0)
    ci = lax.broadcasted_iota(jnp.int32, (n, n), 1)
    eye = (ri == ci).astype(F32)[None]
    inner = ((ri // 2) == (ci // 2))[None]
    p = eye - jnp.where(inner, a, 0.0)
    size = 2
    while size < n:
        size *= 2
        outer = ((ri // size) == (ci // size))[None]
        e = jnp.where(outer & jnp.logical_not(inner), a, 0.0)
        p = p - _bmm(p, _bmm(e, p))
        inner = outer
    return p


def _mixer_kernel(qr_ref, kr_ref, va_ref, ga_ref, qb_ref, kb_ref, vb_ref, gb_ref, ba_ref, zt_ref,
                  intra_ref, rc_ref, hpc_ref, hpr_ref, retg_ref, dng_ref, sret0_ref, sdn0_ref,
                  sret_all_ref, sdn_all_ref, mix_ref, sret_out, sdn_out, sret_s, sdn_s, *, bb, cb):
    del sret_all_ref, sdn_all_ref
    step = pl.program_id(1)
    ng = bb * HEADS
    state_shape = (ng, HEAD_DIM, HEAD_DIM)

    @pl.when(step == 0)
    def _():
        sret_s[...] = sret0_ref[...].reshape(state_shape)
        sdn_s[...] = sdn0_ref[...].reshape(state_shape)

    ri = lax.broadcasted_iota(jnp.int32, (CHUNK, CHUNK), 0)
    ci = lax.broadcasted_iota(jnp.int32, (CHUNK, CHUNK), 1)
    causal = ri >= ci
    strict = ri > ci
    ltri = causal.astype(BF16)
    utri = (ri <= ci).astype(BF16)
    dot = functools.partial(jnp.dot, preferred_element_type=F32)

    hpc = hpc_ref[...]
    hpr = hpr_ref[...]
    neg_a_col = -jnp.exp(hpc[0:1, :])
    neg_a_row = -jnp.exp(hpr[:, 0:1])
    dn_g = dng_ref[...]
    ret_g = retg_ref[...]
    heads = [slice(h * HEAD_DIM, (h + 1) * HEAD_DIM) for h in range(HEADS)]
    reps = cb * bb
    intra = jnp.concatenate([intra_ref[...]] * reps, axis=0)
    inject = jnp.concatenate([rc_ref[1]] * reps, axis=0)
    cross = jnp.concatenate([rc_ref[0]] * bb, axis=0)
    carry = jnp.concatenate([rc_ref[2, :, 0:1, :]] * bb, axis=0)
    ret_gain = jnp.stack([ret_g[:, hs] for hs in heads] * reps)

    def lanes(col, j):
        return jnp.broadcast_to(col[:, j:j + 1], (CHUNK, HEAD_DIM))

    def prepare(chunks):
        def heads_of(ref):
            return jnp.stack([ref[b, pl.ds(c * CHUNK, CHUNK), hs]
                              for c in chunks for b in range(bb) for hs in heads])

        beta, gc_c, gc_r = [], [], []
        for c in chunks:
            r0 = c * CHUNK
            for b in range(bb):
                ba = ba_ref[b, pl.ds(r0, CHUNK), :]
                zt = zt_ref[b, :, r0:r0 + CHUNK]
                beta_col = _sigmoid(ba)
                g_col = neg_a_col * _softplus(ba + hpc[1:2, :])
                g_row = neg_a_row * _softplus(zt + hpr[:, 1:2])
                gh, gm, gl = _split3(g_col)
                gc_col = dot(ltri, gh) + (dot(ltri, gm) + dot(ltri, gl))
                gh, gm, gl = _split3(g_row)
                gc_row = dot(gh, utri) + (dot(gm, utri) + dot(gl, utri))
                for h in range(HEADS):
                    beta.append(lanes(beta_col, h))
                    gc_c.append(lanes(gc_col, HEADS + h))
                    gc_r.append(gc_row[HEADS + h:HEADS + h + 1, :])

        q_r, k_r, v_r = heads_of(qr_ref), heads_of(kr_ref), heads_of(va_ref)
        o_intra = _bmm(_bmm_nt(q_r, k_r) * intra, v_r)
        s_inc = _bmm_tn(k_r, v_r * inject)

        q, k, v = heads_of(qb_ref), heads_of(kb_ref), heads_of(vb_ref)
        beta, gc_c = jnp.stack(beta), jnp.stack(gc_c)
        gc_last = gc_c[:, CHUNK - 1:CHUNK, :]
        e_gc = jnp.exp(gc_c)
        decay = jnp.exp(jnp.where(causal[None], gc_c[:, :, :CHUNK] - jnp.stack(gc_r), -jnp.inf))
        kbeta = k * beta
        kk = _bmm_nt(jnp.concatenate([kbeta, q], axis=1), k)
        a = jnp.where(strict[None], kk[:, :CHUNK] * decay, 0.0)
        tinv = _unit_lower_inverse(a)
        sol = _bmm(tinv, jnp.concatenate([v * beta, kbeta * e_gc], axis=2))
        return dict(
            q_r=q_r, o_intra=o_intra, s_inc=s_inc, attn=kk[:, CHUNK:] * decay,
            u=sol[:, :, :HEAD_DIM],
            wq=jnp.concatenate([sol[:, :, HEAD_DIM:], q * e_gc], axis=1),
            kg=k * jnp.exp(gc_last - gc_c), g_end=jnp.exp(gc_last),
            gate_a=heads_of(ga_ref), gate_b=heads_of(gb_ref))

    groups = [list(range(cb))]
    prepared = [prepare(chunks) for chunks in groups]

    s_ret = sret_s[...]
    s_dn = sdn_s[...]
    for chunks, pre in zip(groups, prepared):
        o_a, o_b = [], []
        for n in range(len(chunks)):
            sl = slice(n * ng, (n + 1) * ng)
            o_a.append(pre["o_intra"][sl] + _bmm(pre["q_r"][sl], s_ret) * cross)
            s_ret = s_ret * carry + pre["s_inc"][sl]
            ws = _bmm(pre["wq"][sl], s_dn)
            v_new = pre["u"][sl] - ws[:, :CHUNK]
            o_b.append(ws[:, CHUNK:] + _bmm(pre["attn"][sl], v_new))
            s_dn = s_dn * pre["g_end"][sl] + _bmm_tn(pre["kg"][sl], v_new)
        o = jnp.concatenate(o_a, axis=0)
        mu = jnp.mean(o, axis=-1, keepdims=True)
        oc = o - mu
        var = jnp.mean(oc * oc, axis=-1, keepdims=True)
        o_a = oc * lax.rsqrt(var + EPS) * ret_gain * pre["gate_a"]
        o_b = _rms(jnp.concatenate(o_b, axis=0), dn_g[None]) * pre["gate_b"]
        for n, c in enumerate(chunks):
            rows = pl.ds(c * CHUNK, CHUNK)
            for b in range(bb):
                for h, hs in enumerate(heads):
                    g = (n * bb + b) * HEADS + h
                    mix_ref[b, rows, hs] = o_a[g].astype(BF16)
                    mix_ref[b, rows, HEAD_W + h * HEAD_DIM:HEAD_W + (h + 1) * HEAD_DIM] = (
                        o_b[g].astype(BF16))
    sret_s[...] = s_ret
    sdn_s[...] = s_dn

    @pl.when(step == pl.num_programs(1) - 1)
    def _():
        sret_out[...] = sret_s[...].reshape(sret_out.shape)
        sdn_out[...] = sdn_s[...].reshape(sdn_out.shape)


def _mixer(qk, rest, zt, l, intra, rc, hpc, hpr, ret_g, dn_g, sret0, sdn0, l0, sret_all, sdn_all, bb,
           cb):
    bsz, t, _ = qk.shape
    blk = cb * CHUNK
    grid = (bsz // bb, t // blk)

    def cols(col):
        return pl.BlockSpec((bb, blk, HEAD_W), lambda i, j: (i, j, col))

    def const(shape):
        return pl.BlockSpec(shape, lambda i, j: (0,) * len(shape))

    def layer(shape):
        return pl.BlockSpec((None,) + shape, lambda i, j: (l,) + (0,) * len(shape))

    state_shape = (bb, HEADS, HEAD_DIM, HEAD_DIM)
    state_in = pl.BlockSpec((None,) + state_shape, lambda i, j: (l0, i, 0, 0, 0))
    state_out = pl.BlockSpec((None,) + state_shape, lambda i, j: (l, i, 0, 0, 0))
    in_place = pl.BlockSpec(memory_space=pl.ANY)
    in_specs = [cols(0), cols(1)] + [cols(c) for c in range(6)] + [
        pl.BlockSpec((bb, blk, LANE), lambda i, j: (i, j, 6 * HEAD_W // LANE)),
        pl.BlockSpec((bb, SUBLANE, blk), lambda i, j: (i, 0, j)),
        const((HEADS, CHUNK, CHUNK)),
        const((3, HEADS, CHUNK, LANE)),
        layer((SUBLANE, LANE)),
        layer((SUBLANE, LANE)),
        layer((1, HEAD_W)),
        layer((1, HEAD_DIM)),
        state_in, state_in, in_place, in_place,
    ]
    kern = functools.partial(_mixer_kernel, bb=bb, cb=cb)
    return pl.pallas_call(
        kern,
        grid=grid,
        in_specs=in_specs,
        out_specs=[
            pl.BlockSpec((bb, blk, 2 * HEAD_W), lambda i, j: (i, j, 0)),
            state_out, state_out,
        ],
        out_shape=[
            jax.ShapeDtypeStruct((bsz, t, 2 * HEAD_W), BF16),
            jax.ShapeDtypeStruct(sret_all.shape, F32),
            jax.ShapeDtypeStruct(sdn_all.shape, F32),
        ],
        input_output_aliases={len(in_specs) - 2: 1, len(in_specs) - 1: 2},
        scratch_shapes=[
            pltpu.VMEM((bb * HEADS, HEAD_DIM, HEAD_DIM), F32),
            pltpu.VMEM((bb * HEADS, HEAD_DIM, HEAD_DIM), F32),
        ],
        compiler_params=pltpu.CompilerParams(
            dimension_semantics=("arbitrary", "arbitrary"), vmem_limit_bytes=VMEM_LIMIT),
        name="mixer",
    )(qk, qk, *([rest] * 7), zt, intra, rc, hpc, hpr, ret_g, dn_g, sret0, sdn0, sret_all, sdn_all)


def _post_kernel(h_ref, mix_ref, p_ref, wout_ref, n2g_ref, wup_ref, wdown_ref, proj_ref,
                 pleg_ref, gatew_ref, gateb_ref, fing_ref, out_ref, acc_s, xn_s, *, final):
    j = pl.program_id(1)

    @pl.when(j == 0)
    def _():
        h1 = h_ref[...] + jnp.dot(mix_ref[...], wout_ref[...],
                                  preferred_element_type=F32)
        acc_s[...] = h1
        xn_s[...] = _rms(h1, n2g_ref[...]).astype(BF16)

    up = jnp.maximum(jnp.dot(xn_s[...], wup_ref[...], preferred_element_type=F32), 0.0)
    acc_s[...] += jnp.dot((up * up).astype(BF16), wdown_ref[...], preferred_element_type=F32)

    @pl.when(j == pl.num_programs(1) - 1)
    def _():
        h2 = acc_s[...]
        gate = _sigmoid(jnp.dot(_rms(h2, pleg_ref[...]).astype(BF16), gatew_ref[...],
                                preferred_element_type=F32) + gateb_ref[...])
        h3 = h2 + jnp.dot(p_ref[...].astype(BF16), proj_ref[...],
                          preferred_element_type=F32) * gate
        if final:
            h3 = _rms(h3, fing_ref[...])
        out_ref[...] = h3


def _post(h, mix, p, l, w_out, n2g, w_up, w_down, proj, pleg, gate_w, gate_b, fin_g, final):
    m = h.shape[0]
    tm, tf = POST_ROWS, POST_FF_COLS

    def row(width):
        return pl.BlockSpec((tm, width), lambda i, j: (i, 0))

    def layer(shape):
        return pl.BlockSpec((None,) + shape, lambda i, j: (l,) + (0,) * len(shape),
                            pipeline_mode=pl.Buffered(1))

    return pl.pallas_call(
        functools.partial(_post_kernel, final=final),
        grid=(m // tm, D_FF // tf),
        in_specs=[
            row(D_MODEL), row(2 * HEAD_W),
            pl.BlockSpec((None, tm, PLE_DIM), lambda i, j: (l, i, 0)),
            layer((2 * HEAD_W, D_MODEL)),
            layer((1, D_MODEL)),
            pl.BlockSpec((None, D_MODEL, tf), lambda i, j: (l, 0, j)),
            pl.BlockSpec((None, tf, D_MODEL), lambda i, j: (l, j, 0)),
            layer((PLE_DIM, D_MODEL)),
            layer((1, D_MODEL)),
            layer((D_MODEL, D_MODEL)),
            layer((1, D_MODEL)),
            pl.BlockSpec((1, D_MODEL), lambda i, j: (0, 0)),
        ],
        out_specs=row(D_MODEL),
        out_shape=jax.ShapeDtypeStruct((m, D_MODEL), F32),
        scratch_shapes=[pltpu.VMEM((tm, D_MODEL), F32), pltpu.VMEM((tm, D_MODEL), BF16)],
        compiler_params=pltpu.CompilerParams(
            dimension_semantics=("arbitrary", "arbitrary"), vmem_limit_bytes=VMEM_LIMIT),
        name="post",
    )(h, mix, p, w_out, n2g, w_up, w_down, proj, pleg, gate_w, gate_b, fin_g)


def _rope_tables(pos0, t):
    half = HEAD_DIM // 2
    step = min(t, LANE)
    inv = ROPE_BASE ** (-jnp.arange(half, dtype=F32) / half)
    inv = jnp.concatenate([inv, inv])
    sign = jnp.concatenate([-jnp.ones((half,), F32), jnp.ones((half,), F32)])
    ang_hi = (pos0 + step * jnp.arange(t // step)).astype(F32)[:, None] * inv[None, :]
    ang_lo = jnp.arange(step).astype(F32)[:, None] * inv[None, :]
    ch, sh = jnp.cos(ang_hi)[:, None, :], jnp.sin(ang_hi)[:, None, :]
    cl, sl = jnp.cos(ang_lo)[None, :, :], jnp.sin(ang_lo)[None, :, :]
    cos = (ch * cl - sh * sl).reshape(t, HEAD_DIM)
    sin = ((sh * cl + ch * sl) * sign).reshape(t, HEAD_DIM)
    return cos, sin


def _retention_tables():
    log_gamma = jnp.log(1.0 - 2.0 ** (-5.0 - jnp.arange(HEADS, dtype=F32)))
    idx = jnp.arange(CHUNK, dtype=F32)
    intra = jnp.exp(jnp.abs(idx[:, None] - idx[None, :])[None] * log_gamma[:, None, None])
    cross = jnp.exp((idx + 1.0)[None, :] * log_gamma[:, None])
    inject = jnp.exp((CHUNK - 1.0 - idx)[None, :] * log_gamma[:, None])
    carry = jnp.exp(CHUNK * log_gamma)
    carry = jnp.broadcast_to(carry[:, None], (HEADS, CHUNK))
    rc = jnp.broadcast_to(jnp.stack([cross, inject, carry])[..., None], (3, HEADS, CHUNK, LANE))
    return intra, rc


def _trunk(x, p, conv0, sret0, sdn0, zero_state_layer, pos0, wts, intra, rc, bb, cb, nb_in, tm_in):
    bsz, t, _ = x.shape
    m = bsz * t
    depth = wts["w_out"].shape[0]
    cos_t, sin_t = _rope_tables(pos0, t)
    p = p.reshape(depth, m, PLE_DIM)
    h = x.reshape(m, D_MODEL)
    rets = jnp.zeros((depth, bsz, HEADS, HEAD_DIM, HEAD_DIM), F32)
    dns = jnp.zeros((depth, bsz, HEADS, HEAD_DIM, HEAD_DIM), F32)
    convs = jnp.zeros((depth, bsz, CONV_W - 1, CONV_CH), F32)
    for l in range(depth):
        l0 = 0 if zero_state_layer else l
        qk, rest, zt, convs = _in_proj(h.reshape(bsz, t, D_MODEL), l, wts["n1g"], wts["w_in"],
                                       wts["w_ba"], cos_t, sin_t, wts["conv_w"],
                                       conv0, l0, convs, nb_in, tm_in)
        mix, rets, dns = _mixer(qk, rest, zt, l, intra, rc, wts["hpc"], wts["hpr"], wts["ret_g"],
                                wts["dn_g"], sret0, sdn0, l0, rets, dns, bb, cb)
        h = _post(h, mix.reshape(m, 2 * HEAD_W), p, l, wts["w_out"], wts["n2g"], wts["w_up"],
                  wts["w_down"], wts["proj"], wts["pleg"], wts["gate_w"], wts["gate_b"],
                  wts["fin_g"], l == depth - 1)
    return h.reshape(bsz, t, D_MODEL), rets, dns, convs


def kernel(x_prompt, x_sample, p_prompt, p_sample, state_ret, state_delta, state_conv, norm1_g, w_in, conv_w, dt_bias, a_log, ret_norm_g, dn_norm_g, w_out, norm2_g, w_up, w_down, ple_proj, ple_norm_g, ple_gate_w, ple_gate_b, final_norm_g):
    depth = w_in.shape[0]
    w_in_b = w_in[:, :, :Z_MAIN].astype(BF16).reshape(depth * D_MODEL, Z_MAIN)
    w_gate_cols = w_in[:, :, Z_MAIN:].astype(BF16)
    hpc = jnp.zeros((depth, SUBLANE, LANE), F32)
    hpc = hpc.at[:, 0, HEADS:N_GATE].set(a_log).at[:, 1, HEADS:N_GATE].set(dt_bias)
    hpr = jnp.zeros((depth, SUBLANE, LANE), F32)
    hpr = hpr.at[:, HEADS:N_GATE, 0].set(a_log).at[:, HEADS:N_GATE, 1].set(dt_bias)
    wts = dict(
        n1g=norm1_g[:, None, :], w_in=w_in_b,
        w_ba=jnp.pad(w_gate_cols, ((0, 0), (0, 0), (0, LANE - N_GATE))),
        conv_w=conv_w, hpc=hpc, hpr=hpr,
        ret_g=ret_norm_g[:, None, :], dn_g=dn_norm_g[:, None, :],
        w_out=w_out.astype(BF16), n2g=norm2_g[:, None, :],
        w_up=w_up.astype(BF16), w_down=w_down.astype(BF16),
        proj=ple_proj.astype(BF16), pleg=ple_norm_g[:, None, :],
        gate_w=ple_gate_w.astype(BF16), gate_b=ple_gate_b[:, None, :],
        fin_g=final_norm_g[None, :])
    intra, rc = _retention_tables()
    bp = x_prompt.shape[0]
    zero_state = jnp.zeros((1, bp, HEADS, HEAD_DIM, HEAD_DIM), F32)
    zero_conv = jnp.zeros((1, bp, CONV_W - 1, CONV_CH), F32)
    y_p, ret_p, dn_p, conv_p = _trunk(x_prompt, p_prompt, zero_conv, zero_state, zero_state, True, 0,
                                      wts, intra, rc, bb=2, cb=4, nb_in=1, tm_in=1024)
    y_s, ret_s, dn_s, conv_s = _trunk(x_sample, p_sample, state_conv, state_ret, state_delta, False,
                                      PAST_LEN, wts, intra, rc, bb=8, cb=1, nb_in=8, tm_in=CHUNK)
    return (y_p, y_s, ret_p, dn_p, conv_p, ret_s, dn_s, conv_s)
```

```python
import functools

import jax
import jax.numpy as jnp
from jax import lax
from jax.experimental import pallas as pl
from jax.experimental.pallas import tpu as pltpu

F32 = jnp.float32
BF16 = jnp.bfloat16

D_MODEL = 1024
CHUNK = 64
HEADS = 4
HEAD_DIM = 128
HEAD_W = HEADS * HEAD_DIM
PLE_DIM = 256
D_FF = 4 * D_MODEL
CONV_W = 4
CONV_CH = 3 * HEAD_W
ROPE_BASE = 10000.0
EPS = 1e-6
PAST_LEN = 4096
Z_MAIN = 8 * HEAD_W
N_GATE = 2 * HEADS
LANE = 128
SUBLANE = 8
REST_W = 6 * HEAD_W + LANE
VMEM_LIMIT = 62 * 1024 * 1024
POST_ROWS = 1024
POST_FF_COLS = 2048


def _split3(x):
    hi = x.astype(BF16)
    r = x - hi.astype(F32)
    mid = r.astype(BF16)
    lo = (r - mid.astype(F32)).astype(BF16)
    return hi, mid, lo


def _rms(x, g):
    return x * lax.rsqrt(jnp.mean(x * x, axis=-1, keepdims=True) + EPS) * g


def _softplus(x):
    return jnp.maximum(x, 0.0) + jnp.log1p(jnp.exp(-jnp.abs(x)))


def _sigmoid(x):
    return 0.5 + 0.5 * jnp.tanh(0.5 * x)


def _silu(x):
    h = 0.5 * x
    return h + h * jnp.tanh(h)


def _in_proj_kernel(x_ref, g_ref, w_ref, wba_ref, cos_ref, sin_ref, convw_ref, conv0_ref,
                    conv_all_ref, qk_ref, rest_ref, zt_ref, convnew_ref, hist_s):
    del conv_all_ref
    j = pl.program_id(1)
    nb, tm, _ = x_ref.shape
    xn = _rms(x_ref[...].reshape(nb * tm, D_MODEL), g_ref[...]).astype(BF16)
    pair_w = 2 * HEAD_DIM

    def proj(col, pair):
        lo = col * HEAD_W + pair * pair_w
        return jnp.dot(xn, w_ref[:, lo:lo + pair_w], preferred_element_type=F32)

    @pl.when(j == 0)
    def _():
        hist_s[...] = jnp.zeros_like(hist_s)
        hist_s[:, SUBLANE - (CONV_W - 1):SUBLANE, :] = conv0_ref[...]

    def rope(col, h, z, r):
        scale = 1.0 if col == 0 else HEAD_DIM ** -0.5
        out = (z * cos_ref[...] + pltpu.roll(z, HEAD_DIM // 2, 1) * sin_ref[...]) * scale
        lo = col * HEAD_W + h * HEAD_DIM
        qk_ref[r, :, lo:lo + HEAD_DIM] = out.astype(BF16)

    def conv(col, h, z, r):
        n = col - 4
        cs = slice(n * HEAD_W + h * HEAD_DIM, n * HEAD_W + (h + 1) * HEAD_DIM)
        xp = jnp.concatenate([hist_s[r, :, cs], z], axis=0)
        acc = z * convw_ref[CONV_W - 1:CONV_W, cs]
        for i in range(CONV_W - 1):
            acc = acc + pltpu.roll(xp, CONV_W - 1 - i, 0)[SUBLANE:] * convw_ref[i:i + 1, cs]
        hist_s[r, :, cs] = z[tm - SUBLANE:, :]
        c = _silu(acc)
        if n == 0:
            c = c * lax.rsqrt(jnp.sum(c * c, axis=-1, keepdims=True) + EPS) * (HEAD_DIM ** -0.5)
        elif n == 1:
            c = c * lax.rsqrt(jnp.sum(c * c, axis=-1, keepdims=True) + EPS)
        lo = (col - 2) * HEAD_W + h * HEAD_DIM
        rest_ref[r, :, lo:lo + HEAD_DIM] = c

    def plain(col, h, z, r):
        rest_ref[r, :, h * HEAD_DIM:(h + 1) * HEAD_DIM] = z

    def gate(col, h, z, r):
        lo = (HEAD_W if col == 3 else 5 * HEAD_W) + h * HEAD_DIM
        rest_ref[r, :, lo:lo + HEAD_DIM] = _silu(z)

    finishers = {0: rope, 1: rope, 2: plain, 3: gate, 4: conv, 5: conv, 6: conv, 7: gate}

    def finish(col, pair, z):
        for hh in range(2):
            for r in range(nb):
                finishers[col](col, 2 * pair + hh,
                               z[r * tm:(r + 1) * tm, hh * HEAD_DIM:(hh + 1) * HEAD_DIM], r)

    order = ((4, 0), (0, 0), (4, 1), (0, 1), (5, 0), (1, 0), (5, 1), (1, 1),
             (6, 0), (3, 0), (6, 1), (3, 1), (7, 0), (7, 1), (2, 0), (2, 1))
    pending = None
    for col, pair in order:
        z = proj(col, pair)
        if pending is not None:
            finish(*pending)
        pending = (col, pair, z)
    finish(*pending)

    ba = jnp.dot(xn, wba_ref[...], preferred_element_type=F32)
    zt = ba.T[:SUBLANE]
    for r in range(nb):
        rest_ref[r, :, 6 * HEAD_W:] = ba[r * tm:(r + 1) * tm]
        zt_ref[r] = zt[:, r * tm:(r + 1) * tm]

    @pl.when(j == pl.num_programs(1) - 1)
    def _():
        convnew_ref[...] = hist_s[:, SUBLANE - (CONV_W - 1):SUBLANE, :]


def _in_proj(x, l, n1g, w, wba, cos_t, sin_t, conv_w, conv0, l0, conv_all, nb, tm):
    bsz, t, _ = x.shape

    def layer(shape):
        return pl.BlockSpec((None,) + shape, lambda i, j: (l,) + (0,) * len(shape))

    in_specs = [
        pl.BlockSpec((nb, tm, D_MODEL), lambda i, j: (i, j, 0)),
        layer((1, D_MODEL)),
        pl.BlockSpec((D_MODEL, Z_MAIN), lambda i, j: (l, 0)),
        layer((D_MODEL, LANE)),
        pl.BlockSpec((tm, LANE), lambda i, j: (j, 0)),
        pl.BlockSpec((tm, LANE), lambda i, j: (j, 0)),
        layer((CONV_W, CONV_CH)),
        pl.BlockSpec((None, nb, CONV_W - 1, CONV_CH), lambda i, j: (l0, i, 0, 0)),
        pl.BlockSpec(memory_space=pl.ANY),
    ]
    return pl.pallas_call(
        _in_proj_kernel,
        grid=(bsz // nb, t // tm),
        in_specs=in_specs,
        out_specs=[
            pl.BlockSpec((nb, tm, 2 * HEAD_W), lambda i, j: (i, j, 0)),
            pl.BlockSpec((nb, tm, REST_W), lambda i, j: (i, j, 0)),
            pl.BlockSpec((nb, SUBLANE, tm), lambda i, j: (i, 0, j)),
            pl.BlockSpec((None, nb, CONV_W - 1, CONV_CH), lambda i, j: (l, i, 0, 0)),
        ],
        out_shape=[
            jax.ShapeDtypeStruct((bsz, t, 2 * HEAD_W), BF16),
            jax.ShapeDtypeStruct((bsz, t, REST_W), F32),
            jax.ShapeDtypeStruct((bsz, SUBLANE, t), F32),
            jax.ShapeDtypeStruct(conv_all.shape, F32),
        ],
        input_output_aliases={len(in_specs) - 1: 3},
        scratch_shapes=[pltpu.VMEM((nb, SUBLANE, CONV_CH), F32)],
        compiler_params=pltpu.CompilerParams(
            dimension_semantics=("arbitrary", "arbitrary"), vmem_limit_bytes=VMEM_LIMIT),
        name="in_proj",
    )(x, n1g, w, wba, cos_t, sin_t, conv_w, conv0, conv_all)


def _bmm(a, b):
    return jnp.einsum("gmk,gkn->gmn", a.astype(BF16), b.astype(BF16), preferred_element_type=F32)


def _bmm_nt(a, b):
    return jnp.einsum("gmk,gnk->gmn", a.astype(BF16), b.astype(BF16), preferred_element_type=F32)


def _bmm_tn(a, b):
    return jnp.einsum("gkm,gkn->gmn", a.astype(BF16), b.astype(BF16), preferred_element_type=F32)


def _unit_lower_inverse(a):
    n = a.shape[-1]
    ri = lax.broadcasted_iota(jnp.int32, (n, n), 0)
    ci = lax.broadcasted_iota(jnp.int32, (n, n), 1)
    eye = (ri == ci).astype(F32)[None]
    inner = ((ri // 2) == (ci // 2))[None]
    p = eye - jnp.where(inner, a, 0.0)
    size = 2
    while size < n:
        size *= 2
        outer = ((ri // size) == (ci // size))[None]
        e = jnp.where(outer & jnp.logical_not(inner), a, 0.0)
        p = p - _bmm(p, _bmm(e, p))
        inner = outer
    return p


def _mixer_kernel(qr_ref, kr_ref, va_ref, ga_ref, qb_ref, kb_ref, vb_ref, gb_ref, ba_ref, zt_ref,
                  intra_ref, rc_ref, hpc_ref, hpr_ref, retg_ref, dng_ref, sret0_ref, sdn0_ref,
                  sret_all_ref, sdn_all_ref, mix_ref, sret_out, sdn_out, sret_s, sdn_s, *, bb, cb):
    del sret_all_ref, sdn_all_ref
    step = pl.program_id(1)
    ng = bb * HEADS
    state_shape = (ng, HEAD_DIM, HEAD_DIM)

    @pl.when(step == 0)
    def _():
        sret_s[...] = sret0_ref[...].reshape(state_shape)
        sdn_s[...] = sdn0_ref[...].reshape(state_shape)

    ri = lax.broadcasted_iota(jnp.int32, (CHUNK, CHUNK), 0)
    ci = lax.broadcasted_iota(jnp.int32, (CHUNK, CHUNK), 1)
    causal = ri >= ci
    strict = ri > ci
    ltri = causal.astype(BF16)
    utri = (ri <= ci).astype(BF16)
    dot = functools.partial(jnp.dot, preferred_element_type=F32)

    hpc = hpc_ref[...]
    hpr = hpr_ref[...]
    neg_a_col = -jnp.exp(hpc[0:1, :])
    neg_a_row = -jnp.exp(hpr[:, 0:1])
    dn_g = dng_ref[...]
    ret_g = retg_ref[...]
    heads = [slice(h * HEAD_DIM, (h + 1) * HEAD_DIM) for h in range(HEADS)]
    reps = cb * bb
    intra = jnp.concatenate([intra_ref[...]] * reps, axis=0)
    inject = jnp.concatenate([rc_ref[1]] * reps, axis=0)
    cross = jnp.concatenate([rc_ref[0]] * bb, axis=0)
    carry = jnp.concatenate([rc_ref[2, :, 0:1, :]] * bb, axis=0)
    ret_gain = jnp.stack([ret_g[:, hs] for hs in heads] * reps)

    def lanes(col, j):
        return jnp.broadcast_to(col[:, j:j + 1], (CHUNK, HEAD_DIM))

    def prepare(chunks):
        def heads_of(ref):
            return jnp.stack([ref[b, pl.ds(c * CHUNK, CHUNK), hs]
                              for c in chunks for b in range(bb) for hs in heads])

        beta, gc_c, gc_r = [], [], []
        for c in chunks:
            r0 = c * CHUNK
            for b in range(bb):
                ba = ba_ref[b, pl.ds(r0, CHUNK), :]
                zt = zt_ref[b, :, r0:r0 + CHUNK]
                beta_col = _sigmoid(ba)
                g_col = neg_a_col * _softplus(ba + hpc[1:2, :])
                g_row = neg_a_row * _softplus(zt + hpr[:, 1:2])
                gh, gm, gl = _split3(g_col)
                gc_col = dot(ltri, gh) + (dot(ltri, gm) + dot(ltri, gl))
                gh, gm, gl = _split3(g_row)
                gc_row = dot(gh, utri) + (dot(gm, utri) + dot(gl, utri))
                for h in range(HEADS):
                    beta.append(lanes(beta_col, h))
                    gc_c.append(lanes(gc_col, HEADS + h))
                    gc_r.append(gc_row[HEADS + h:HEADS + h + 1, :])

        q_r, k_r, v_r = heads_of(qr_ref), heads_of(kr_ref), heads_of(va_ref)
        o_intra = _bmm(_bmm_nt(q_r, k_r) * intra, v_r)
        s_inc = _bmm_tn(k_r, v_r * inject)

        q, k, v = heads_of(qb_ref), heads_of(kb_ref), heads_of(vb_ref)
        beta, gc_c = jnp.stack(beta), jnp.stack(gc_c)
        gc_last = gc_c[:, CHUNK - 1:CHUNK, :]
        e_gc = jnp.exp(gc_c)
        decay = jnp.exp(jnp.where(causal[None], gc_c[:, :, :CHUNK] - jnp.stack(gc_r), -jnp.inf))
        kbeta = k * beta
        kk = _bmm_nt(jnp.concatenate([kbeta, q], axis=1), k)
        a = jnp.where(strict[None], kk[:, :CHUNK] * decay, 0.0)
        tinv = _unit_lower_inverse(a)
        sol = _bmm(tinv, jnp.concatenate([v * beta, kbeta * e_gc], axis=2))
        return dict(
            q_r=q_r, o_intra=o_intra, s_inc=s_inc, attn=kk[:, CHUNK:] * decay,
            u=sol[:, :, :HEAD_DIM],
            wq=jnp.concatenate([sol[:, :, HEAD_DIM:], q * e_gc], axis=1),
            kg=k * jnp.exp(gc_last - gc_c), g_end=jnp.exp(gc_last),
            gate_a=heads_of(ga_ref), gate_b=heads_of(gb_ref))

    groups = [list(range(cb))]
    prepared = [prepare(chunks) for chunks in groups]

    s_ret = sret_s[...]
    s_dn = sdn_s[...]
    for chunks, pre in zip(groups, prepared):
        o_a, o_b = [], []
        for n in range(len(chunks)):
            sl = slice(n * ng, (n + 1) * ng)
            o_a.append(pre["o_intra"][sl] + _bmm(pre["q_r"][sl], s_ret) * cross)
            s_ret = s_ret * carry + pre["s_inc"][sl]
            ws = _bmm(pre["wq"][sl], s_dn)
            v_new = pre["u"][sl] - ws[:, :CHUNK]
            o_b.append(ws[:, CHUNK:] + _bmm(pre["attn"][sl], v_new))
            s_dn = s_dn * pre["g_end"][sl] + _bmm_tn(pre["kg"][sl], v_new)
        o = jnp.concatenate(o_a, axis=0)
        mu = jnp.mean(o, axis=-1, keepdims=True)
        oc = o - mu
        var = jnp.mean(oc * oc, axis=-1, keepdims=True)
        o_a = oc * lax.rsqrt(var + EPS) * ret_gain * pre["gate_a"]
        o_b = _rms(jnp.concatenate(o_b, axis=0), dn_g[None]) * pre["gate_b"]
        for n, c in enumerate(chunks):
            rows = pl.ds(c * CHUNK, CHUNK)
            for b in range(bb):
                for h, hs in enumerate(heads):
                    g = (n * bb + b) * HEADS + h
                    mix_ref[b, rows, hs] = o_a[g].astype(BF16)
                    mix_ref[b, rows, HEAD_W + h * HEAD_DIM:HEAD_W + (h + 1) * HEAD_DIM] = (
                        o_b[g].astype(BF16))
    sret_s[...] = s_ret
    sdn_s[...] = s_dn

    @pl.when(step == pl.num_programs(1) - 1)
    def _():
        sret_out[...] = sret_s[...].reshape(sret_out.shape)
        sdn_out[...] = sdn_s[...].reshape(sdn_out.shape)


def _mixer(qk, rest, zt, l, intra, rc, hpc, hpr, ret_g, dn_g, sret0, sdn0, l0, sret_all, sdn_all, bb,
           cb):
    bsz, t, _ = qk.shape
    blk = cb * CHUNK
    grid = (bsz // bb, t // blk)

    def cols(col):
        return pl.BlockSpec((bb, blk, HEAD_W), lambda i, j: (i, j, col))

    def const(shape):
        return pl.BlockSpec(shape, lambda i, j: (0,) * len(shape))

    def layer(shape):
        return pl.BlockSpec((None,) + shape, lambda i, j: (l,) + (0,) * len(shape))

    state_shape = (bb, HEADS, HEAD_DIM, HEAD_DIM)
    state_in = pl.BlockSpec((None,) + state_shape, lambda i, j: (l0, i, 0, 0, 0))
    state_out = pl.BlockSpec((None,) + state_shape, lambda i, j: (l, i, 0, 0, 0))
    in_place = pl.BlockSpec(memory_space=pl.ANY)
    in_specs = [cols(0), cols(1)] + [cols(c) for c in range(6)] + [
        pl.BlockSpec((bb, blk, LANE), lambda i, j: (i, j, 6 * HEAD_W // LANE)),
        pl.BlockSpec((bb, SUBLANE, blk), lambda i, j: (i, 0, j)),
        const((HEADS, CHUNK, CHUNK)),
        const((3, HEADS, CHUNK, LANE)),
        layer((SUBLANE, LANE)),
        layer((SUBLANE, LANE)),
        layer((1, HEAD_W)),
        layer((1, HEAD_DIM)),
        state_in, state_in, in_place, in_place,
    ]
    kern = functools.partial(_mixer_kernel, bb=bb, cb=cb)
    return pl.pallas_call(
        kern,
        grid=grid,
        in_specs=in_specs,
        out_specs=[
            pl.BlockSpec((bb, blk, 2 * HEAD_W), lambda i, j: (i, j, 0)),
            state_out, state_out,
        ],
        out_shape=[
            jax.ShapeDtypeStruct((bsz, t, 2 * HEAD_W), BF16),
            jax.ShapeDtypeStruct(sret_all.shape, F32),
            jax.ShapeDtypeStruct(sdn_all.shape, F32),
        ],
        input_output_aliases={len(in_specs) - 2: 1, len(in_specs) - 1: 2},
        scratch_shapes=[
            pltpu.VMEM((bb * HEADS, HEAD_DIM, HEAD_DIM), F32),
            pltpu.VMEM((bb * HEADS, HEAD_DIM, HEAD_DIM), F32),
        ],
        compiler_params=pltpu.CompilerParams(
            dimension_semantics=("arbitrary", "arbitrary"), vmem_limit_bytes=VMEM_LIMIT),
        name="mixer",
    )(qk, qk, *([rest] * 7), zt, intra, rc, hpc, hpr, ret_g, dn_g, sret0, sdn0, sret_all, sdn_all)


def _post_kernel(h_ref, mix_ref, p_ref, wout_ref, n2g_ref, wup_ref, wdown_ref, proj_ref,
                 pleg_ref, gatew_ref, gateb_ref, fing_ref, out_ref, acc_s, xn_s, *, final):
    j = pl.program_id(1)

    @pl.when(j == 0)
    def _():
        h1 = h_ref[...] + jnp.dot(mix_ref[...], wout_ref[...],
                                  preferred_element_type=F32)
        acc_s[...] = h1
        xn_s[...] = _rms(h1, n2g_ref[...]).astype(BF16)

    up = jnp.maximum(jnp.dot(xn_s[...], wup_ref[...], preferred_element_type=F32), 0.0)
    acc_s[...] += jnp.dot((up * up).astype(BF16), wdown_ref[...], preferred_element_type=F32)

    @pl.when(j == pl.num_programs(1) - 1)
    def _():
        h2 = acc_s[...]
        gate = _sigmoid(jnp.dot(_rms(h2, pleg_ref[...]).astype(BF16), gatew_ref[...],
                                preferred_element_type=F32) + gateb_ref[...])
        h3 = h2 + jnp.dot(p_ref[...].astype(BF16), proj_ref[...],
                          preferred_element_type=F32) * gate
        if final:
            h3 = _rms(h3, fing_ref[...])
        out_ref[...] = h3


def _post(h, mix, p, l, w_out, n2g, w_up, w_down, proj, pleg, gate_w, gate_b, fin_g, final):
    m = h.shape[0]
    tm, tf = POST_ROWS, POST_FF_COLS

    def row(width):
        return pl.BlockSpec((tm, width), lambda i, j: (i, 0))

    def layer(shape):
        return pl.BlockSpec((None,) + shape, lambda i, j: (l,) + (0,) * len(shape),
                            pipeline_mode=pl.Buffered(1))

    return pl.pallas_call(
        functools.partial(_post_kernel, final=final),
        grid=(m // tm, D_FF // tf),
        in_specs=[
            row(D_MODEL), row(2 * HEAD_W),
            pl.BlockSpec((None, tm, PLE_DIM), lambda i, j: (l, i, 0)),
            layer((2 * HEAD_W, D_MODEL)),
            layer((1, D_MODEL)),
            pl.BlockSpec((None, D_MODEL, tf), lambda i, j: (l, 0, j)),
            pl.BlockSpec((None, tf, D_MODEL), lambda i, j: (l, j, 0)),
            layer((PLE_DIM, D_MODEL)),
            layer((1, D_MODEL)),
            layer((D_MODEL, D_MODEL)),
            layer((1, D_MODEL)),
            pl.BlockSpec((1, D_MODEL), lambda i, j: (0, 0)),
        ],
        out_specs=row(D_MODEL),
        out_shape=jax.ShapeDtypeStruct((m, D_MODEL), F32),
        scratch_shapes=[pltpu.VMEM((tm, D_MODEL), F32), pltpu.VMEM((tm, D_MODEL), BF16)],
        compiler_params=pltpu.CompilerParams(
            dimension_semantics=("arbitrary", "arbitrary"), vmem_limit_bytes=VMEM_LIMIT),
        name="post",
    )(h, mix, p, w_out, n2g, w_up, w_down, proj, pleg, gate_w, gate_b, fin_g)


def _rope_tables(pos0, t):
    half = HEAD_DIM // 2
    step = min(t, LANE)
    inv = ROPE_BASE ** (-jnp.arange(half, dtype=F32) / half)
    inv = jnp.concatenate([inv, inv])
    sign = jnp.concatenate([-jnp.ones((half,), F32), jnp.ones((half,), F32)])
    ang_hi = (pos0 + step * jnp.arange(t // step)).astype(F32)[:, None] * inv[None, :]
    ang_lo = jnp.arange(step).astype(F32)[:, None] * inv[None, :]
    ch, sh = jnp.cos(ang_hi)[:, None, :], jnp.sin(ang_hi)[:, None, :]
    cl, sl = jnp.cos(ang_lo)[None, :, :], jnp.sin(ang_lo)[None, :, :]
    cos = (ch * cl - sh * sl).reshape(t, HEAD_DIM)
    sin = ((sh * cl + ch * sl) * sign).reshape(t, HEAD_DIM)
    return cos, sin


def _retention_tables():
    log_gamma = jnp.log(1.0 - 2.0 ** (-5.0 - jnp.arange(HEADS, dtype=F32)))
    idx = jnp.arange(CHUNK, dtype=F32)
    intra = jnp.exp(jnp.abs(idx[:, None] - idx[None, :])[None] * log_gamma[:, None, None])
    cross = jnp.exp((idx + 1.0)[None, :] * log_gamma[:, None])
    inject = jnp.exp((CHUNK - 1.0 - idx)[None, :] * log_gamma[:, None])
    carry = jnp.exp(CHUNK * log_gamma)
    carry = jnp.broadcast_to(carry[:, None], (HEADS, CHUNK))
    rc = jnp.broadcast_to(jnp.stack([cross, inject, carry])[..., None], (3, HEADS, CHUNK, LANE))
    return intra, rc


def _trunk(x, p, conv0, sret0, sdn0, zero_state_layer, pos0, wts, intra, rc, bb, cb, nb_in, tm_in):
    bsz, t, _ = x.shape
    m = bsz * t
    depth = wts["w_out"].shape[0]
    cos_t, sin_t = _rope_tables(pos0, t)
    p = p.reshape(depth, m, PLE_DIM)
    h = x.reshape(m, D_MODEL)
    rets = jnp.zeros((depth, bsz, HEADS, HEAD_DIM, HEAD_DIM), F32)
    dns = jnp.zeros((depth, bsz, HEADS, HEAD_DIM, HEAD_DIM), F32)
    convs = jnp.zeros((depth, bsz, CONV_W - 1, CONV_CH), F32)
    for l in range(depth):
        l0 = 0 if zero_state_layer else l
        qk, rest, zt, convs = _in_proj(h.reshape(bsz, t, D_MODEL), l, wts["n1g"], wts["w_in"],
                                       wts["w_ba"], cos_t, sin_t, wts["conv_w"],
                                       conv0, l0, convs, nb_in, tm_in)
        mix, rets, dns = _mixer(qk, rest, zt, l, intra, rc, wts["hpc"], wts["hpr"], wts["ret_g"],
                                wts["dn_g"], sret0, sdn0, l0, rets, dns, bb, cb)
        h = _post(h, mix.reshape(m, 2 * HEAD_W), p, l, wts["w_out"], wts["n2g"], wts["w_up"],
                  wts["w_down"], wts["proj"], wts["pleg"], wts["gate_w"], wts["gate_b"],
                  wts["fin_g"], l == depth - 1)
    return h.reshape(bsz, t, D_MODEL), rets, dns, convs


def kernel(x_prompt, x_sample, p_prompt, p_sample, state_ret, state_delta, state_conv, norm1_g, w_in, conv_w, dt_bias, a_log, ret_norm_g, dn_norm_g, w_out, norm2_g, w_up, w_down, ple_proj, ple_norm_g, ple_gate_w, ple_gate_b, final_norm_g):
    depth = w_in.shape[0]
    w_in_b = w_in[:, :, :Z_MAIN].astype(BF16).reshape(depth * D_MODEL, Z_MAIN)
    w_gate_cols = w_in[:, :, Z_MAIN:].astype(BF16)
    hpc = jnp.zeros((depth, SUBLANE, LANE), F32)
    hpc = hpc.at[:, 0, HEADS:N_GATE].set(a_log).at[:, 1, HEADS:N_GATE].set(dt_bias)
    hpr = jnp.zeros((depth, SUBLANE, LANE), F32)
    hpr = hpr.at[:, HEADS:N_GATE, 0].set(a_log).at[:, HEADS:N_GATE, 1].set(dt_bias)
    wts = dict(
        n1g=norm1_g[:, None, :], w_in=w_in_b,
        w_ba=jnp.pad(w_gate_cols, ((0, 0), (0, 0), (0, LANE - N_GATE))),
        conv_w=conv_w, hpc=hpc, hpr=hpr,
        ret_g=ret_norm_g[:, None, :], dn_g=dn_norm_g[:, None, :],
        w_out=w_out.astype(BF16), n2g=norm2_g[:, None, :],
        w_up=w_up.astype(BF16), w_down=w_down.astype(BF16),
        proj=ple_proj.astype(BF16), pleg=ple_norm_g[:, None, :],
        gate_w=ple_gate_w.astype(BF16), gate_b=ple_gate_b[:, None, :],
        fin_g=final_norm_g[None, :])
    intra, rc = _retention_tables()
    bp = x_prompt.shape[0]
    zero_state = jnp.zeros((1, bp, HEADS, HEAD_DIM, HEAD_DIM), F32)
    zero_conv = jnp.zeros((1, bp, CONV_W - 1, CONV_CH), F32)
    y_p, ret_p, dn_p, conv_p = _trunk(x_prompt, p_prompt, zero_conv, zero_state, zero_state, True, 0,
                                      wts, intra, rc, bb=2, cb=4, nb_in=1, tm_in=1024)
    y_s, ret_s, dn_s, conv_s = _trunk(x_sample, p_sample, state_conv, state_ret, state_delta, False,
                                      PAST_LEN, wts, intra, rc, bb=8, cb=1, nb_in=8, tm_in=CHUNK)
    return (y_p, y_s, ret_p, dn_p, conv_p, ret_s, dn_s, conv_s)
```

```python
import functools

import jax
import jax.numpy as jnp
from jax import lax
from jax.experimental import pallas as pl
from jax.experimental.pallas import tpu as pltpu

F32 = jnp.float32
BF16 = jnp.bfloat16

D_MODEL = 1024
CHUNK = 64
HEADS = 4
HEAD_DIM = 128
HEAD_W = HEADS * HEAD_DIM
PLE_DIM = 256
D_FF = 4 * D_MODEL
CONV_W = 4
CONV_CH = 3 * HEAD_W
ROPE_BASE = 10000.0
EPS = 1e-6
PAST_LEN = 4096
Z_MAIN = 8 * HEAD_W
N_GATE = 2 * HEADS
LANE = 128
SUBLANE = 8
REST_W = 6 * HEAD_W + LANE
VMEM_LIMIT = 62 * 1024 * 1024
POST_ROWS = 1024
POST_FF_COLS = 2048


def _split3(x):
    hi = x.astype(BF16)
    r = x - hi.astype(F32)
    mid = r.astype(BF16)
    lo = (r - mid.astype(F32)).astype(BF16)
    return hi, mid, lo


def _rms(x, g):
    return x * lax.rsqrt(jnp.mean(x * x, axis=-1, keepdims=True) + EPS) * g


def _softplus(x):
    return jnp.maximum(x, 0.0) + jnp.log1p(jnp.exp(-jnp.abs(x)))


def _sigmoid(x):
    return 0.5 + 0.5 * jnp.tanh(0.5 * x)


def _silu(x):
    h = 0.5 * x
    return h + h * jnp.tanh(h)


def _in_proj_kernel(x_ref, g_ref, w_ref, wba_ref, cos_ref, sin_ref, convw_ref, conv0_ref,
                    conv_all_ref, qk_ref, rest_ref, zt_ref, convnew_ref, hist_s):
    del conv_all_ref
    j = pl.program_id(1)
    nb, tm, _ = x_ref.shape
    xn = _rms(x_ref[...].reshape(nb * tm, D_MODEL), g_ref[...]).astype(BF16)
    pair_w = 2 * HEAD_DIM

    def proj(col, pair):
        lo = col * HEAD_W + pair * pair_w
        return jnp.dot(xn, w_ref[:, lo:lo + pair_w], preferred_element_type=F32)

    @pl.when(j == 0)
    def _():
        hist_s[...] = jnp.zeros_like(hist_s)
        hist_s[:, SUBLANE - (CONV_W - 1):SUBLANE, :] = conv0_ref[...]

    def rope(col, h, z, r):
        scale = 1.0 if col == 0 else HEAD_DIM ** -0.5
        out = (z * cos_ref[...] + pltpu.roll(z, HEAD_DIM // 2, 1) * sin_ref[...]) * scale
        lo = col * HEAD_W + h * HEAD_DIM
        qk_ref[r, :, lo:lo + HEAD_DIM] = out.astype(BF16)

    def conv(col, h, z, r):
        n = col - 4
        cs = slice(n * HEAD_W + h * HEAD_DIM, n * HEAD_W + (h + 1) * HEAD_DIM)
        xp = jnp.concatenate([hist_s[r, :, cs], z], axis=0)
        acc = z * convw_ref[CONV_W - 1:CONV_W, cs]
        for i in range(CONV_W - 1):
            acc = acc + pltpu.roll(xp, CONV_W - 1 - i, 0)[SUBLANE:] * convw_ref[i:i + 1, cs]
        hist_s[r, :, cs] = z[tm - SUBLANE:, :]
        c = _silu(acc)
        if n == 0:
            c = c * lax.rsqrt(jnp.sum(c * c, axis=-1, keepdims=True) + EPS) * (HEAD_DIM ** -0.5)
        elif n == 1:
            c = c * lax.rsqrt(jnp.sum(c * c, axis=-1, keepdims=True) + EPS)
        lo = (col - 2) * HEAD_W + h * HEAD_DIM
        rest_ref[r, :, lo:lo + HEAD_DIM] = c

    def plain(col, h, z, r):
        rest_ref[r, :, h * HEAD_DIM:(h + 1) * HEAD_DIM] = z

    def gate(col, h, z, r):
        lo = (HEAD_W if col == 3 else 5 * HEAD_W) + h * HEAD_DIM
        rest_ref[r, :, lo:lo + HEAD_DIM] = _silu(z)

    finishers = {0: rope, 1: rope, 2: plain, 3: gate, 4: conv, 5: conv, 6: conv, 7: gate}

    def finish(col, pair, z):
        for hh in range(2):
            for r in range(nb):
                finishers[col](col, 2 * pair + hh,
                               z[r * tm:(r + 1) * tm, hh * HEAD_DIM:(hh + 1) * HEAD_DIM], r)

    order = ((4, 0), (0, 0), (4, 1), (0, 1), (5, 0), (1, 0), (5, 1), (1, 1),
             (6, 0), (3, 0), (6, 1), (3, 1), (7, 0), (7, 1), (2, 0), (2, 1))
    pending = None
    for col, pair in order:
        z = proj(col, pair)
        if pending is not None:
            finish(*pending)
        pending = (col, pair, z)
    finish(*pending)

    ba = jnp.dot(xn, wba_ref[...], preferred_element_type=F32)
    zt = ba.T[:SUBLANE]
    for r in range(nb):
        rest_ref[r, :, 6 * HEAD_W:] = ba[r * tm:(r + 1) * tm]
        zt_ref[r] = zt[:, r * tm:(r + 1) * tm]

    @pl.when(j == pl.num_programs(1) - 1)
    def _():
        convnew_ref[...] = hist_s[:, SUBLANE - (CONV_W - 1):SUBLANE, :]


def _in_proj(x, l, n1g, w, wba, cos_t, sin_t, conv_w, conv0, l0, conv_all, nb, tm):
    bsz, t, _ = x.shape

    def layer(shape):
        return pl.BlockSpec((None,) + shape, lambda i, j: (l,) + (0,) * len(shape))

    in_specs = [
        pl.BlockSpec((nb, tm, D_MODEL), lambda i, j: (i, j, 0)),
        layer((1, D_MODEL)),
        pl.BlockSpec((D_MODEL, Z_MAIN), lambda i, j: (l, 0)),
        layer((D_MODEL, LANE)),
        pl.BlockSpec((tm, LANE), lambda i, j: (j, 0)),
        pl.BlockSpec((tm, LANE), lambda i, j: (j, 0)),
        layer((CONV_W, CONV_CH)),
        pl.BlockSpec((None, nb, CONV_W - 1, CONV_CH), lambda i, j: (l0, i, 0, 0)),
        pl.BlockSpec(memory_space=pl.ANY),
    ]
    return pl.pallas_call(
        _in_proj_kernel,
        grid=(bsz // nb, t // tm),
        in_specs=in_specs,
        out_specs=[
            pl.BlockSpec((nb, tm, 2 * HEAD_W), lambda i, j: (i, j, 0)),
            pl.BlockSpec((nb, tm, REST_W), lambda i, j: (i, j, 0)),
            pl.BlockSpec((nb, SUBLANE, tm), lambda i, j: (i, 0, j)),
            pl.BlockSpec((None, nb, CONV_W - 1, CONV_CH), lambda i, j: (l, i, 0, 0)),
        ],
        out_shape=[
            jax.ShapeDtypeStruct((bsz, t, 2 * HEAD_W), BF16),
            jax.ShapeDtypeStruct((bsz, t, REST_W), F32),
            jax.ShapeDtypeStruct((bsz, SUBLANE, t), F32),
            jax.ShapeDtypeStruct(conv_all.shape, F32),
        ],
        input_output_aliases={len(in_specs) - 1: 3},
        scratch_shapes=[pltpu.VMEM((nb, SUBLANE, CONV_CH), F32)],
        compiler_params=pltpu.CompilerParams(
            dimension_semantics=("arbitrary", "arbitrary"), vmem_limit_bytes=VMEM_LIMIT),
        name="in_proj",
    )(x, n1g, w, wba, cos_t, sin_t, conv_w, conv0, conv_all)


def _bmm(a, b):
    return jnp.einsum("gmk,gkn->gmn", a.astype(BF16), b.astype(BF16), preferred_element_type=F32)


def _bmm_nt(a, b):
    return jnp.einsum("gmk,gnk->gmn", a.astype(BF16), b.astype(BF16), preferred_element_type=F32)


def _bmm_tn(a, b):
    return jnp.einsum("gkm,gkn->gmn", a.astype(BF16), b.astype(BF16), preferred_element_type=F32)


def _unit_lower_inverse(a):
    n = a.shape[-1]
    ri = lax.broadcasted_iota(jnp.int32, (n, n), 0)
    ci = lax.broadcasted_iota(jnp.int32, (n, n), 1)
    eye = (ri == ci).astype(F32)[None]
    inner = ((ri // 2) == (ci // 2))[None]
    p = eye - jnp.where(inner, a, 0.0)
    size = 2
    while size < n:
        size *= 2
        outer = ((ri // size) == (ci // size))[None]
        e = jnp.where(outer & jnp.logical_not(inner), a, 0.0)
        p = p - _bmm(p, _bmm(e, p))
        inner = outer
    return p


def _mixer_kernel(qr_ref, kr_ref, va_ref, ga_ref, qb_ref, kb_ref, vb_ref, gb_ref, ba_ref, zt_ref,
                  intra_ref, rc_ref, hpc_ref, hpr_ref, retg_ref, dng_ref, sret0_ref, sdn0_ref,
                  sret_all_ref, sdn_all_ref, mix_ref, sret_out, sdn_out, sret_s, sdn_s, *, bb, cb):
    del sret_all_ref, sdn_all_ref
    step = pl.program_id(1)
    ng = bb * HEADS
    state_shape = (ng, HEAD_DIM, HEAD_DIM)

    @pl.when(step == 0)
    def _():
        sret_s[...] = sret0_ref[...].reshape(state_shape)
        sdn_s[...] = sdn0_ref[...].reshape(state_shape)

    ri = lax.broadcasted_iota(jnp.int32, (CHUNK, CHUNK), 0)
    ci = lax.broadcasted_iota(jnp.int32, (CHUNK, CHUNK), 1)
    causal = ri >= ci
    strict = ri > ci
    ltri = causal.astype(BF16)
    utri = (ri <= ci).astype(BF16)
    dot = functools.partial(jnp.dot, preferred_element_type=F32)

    hpc = hpc_ref[...]
    hpr = hpr_ref[...]
    neg_a_col = -jnp.exp(hpc[0:1, :])
    neg_a_row = -jnp.exp(hpr[:, 0:1])
    dn_g = dng_ref[...]
    ret_g = retg_ref[...]
    heads = [slice(h * HEAD_DIM, (h + 1) * HEAD_DIM) for h in range(HEADS)]
    reps = cb * bb
    intra = jnp.concatenate([intra_ref[...]] * reps, axis=0)
    inject = jnp.concatenate([rc_ref[1]] * reps, axis=0)
    cross = jnp.concatenate([rc_ref[0]] * bb, axis=0)
    carry = jnp.concatenate([rc_ref[2, :, 0:1, :]] * bb, axis=0)
    ret_gain = jnp.stack([ret_g[:, hs] for hs in heads] * reps)

    def lanes(col, j):
        return jnp.broadcast_to(col[:, j:j + 1], (CHUNK, HEAD_DIM))

    def prepare(chunks):
        def heads_of(ref):
            return jnp.stack([ref[b, pl.ds(c * CHUNK, CHUNK), hs]
                              for c in chunks for b in range(bb) for hs in heads])

        beta, gc_c, gc_r = [], [], []
        for c in chunks:
            r0 = c * CHUNK
            for b in range(bb):
                ba = ba_ref[b, pl.ds(r0, CHUNK), :]
                zt = zt_ref[b, :, r0:r0 + CHUNK]
                beta_col = _sigmoid(ba)
                g_col = neg_a_col * _softplus(ba + hpc[1:2, :])
                g_row = neg_a_row * _softplus(zt + hpr[:, 1:2])
                gh, gm, gl = _split3(g_col)
                gc_col = dot(ltri, gh) + (dot(ltri, gm) + dot(ltri, gl))
                gh, gm, gl = _split3(g_row)
                gc_row = dot(gh, utri) + (dot(gm, utri) + dot(gl, utri))
                for h in range(HEADS):
                    beta.append(lanes(beta_col, h))
                    gc_c.append(lanes(gc_col, HEADS + h))
                    gc_r.append(gc_row[HEADS + h:HEADS + h + 1, :])

        q_r, k_r, v_r = heads_of(qr_ref), heads_of(kr_ref), heads_of(va_ref)
        o_intra = _bmm(_bmm_nt(q_r, k_r) * intra, v_r)
        s_inc = _bmm_tn(k_r, v_r * inject)

        q, k, v = heads_of(qb_ref), heads_of(kb_ref), heads_of(vb_ref)
        beta, gc_c = jnp.stack(beta), jnp.stack(gc_c)
        gc_last = gc_c[:, CHUNK - 1:CHUNK, :]
        e_gc = jnp.exp(gc_c)
        decay = jnp.exp(jnp.where(causal[None], gc_c[:, :, :CHUNK] - jnp.stack(gc_r), -jnp.inf))
        kbeta = k * beta
        kk = _bmm_nt(jnp.concatenate([kbeta, q], axis=1), k)
        a = jnp.where(strict[None], kk[:, :CHUNK] * decay, 0.0)
        tinv = _unit_lower_inverse(a)
        sol = _bmm(tinv, jnp.concatenate([v * beta, kbeta * e_gc], axis=2))
        return dict(
            q_r=q_r, o_intra=o_intra, s_inc=s_inc, attn=kk[:, CHUNK:] * decay,
            u=sol[:, :, :HEAD_DIM],
            wq=jnp.concatenate([sol[:, :, HEAD_DIM:], q * e_gc], axis=1),
            kg=k * jnp.exp(gc_last - gc_c), g_end=jnp.exp(gc_last),
            gate_a=heads_of(ga_ref), gate_b=heads_of(gb_ref))

    groups = [list(range(cb))]
    prepared = [prepare(chunks) for chunks in groups]

    s_ret = sret_s[...]
    s_dn = sdn_s[...]
    for chunks, pre in zip(groups, prepared):
        o_a, o_b = [], []
        for n in range(len(chunks)):
            sl = slice(n * ng, (n + 1) * ng)
            o_a.append(pre["o_intra"][sl] + _bmm(pre["q_r"][sl], s_ret) * cross)
            s_ret = s_ret * carry + pre["s_inc"][sl]
            ws = _bmm(pre["wq"][sl], s_dn)
            v_new = pre["u"][sl] - ws[:, :CHUNK]
            o_b.append(ws[:, CHUNK:] + _bmm(pre["attn"][sl], v_new))
            s_dn = s_dn * pre["g_end"][sl] + _bmm_tn(pre["kg"][sl], v_new)
        o = jnp.concatenate(o_a, axis=0)
        mu = jnp.mean(o, axis=-1, keepdims=True)
        oc = o - mu
        var = jnp.mean(oc * oc, axis=-1, keepdims=True)
        o_a = oc * lax.rsqrt(var + EPS) * ret_gain * pre["gate_a"]
        o_b = _rms(jnp.concatenate(o_b, axis=0), dn_g[None]) * pre["gate_b"]
        for n, c in enumerate(chunks):
            rows = pl.ds(c * CHUNK, CHUNK)
            for b in range(bb):
                for h, hs in enumerate(heads):
                    g = (n * bb + b) * HEADS + h
                    mix_ref[b, rows, hs] = o_a[g].astype(BF16)
                    mix_ref[b, rows, HEAD_W + h * HEAD_DIM:HEAD_W + (h + 1) * HEAD_DIM] = (
                        o_b[g].astype(BF16))
    sret_s[...] = s_ret
    sdn_s[...] = s_dn

    @pl.when(step == pl.num_programs(1) - 1)
    def _():
        sret_out[...] = sret_s[...].reshape(sret_out.shape)
        sdn_out[...] = sdn_s[...].reshape(sdn_out.shape)


def _mixer(qk, rest, zt, l, intra, rc, hpc, hpr, ret_g, dn_g, sret0, sdn0, l0, sret_all, sdn_all, bb,
           cb):
    bsz, t, _ = qk.shape
    blk = cb * CHUNK
    grid = (bsz // bb, t // blk)

    def cols(col):
        return pl.BlockSpec((bb, blk, HEAD_W), lambda i, j: (i, j, col))

    def const(shape):
        return pl.BlockSpec(shape, lambda i, j: (0,) * len(shape))

    def layer(shape):
        return pl.BlockSpec((None,) + shape, lambda i, j: (l,) + (0,) * len(shape))

    state_shape = (bb, HEADS, HEAD_DIM, HEAD_DIM)
    state_in = pl.BlockSpec((None,) + state_shape, lambda i, j: (l0, i, 0, 0, 0))
    state_out = pl.BlockSpec((None,) + state_shape, lambda i, j: (l, i, 0, 0, 0))
    in_place = pl.BlockSpec(memory_space=pl.ANY)
    in_specs = [cols(0), cols(1)] + [cols(c) for c in range(6)] + [
        pl.BlockSpec((bb, blk, LANE), lambda i, j: (i, j, 6 * HEAD_W // LANE)),
        pl.BlockSpec((bb, SUBLANE, blk), lambda i, j: (i, 0, j)),
        const((HEADS, CHUNK, CHUNK)),
        const((3, HEADS, CHUNK, LANE)),
        layer((SUBLANE, LANE)),
        layer((SUBLANE, LANE)),
        layer((1, HEAD_W)),
        layer((1, HEAD_DIM)),
        state_in, state_in, in_place, in_place,
    ]
    kern = functools.partial(_mixer_kernel, bb=bb, cb=cb)
    return pl.pallas_call(
        kern,
        grid=grid,
        in_specs=in_specs,
        out_specs=[
            pl.BlockSpec((bb, blk, 2 * HEAD_W), lambda i, j: (i, j, 0)),
            state_out, state_out,
        ],
        out_shape=[
            jax.ShapeDtypeStruct((bsz, t, 2 * HEAD_W), BF16),
            jax.ShapeDtypeStruct(sret_all.shape, F32),
            jax.ShapeDtypeStruct(sdn_all.shape, F32),
        ],
        input_output_aliases={len(in_specs) - 2: 1, len(in_specs) - 1: 2},
        scratch_shapes=[
            pltpu.VMEM((bb * HEADS, HEAD_DIM, HEAD_DIM), F32),
            pltpu.VMEM((bb * HEADS, HEAD_DIM, HEAD_DIM), F32),
        ],
        compiler_params=pltpu.CompilerParams(
            dimension_semantics=("arbitrary", "arbitrary"), vmem_limit_bytes=VMEM_LIMIT),
        name="mixer",
    )(qk, qk, *([rest] * 7), zt, intra, rc, hpc, hpr, ret_g, dn_g, sret0, sdn0, sret_all, sdn_all)


def _post_kernel(h_ref, mix_ref, p_ref, wout_ref, n2g_ref, wup_ref, wdown_ref, proj_ref,
                 pleg_ref, gatew_ref, gateb_ref, fing_ref, out_ref, acc_s, xn_s, *, final):
    j = pl.program_id(1)

    @pl.when(j == 0)
    def _():
        h1 = h_ref[...] + jnp.dot(mix_ref[...], wout_ref[...],
                                  preferred_element_type=F32)
        acc_s[...] = h1
        xn_s[...] = _rms(h1, n2g_ref[...]).astype(BF16)

    up = jnp.maximum(jnp.dot(xn_s[...], wup_ref[...], preferred_element_type=F32), 0.0)
    acc_s[...] += jnp.dot((up * up).astype(BF16), wdown_ref[...], preferred_element_type=F32)

    @pl.when(j == pl.num_programs(1) - 1)
    def _():
        h2 = acc_s[...]
        gate = _sigmoid(jnp.dot(_rms(h2, pleg_ref[...]).astype(BF16), gatew_ref[...],
                                preferred_element_type=F32) + gateb_ref[...])
        h3 = h2 + jnp.dot(p_ref[...].astype(BF16), proj_ref[...],
                          preferred_element_type=F32) * gate
        if final:
            h3 = _rms(h3, fing_ref[...])
        out_ref[...] = h3


def _post(h, mix, p, l, w_out, n2g, w_up, w_down, proj, pleg, gate_w, gate_b, fin_g, final):
    m = h.shape[0]
    tm, tf = POST_ROWS, POST_FF_COLS

    def row(width):
        return pl.BlockSpec((tm, width), lambda i, j: (i, 0))

    def layer(shape):
        return pl.BlockSpec((None,) + shape, lambda i, j: (l,) + (0,) * len(shape),
                            pipeline_mode=pl.Buffered(1))

    return pl.pallas_call(
        functools.partial(_post_kernel, final=final),
        grid=(m // tm, D_FF // tf),
        in_specs=[
            row(D_MODEL), row(2 * HEAD_W),
            pl.BlockSpec((None, tm, PLE_DIM), lambda i, j: (l, i, 0)),
            layer((2 * HEAD_W, D_MODEL)),
            layer((1, D_MODEL)),
            pl.BlockSpec((None, D_MODEL, tf), lambda i, j: (l, 0, j)),
            pl.BlockSpec((None, tf, D_MODEL), lambda i, j: (l, j, 0)),
            layer((PLE_DIM, D_MODEL)),
            layer((1, D_MODEL)),
            layer((D_MODEL, D_MODEL)),
            layer((1, D_MODEL)),
            pl.BlockSpec((1, D_MODEL), lambda i, j: (0, 0)),
        ],
        out_specs=row(D_MODEL),
        out_shape=jax.ShapeDtypeStruct((m, D_MODEL), F32),
        scratch_shapes=[pltpu.VMEM((tm, D_MODEL), F32), pltpu.VMEM((tm, D_MODEL), BF16)],
        compiler_params=pltpu.CompilerParams(
            dimension_semantics=("arbitrary", "arbitrary"), vmem_limit_bytes=VMEM_LIMIT),
        name="post",
    )(h, mix, p, w_out, n2g, w_up, w_down, proj, pleg, gate_w, gate_b, fin_g)


def _rope_tables(pos0, t):
    half = HEAD_DIM // 2
    step = min(t, LANE)
    inv = ROPE_BASE ** (-jnp.arange(half, dtype=F32) / half)
    inv = jnp.concatenate([inv, inv])
    sign = jnp.concatenate([-jnp.ones((half,), F32), jnp.ones((half,), F32)])
    ang_hi = (pos0 + step * jnp.arange(t // step)).astype(F32)[:, None] * inv[None, :]
    ang_lo = jnp.arange(step).astype(F32)[:, None] * inv[None, :]
    ch, sh = jnp.cos(ang_hi)[:, None, :], jnp.sin(ang_hi)[:, None, :]
    cl, sl = jnp.cos(ang_lo)[None, :, :], jnp.sin(ang_lo)[None, :, :]
    cos = (ch * cl - sh * sl).reshape(t, HEAD_DIM)
    sin = ((sh * cl + ch * sl) * sign).reshape(t, HEAD_DIM)
    return cos, sin


def _retention_tables():
    log_gamma = jnp.log(1.0 - 2.0 ** (-5.0 - jnp.arange(HEADS, dtype=F32)))
    idx = jnp.arange(CHUNK, dtype=F32)
    intra = jnp.exp(jnp.abs(idx[:, None] - idx[None, :])[None] * log_gamma[:, None, None])
    cross = jnp.exp((idx + 1.0)[None, :] * log_gamma[:, None])
    inject = jnp.exp((CHUNK - 1.0 - idx)[None, :] * log_gamma[:, None])
    carry = jnp.exp(CHUNK * log_gamma)
    carry = jnp.broadcast_to(carry[:, None], (HEADS, CHUNK))
    rc = jnp.broadcast_to(jnp.stack([cross, inject, carry])[..., None], (3, HEADS, CHUNK, LANE))
    return intra, rc


def _trunk(x, p, conv0, sret0, sdn0, zero_state_layer, pos0, wts, intra, rc, bb, cb, nb_in, tm_in):
    bsz, t, _ = x.shape
    m = bsz * t
    depth = wts["w_out"].shape[0]
    cos_t, sin_t = _rope_tables(pos0, t)
    p = p.reshape(depth, m, PLE_DIM)
    h = x.reshape(m, D_MODEL)
    rets = jnp.zeros((depth, bsz, HEADS, HEAD_DIM, HEAD_DIM), F32)
    dns = jnp.zeros((depth, bsz, HEADS, HEAD_DIM, HEAD_DIM), F32)
    convs = jnp.zeros((depth, bsz, CONV_W - 1, CONV_CH), F32)
    for l in range(depth):
        l0 = 0 if zero_state_layer else l
        qk, rest, zt, convs = _in_proj(h.reshape(bsz, t, D_MODEL), l, wts["n1g"], wts["w_in"],
                                       wts["w_ba"], cos_t, sin_t, wts["conv_w"],
                                       conv0, l0, convs, nb_in, tm_in)
        mix, rets, dns = _mixer(qk, rest, zt, l, intra, rc, wts["hpc"], wts["hpr"], wts["ret_g"],
                                wts["dn_g"], sret0, sdn0, l0, rets, dns, bb, cb)
        h = _post(h, mix.reshape(m, 2 * HEAD_W), p, l, wts["w_out"], wts["n2g"], wts["w_up"],
                  wts["w_down"], wts["proj"], wts["pleg"], wts["gate_w"], wts["gate_b"],
                  wts["fin_g"], l == depth - 1)
    return h.reshape(bsz, t, D_MODEL), rets, dns, convs


def kernel(x_prompt, x_sample, p_prompt, p_sample, state_ret, state_delta, state_conv, norm1_g, w_in, conv_w, dt_bias, a_log, ret_norm_g, dn_norm_g, w_out, norm2_g, w_up, w_down, ple_proj, ple_norm_g, ple_gate_w, ple_gate_b, final_norm_g):
    depth = w_in.shape[0]
    w_in_b = w_in[:, :, :Z_MAIN].astype(BF16).reshape(depth * D_MODEL, Z_MAIN)
    w_gate_cols = w_in[:, :, Z_MAIN:].astype(BF16)
    hpc = jnp.zeros((depth, SUBLANE, LANE), F32)
    hpc = hpc.at[:, 0, HEADS:N_GATE].set(a_log).at[:, 1, HEADS:N_GATE].set(dt_bias)
    hpr = jnp.zeros((depth, SUBLANE, LANE), F32)
    hpr = hpr.at[:, HEADS:N_GATE, 0].set(a_log).at[:, HEADS:N_GATE, 1].set(dt_bias)
    wts = dict(
        n1g=norm1_g[:, None, :], w_in=w_in_b,
        w_ba=jnp.pad(w_gate_cols, ((0, 0), (0, 0), (0, LANE - N_GATE))),
        conv_w=conv_w, hpc=hpc, hpr=hpr,
        ret_g=ret_norm_g[:, None, :], dn_g=dn_norm_g[:, None, :],
        w_out=w_out.astype(BF16), n2g=norm2_g[:, None, :],
        w_up=w_up.astype(BF16), w_down=w_down.astype(BF16),
        proj=ple_proj.astype(BF16), pleg=ple_norm_g[:, None, :],
        gate_w=ple_gate_w.astype(BF16), gate_b=ple_gate_b[:, None, :],
        fin_g=final_norm_g[None, :])
    intra, rc = _retention_tables()
    bp = x_prompt.shape[0]
    zero_state = jnp.zeros((1, bp, HEADS, HEAD_DIM, HEAD_DIM), F32)
    zero_conv = jnp.zeros((1, bp, CONV_W - 1, CONV_CH), F32)
    y_p, ret_p, dn_p, conv_p = _trunk(x_prompt, p_prompt, zero_conv, zero_state, zero_state, True, 0,
                                      wts, intra, rc, bb=2, cb=8, nb_in=1, tm_in=1024)
    y_s, ret_s, dn_s, conv_s = _trunk(x_sample, p_sample, state_conv, state_ret, state_delta, False,
                                      PAST_LEN, wts, intra, rc, bb=8, cb=1, nb_in=8, tm_in=CHUNK)
    return (y_p, y_s, ret_p, dn_p, conv_p, ret_s, dn_s, conv_s)
```
